```python
import math
import jax, jax.numpy as jnp
from jax import lax
import numpy as np

D_MODEL = 1024
BATCH = 16
SEQ = 2048
DEPTH = 2

GRID_W = 64
CTX_LEN = 256
N_MIXERS = 2
N_HYENA = (DEPTH + 1) // 2
N_ATTN = DEPTH // 2
EPS = 1e-6

HY_STREAMS = 3
HY_SHORT = 3
HY_BANDS = 16
HY_EMB = 1 + 2 * HY_BANDS
HY_FILT_W = 64
HY_DECAY_TARGET = 1e-2
HY_FAST = 0.3
HY_SLOW = 1.5
HY_SHIFT = 0.0

HEAD_DIM = 128
N_HEADS = D_MODEL // HEAD_DIM
N_KV_HEADS = 2
GROUP = N_HEADS // N_KV_HEADS
QKV_DIM = (N_HEADS + 2 * N_KV_HEADS) * HEAD_DIM
ROPE_PAIRS = HEAD_DIM // 4
ROPE_THETA = 10000.0
Q_BLOCK = 128
ATTN_SCALE = HEAD_DIM ** -0.5

D_FF = 4 * D_MODEL

kernel_name = 'hyena_gqa_interleaved_dit_block'


def rms_norm(x, g):
    xf = x.astype(jnp.float32)
    y = xf * lax.rsqrt(jnp.mean(xf * xf, axis=-1, keepdims=True) + EPS)
    return (y * g).astype(x.dtype)


def modulation(cond, w, b):
    return jnp.split(jax.nn.silu(cond) @ w + b, 6, axis=-1)


def short_conv(z, w, b):
    L = z.shape[1]
    zp = jnp.pad(z, ((0, 0), (1, 1), (0, 0)))
    return zp[:, :L] * w[0] + zp[:, 1:L + 1] * w[1] + zp[:, 2:] * w[2] + b


def hyena_filters(L, w1, b1, f1, w2, b2, f2, w3):
    t = jnp.arange(L, dtype=jnp.float32) / L
    bands = jnp.linspace(1e-4, HY_BANDS - 1, HY_BANDS, dtype=jnp.float32)
    ang = 2.0 * math.pi * t[:, None] * bands[None, :]
    z = jnp.concatenate([t[:, None], jnp.cos(ang), jnp.sin(ang)], axis=-1)
    h = jnp.sin(f1 * (z @ w1 + b1))
    h = jnp.sin(f2 * (h @ w2 + b2))
    h = (h @ w3).astype(jnp.float32)
    deltas = jnp.abs(jnp.linspace(math.log(HY_DECAY_TARGET) / HY_SLOW,
                                  math.log(HY_DECAY_TARGET) / HY_FAST, D_MODEL, dtype=jnp.float32))
    decay = jnp.exp(-t[:, None] * deltas[None, :])
    h = h * (jnp.concatenate([decay, decay], axis=-1) + HY_SHIFT)
    return h[:, :D_MODEL], h[:, D_MODEL:]


def bidir_fftconv(u, h_f, h_b, bias):
    L = u.shape[1]
    k2 = jnp.concatenate([h_f, jnp.zeros((1, h_f.shape[1]), jnp.float32), h_b[:0:-1]], axis=0)
    kf = jnp.fft.rfft(k2, n=2 * L, axis=0)
    uf32 = u.astype(jnp.float32)
    uf = jnp.fft.rfft(uf32, n=2 * L, axis=1)
    y = jnp.fft.irfft(uf * kf[None], n=2 * L, axis=1)[:, :L]
    return (y + uf32 * bias).astype(u.dtype)


def hyena_mixer(h, w_in, b_in, conv_w, conv_b, fw1, fb1, ff1, fw2, fb2, ff2, fw3, fbias, w_out, b_out):
    L = h.shape[1]
    z = short_conv(h @ w_in + b_in, conv_w, conv_b)
    x1, x2, v = jnp.split(z, HY_STREAMS, axis=-1)
    h_f, h_b = hyena_filters(L, fw1, fb1, ff1, fw2, fb2, ff2, fw3)
    v = bidir_fftconv(v * x2, h_f, h_b, fbias)
    return (v * x1) @ w_out + b_out


def axial_rope_tables(rows):
    row = jnp.repeat(jnp.arange(rows, dtype=jnp.float32), GRID_W)
    col = jnp.tile(jnp.arange(GRID_W, dtype=jnp.float32), rows)
    inv = ROPE_THETA ** (-jnp.arange(ROPE_PAIRS, dtype=jnp.float32) / ROPE_PAIRS)
    ang = jnp.concatenate([row[:, None] * inv[None, :], col[:, None] * inv[None, :]], axis=-1)
    return jnp.cos(ang), jnp.sin(ang)


def apply_rope(x, cos, sin):
    xf = x.astype(jnp.float32).reshape(x.shape[:-1] + (HEAD_DIM // 2, 2))
    x0, x1 = xf[..., 0], xf[..., 1]
    out = jnp.stack([x0 * cos - x1 * sin, x0 * sin + x1 * cos], axis=-1)
    return out.reshape(x.shape).astype(x.dtype)


def project_qkv(h, w_qkv, q_gain, k_gain):
    B, L, _ = h.shape
    qkv = h @ w_qkv
    q = qkv[..., :N_HEADS * HEAD_DIM].reshape(B, L, N_KV_HEADS, GROUP, HEAD_DIM)
    k = qkv[..., N_HEADS * HEAD_DIM:(N_HEADS + N_KV_HEADS) * HEAD_DIM].reshape(B, L, N_KV_HEADS, HEAD_DIM)
    v = qkv[..., (N_HEADS + N_KV_HEADS) * HEAD_DIM:].reshape(B, L, N_KV_HEADS, HEAD_DIM)
    return rms_norm(q, q_gain), rms_norm(k, k_gain), v


def attend(q, k, v):
    s = jnp.einsum('bqkgd,bskd->bkgqs', q, k).astype(jnp.float32) * ATTN_SCALE
    p = jax.nn.softmax(s, axis=-1).astype(v.dtype)
    return jnp.einsum('bkgqs,bskd->bqkgd', p, v)


def gqa_mixer(h_lat, h_ctx, w_qkv, q_gain, k_gain, w_o, cos, sin, with_ctx_out):
    B, L, _ = h_lat.shape
    C = h_ctx.shape[1]
    q_l, k_l, v_l = project_qkv(h_lat, w_qkv, q_gain, k_gain)
    q_l = apply_rope(q_l, cos[:, None, None, :], sin[:, None, None, :])
    k_l = apply_rope(k_l, cos[:, None, :], sin[:, None, :])
    q_c, k_c, v_c = project_qkv(h_ctx, w_qkv, q_gain, k_gain)
    k_all = jnp.concatenate([k_l, k_c], axis=1)
    v_all = jnp.concatenate([v_l, v_c], axis=1)
    nb = L // Q_BLOCK
    q_blocks = jnp.moveaxis(q_l.reshape(B, nb, Q_BLOCK, N_KV_HEADS, GROUP, HEAD_DIM), 1, 0)
    o = lax.map(lambda qb: attend(qb, k_all, v_all), q_blocks)
    y_lat = jnp.moveaxis(o, 0, 1).reshape(B, L, N_HEADS * HEAD_DIM) @ w_o
    y_ctx = None
    if with_ctx_out:
        y_ctx = attend(q_c, k_c, v_c).reshape(B, C, N_HEADS * HEAD_DIM) @ w_o
    return y_lat, y_ctx


def sq_relu_mlp(h, w1, w2):
    return jnp.square(jax.nn.relu(h @ w1)) @ w2


def setup_inputs(seed: int = 0) -> dict:
    key = jax.random.key(seed)
    ks = iter(jax.random.split(key, 40))
    f32 = jnp.float32

    def nrm(shape, scale):
        return jax.random.normal(next(ks), shape, f32) * scale

    def gain(shape):
        return 1.0 + nrm(shape, 0.05)

    D = D_MODEL
    return {
        'x': nrm((BATCH, SEQ, D), 1.0),
        'c': nrm((BATCH, D), 1.0),
        'ctx': nrm((BATCH, CTX_LEN, D), 1.0),
        'c_ctx': nrm((D,), 1.0),
        'mod_w': nrm((DEPTH, D, 6 * D), 0.5 * D ** -0.5),
        'mod_b': nrm((DEPTH, 6 * D), 0.02),
        'mix_norm_pre': gain((DEPTH, D)),
        'mix_norm_post': gain((DEPTH, D)),
        'mlp_norm_pre': gain((DEPTH, D)),
        'mlp_norm_post': gain((DEPTH, D)),
        'mlp_w1': nrm((DEPTH, D, D_FF), D ** -0.5),
        'mlp_w2': nrm((DEPTH, D_FF, D), D_FF ** -0.5),
        'hy_w_in': nrm((N_HYENA, D, HY_STREAMS * D), D ** -0.5),
        'hy_b_in': nrm((N_HYENA, HY_STREAMS * D), 0.02),
        'hy_conv_w': nrm((N_HYENA, HY_SHORT, HY_STREAMS * D), HY_SHORT ** -0.5),
        'hy_conv_b': nrm((N_HYENA, HY_STREAMS * D), 0.02),
        'hy_filt_w1': nrm((N_HYENA, HY_EMB, HY_FILT_W), HY_EMB ** -0.5),
        'hy_filt_b1': nrm((N_HYENA, HY_FILT_W), 0.1),
        'hy_filt_freq1': gain((N_HYENA, HY_FILT_W)),
        'hy_filt_w2': nrm((N_HYENA, HY_FILT_W, HY_FILT_W), HY_FILT_W ** -0.5),
        'hy_filt_b2': nrm((N_HYENA, HY_FILT_W), 0.1),
        'hy_filt_freq2': gain((N_HYENA, HY_FILT_W)),
        'hy_filt_w3': nrm((N_HYENA, HY_FILT_W, 2 * D), HY_FILT_W ** -0.5),
        'hy_filt_bias': nrm((N_HYENA, D), 0.1),
        'hy_w_out': nrm((N_HYENA, D, D), D ** -0.5),
        'hy_b_out': nrm((N_HYENA, D), 0.02),
        'attn_w_qkv': nrm((N_ATTN, D, QKV_DIM), D ** -0.5),
        'attn_q_norm': gain((N_ATTN, HEAD_DIM)),
        'attn_k_norm': gain((N_ATTN, HEAD_DIM)),
        'attn_w_o': nrm((N_ATTN, N_HEADS * HEAD_DIM, D), D ** -0.5),
    }


def reference(x, c, ctx, c_ctx, mod_w, mod_b, mix_norm_pre, mix_norm_post, mlp_norm_pre, mlp_norm_post,
              mlp_w1, mlp_w2, hy_w_in, hy_b_in, hy_conv_w, hy_conv_b, hy_filt_w1, hy_filt_b1, hy_filt_freq1,
              hy_filt_w2, hy_filt_b2, hy_filt_freq2, hy_filt_w3, hy_filt_bias, hy_w_out, hy_b_out,
              attn_w_qkv, attn_q_norm, attn_k_norm, attn_w_o):
    L = x.shape[1]
    rows = L // GRID_W
    cos, sin = axial_rope_tables(rows)
    for i in range(DEPTH):
        last = i == DEPTH - 1
        j = i // N_MIXERS
        sh1, sc1, g1, sh2, sc2, g2 = [m[:, None, :] for m in modulation(c, mod_w[i], mod_b[i])]
        csh1, csc1, cg1, csh2, csc2, cg2 = modulation(c_ctx, mod_w[i], mod_b[i])

        hx = rms_norm(x, mix_norm_pre[i]) * (1.0 + sc1) + sh1
        hc = rms_norm(ctx, mix_norm_pre[i]) * (1.0 + csc1) + csh1
        if i % N_MIXERS == 0:
            hp = (hy_w_in[j], hy_b_in[j], hy_conv_w[j], hy_conv_b[j], hy_filt_w1[j], hy_filt_b1[j],
                  hy_filt_freq1[j], hy_filt_w2[j], hy_filt_b2[j], hy_filt_freq2[j], hy_filt_w3[j],
                  hy_filt_bias[j], hy_w_out[j], hy_b_out[j])
            yx = hyena_mixer(hx, *hp)
            yc = None if last else hyena_mixer(hc, *hp)
        else:
            yx, yc = gqa_mixer(hx, hc, attn_w_qkv[j], attn_q_norm[j], attn_k_norm[j], attn_w_o[j],
                               cos, sin, not last)
        x = x + g1 * rms_norm(yx, mix_norm_post[i])
        if not last:
            ctx = ctx + cg1 * rms_norm(yc, mix_norm_post[i])

        hx = rms_norm(x, mlp_norm_pre[i]) * (1.0 + sc2) + sh2
        x = x + g2 * rms_norm(sq_relu_mlp(hx, mlp_w1[i], mlp_w2[i]), mlp_norm_post[i])
        if not last:
            hc = rms_norm(ctx, mlp_norm_pre[i]) * (1.0 + csc2) + csh2
            ctx = ctx + cg2 * rms_norm(sq_relu_mlp(hc, mlp_w1[i], mlp_w2[i]), mlp_norm_post[i])
    return x
```

```python
import functools
import math

import jax
import jax.numpy as jnp
from jax import lax
from jax.experimental import pallas as pl
from jax.experimental.pallas import tpu as pltpu

F32 = jnp.float32
BF16 = jnp.bfloat16

EPS = 1e-6
GRID_W = 64
HY_BANDS = 16
HY_DECAY_TARGET = 1e-2
HY_FAST = 0.3
HY_SLOW = 1.5
HY_SHIFT = 0.0
HEAD_DIM = 128
N_KV_HEADS = 2
ROPE_THETA = 10000.0
ATTN_SCALE = HEAD_DIM ** -0.5

MOD_ROWS_PAD = 8

VMEM_LIMIT = 56 * 1024 * 1024


def _cparams(*sem):
    return pltpu.CompilerParams(dimension_semantics=sem, vmem_limit_bytes=VMEM_LIMIT)


def _const_spec(shape):
    nd = len(shape)
    return pl.BlockSpec(shape, lambda *_: (0,) * nd, pipeline_mode=pl.Buffered(1))


def _rms(x):
    return x * lax.rsqrt(jnp.mean(x * x, axis=-1, keepdims=True) + EPS)


def _norm_mod(x, gain, sc, sh):
    return (_rms(x) * gain) * (1.0 + sc) + sh


def _dot(a, b):
    return jnp.dot(a, b, preferred_element_type=F32)


def _dot_hi(a, b):
    return jnp.dot(a, b, preferred_element_type=F32, precision=lax.Precision.HIGHEST)


def _mod_kernel(cond_ref, w_ref, b_ref, o_ref):
    s = cond_ref[...]
    s = s * jax.nn.sigmoid(s)
    o_ref[0] = _dot(s.astype(BF16), w_ref[0].astype(BF16)) + b_ref[0]


def _modulation(cond, mod_w, mod_b):
    depth, d, n = mod_w.shape
    rows = cond.shape[0]
    tn = d
    return pl.pallas_call(
        _mod_kernel,
        grid=(depth, n // tn),
        in_specs=[
            pl.BlockSpec((rows, d), lambda i, j: (0, 0)),
            pl.BlockSpec((1, d, tn), lambda i, j: (i, 0, j)),
            pl.BlockSpec((1, 1, tn), lambda i, j: (i, 0, j)),
        ],
        out_specs=pl.BlockSpec((1, rows, tn), lambda i, j: (i, 0, j)),
        out_shape=jax.ShapeDtypeStruct((depth, rows, n), F32),
        compiler_params=_cparams("arbitrary", "arbitrary"),
    )(cond, mod_w, mod_b.reshape(depth, 1, n))


def _mod_block(d, chunk, row_fn):
    return pl.BlockSpec((1, 1, 1, d), lambda *idx: (row_fn(*idx), chunk, 0, 0))


def _hyena_in_kernel(x_ref, gain_ref, sc_ref, sh_ref, w0_ref, w1_ref, w2_ref, b_ref, cw_ref, cb_ref,
                     u_ref, x1_ref, h_buf, *, seq, row_chunk):
    j = pl.program_id(1)

    @pl.when(j == 0)
    def _():
        gain = gain_ref[...]
        sc = sc_ref[0, 0]
        sh = sh_ref[0, 0]

        def body(r, carry):
            rows = pl.ds(pl.multiple_of(r * row_chunk, row_chunk), row_chunk)
            h_buf[rows, :] = _norm_mod(x_ref[0, rows, :], gain, sc, sh).astype(BF16)
            return carry

        lax.fori_loop(0, seq // row_chunk, body, 0)

    h = h_buf[...]
    w = u_ref.shape[-1]
    row = lax.broadcasted_iota(jnp.int32, (seq, w), 0)
    first = row == 0
    last = row == seq - 1
    streams = []
    for s, w_ref in enumerate((w0_ref, w1_ref, w2_ref)):
        z = _dot(h, w_ref[...]) + b_ref[s:s + 1, :]
        z_prev = jnp.where(first, 0.0, pltpu.roll(z, 1, 0))
        z_next = jnp.where(last, 0.0, pltpu.roll(z, seq - 1, 0))
        streams.append(z_prev * cw_ref[0, s:s + 1, :] + z * cw_ref[1, s:s + 1, :]
                       + z_next * cw_ref[2, s:s + 1, :] + cb_ref[s:s + 1, :])
    x1, x2, v = streams
    u_ref[0] = (v * x2).astype(BF16)
    x1_ref[0] = x1.astype(BF16)


def _hyena_in(x, gain, mods, mod_row_fn, w_in, b_in, conv_w, conv_b, *, w=256):
    bsz, seq, d = x.shape
    nj = d // w
    row_chunk = min(seq, 256)
    kern = functools.partial(_hyena_in_kernel, seq=seq, row_chunk=row_chunk)
    wspec = lambda s: pl.BlockSpec((d, w), lambda b, j: (0, s * nj + j))
    out_spec = pl.BlockSpec((1, seq, w), lambda b, j: (b, 0, j))
    return pl.pallas_call(
        kern,
        grid=(bsz, nj),
        in_specs=[
            pl.BlockSpec((1, seq, d), lambda b, j: (b, 0, 0)),
            pl.BlockSpec((1, d), lambda b, j: (0, 0)),
            _mod_block(d, 1, lambda b, j: mod_row_fn(b)),
            _mod_block(d, 0, lambda b, j: mod_row_fn(b)),
            wspec(0), wspec(1), wspec(2),
            pl.BlockSpec((3, w), lambda b, j: (0, j)),
            pl.BlockSpec((3, 3, w), lambda b, j: (0, 0, j)),
            pl.BlockSpec((3, w), lambda b, j: (0, j)),
        ],
        out_specs=[out_spec, out_spec],
        out_shape=[jax.ShapeDtypeStruct((bsz, seq, d), BF16)] * 2,
        scratch_shapes=[pltpu.VMEM((seq, d), BF16)],
        compiler_params=_cparams("arbitrary", "arbitrary"),
    )(x, gain.reshape(1, d), mods, mods, w_in, w_in, w_in,
      b_in.reshape(3, d), conv_w.reshape(3, 3, d), conv_b.reshape(3, d))


def _dft_table(seq, shifted):
    k = lax.broadcasted_iota(jnp.int32, (seq, seq), 0)
    n = lax.broadcasted_iota(jnp.int32, (seq, seq), 1)
    if shifted:
        period = 8 * seq
        m = ((2 * k + 1) * (2 * n + 1)) % period
    else:
        period = 4 * seq
        m = ((2 * k + 1) * n) % period
    ang = m.astype(F32) * (2.0 * math.pi / period)
    return jnp.concatenate([jnp.cos(ang), jnp.sin(ang)], axis=0).astype(BF16)


def _filter_kernel(bands_ref, w1t_ref, w1c_ref, w1s_ref, b1_ref, f1_ref, w2_ref, b2_ref, f2_ref,
                   w3f_ref, w3b_ref, deltas_ref, bias_ref, tab_ref, p_ref, q_ref, *, seq):
    w = p_ref.shape[-1]
    t64 = lax.broadcasted_iota(jnp.int32, (seq, w1t_ref.shape[-1]), 0).astype(F32) / seq
    t16 = lax.broadcasted_iota(jnp.int32, (seq, HY_BANDS), 0).astype(F32) / seq
    ang = (2.0 * math.pi * t16) * bands_ref[...]
    pre = t64 * w1t_ref[...] + _dot_hi(jnp.cos(ang), w1c_ref[...]) + _dot_hi(jnp.sin(ang), w1s_ref[...])
    h = jnp.sin(f1_ref[...] * (pre + b1_ref[...]))
    h = jnp.sin(f2_ref[...] * (_dot_hi(h, w2_ref[...]) + b2_ref[...]))
    tw = lax.broadcasted_iota(jnp.int32, (seq, w), 0).astype(F32) / seq
    decay = jnp.exp(-tw * deltas_ref[...]) + HY_SHIFT
    h_f = _dot_hi(h, w3f_ref[...]) * decay
    h_b = _dot_hi(h, w3b_ref[...]) * decay
    row = lax.broadcasted_iota(jnp.int32, (seq, w), 0)
    h_b = jnp.where(row == 0, 0.0, h_b)
    p_ref[...] = _dot(tab_ref[pl.ds(0, seq), :], (h_f + h_b).astype(BF16)) + bias_ref[...]
    q_ref[...] = _dot(tab_ref[pl.ds(seq, seq), :], (h_b - h_f).astype(BF16))


def _hyena_filter_spectrum(seq, fw1, fb1, ff1, fw2, fb2, ff2, fw3, fbias, tab_plain, *, w=256):
    d = fbias.shape[-1]
    fwid = fw2.shape[0]
    bands = jnp.linspace(1e-4, HY_BANDS - 1, HY_BANDS, dtype=F32).reshape(1, HY_BANDS)
    deltas = jnp.abs(jnp.linspace(math.log(HY_DECAY_TARGET) / HY_SLOW, math.log(HY_DECAY_TARGET) / HY_FAST,
                                  d, dtype=F32)).reshape(1, d)
    nj = d // w
    small = lambda shape: pl.BlockSpec(shape, lambda j: (0,) * len(shape))
    kern = functools.partial(_filter_kernel, seq=seq)
    return pl.pallas_call(
        kern,
        grid=(nj,),
        in_specs=[
            small((1, HY_BANDS)), small((1, fwid)), small((HY_BANDS, fwid)), small((HY_BANDS, fwid)),
            small((1, fwid)), small((1, fwid)), small((fwid, fwid)), small((1, fwid)), small((1, fwid)),
            pl.BlockSpec((fwid, w), lambda j: (0, j)),
            pl.BlockSpec((fwid, w), lambda j: (0, nj + j)),
            pl.BlockSpec((1, w), lambda j: (0, j)),
            pl.BlockSpec((1, w), lambda j: (0, j)),
            _const_spec((2 * seq, seq)),
        ],
        out_specs=[pl.BlockSpec((seq, w), lambda j: (0, j))] * 2,
        out_shape=[jax.ShapeDtypeStruct((seq, d), F32)] * 2,
        compiler_params=_cparams("arbitrary"),
    )(bands, fw1[0:1], fw1[1:1 + HY_BANDS], fw1[1 + HY_BANDS:], fb1.reshape(1, fwid), ff1.reshape(1, fwid),
      fw2, fb2.reshape(1, fwid), ff2.reshape(1, fwid), fw3, fw3, deltas, fbias.reshape(1, d), tab_plain)


def _fftconv_kernel(u_ref, x1_ref, p_ref, q_ref, tab_ref, o_ref, *, seq):
    u = u_ref[0]
    a = _dot(tab_ref[pl.ds(0, seq), :], u)
    b = _dot(tab_ref[pl.ds(seq, seq), :], u)
    p = p_ref[...]
    q = q_ref[...]
    y_re = (a * p + b * q).astype(BF16)
    y_im = (a * q - b * p).astype(BF16)
    y = _dot(tab_ref[pl.ds(0, seq), :], y_re) - _dot(tab_ref[pl.ds(seq, seq), :], y_im)
    o_ref[0] = (y * (1.0 / seq) * x1_ref[0].astype(F32)).astype(BF16)


def _fftconv(u, x1, p, q, tab_shift, *, w=256):
    bsz, seq, d = u.shape
    nj = d // w
    act = pl.BlockSpec((1, seq, w), lambda j, b: (b, 0, j))
    spec = pl.BlockSpec((seq, w), lambda j, b: (0, j))
    kern = functools.partial(_fftconv_kernel, seq=seq)
    return pl.pallas_call(
        kern,
        grid=(nj, bsz),
        in_specs=[act, act, spec, spec, _const_spec((2 * seq, seq))],
        out_specs=act,
        out_shape=jax.ShapeDtypeStruct((bsz, seq, d), BF16),
        compiler_params=_cparams("arbitrary", "arbitrary"),
    )(u, x1, p, q, tab_shift)


def _proj_res_kernel(a_ref, x_ref, w_ref, b_ref, gain_ref, g_ref, o_ref):
    y = _dot(a_ref[...], w_ref[...]) + b_ref[...]
    o_ref[...] = x_ref[...] + g_ref[0, 0] * (_rms(y) * gain_ref[...])


def _proj_res(a, x, w, b, gain, mods, gate_chunk, mod_row_fn, *, tm):
    rows, d = x.shape
    k = a.shape[-1]
    return pl.pallas_call(
        _proj_res_kernel,
        grid=(rows // tm,),
        in_specs=[
            pl.BlockSpec((tm, k), lambda i: (i, 0)),
            pl.BlockSpec((tm, d), lambda i: (i, 0)),
            _const_spec((k, d)),
            pl.BlockSpec((1, d), lambda i: (0, 0)),
            pl.BlockSpec((1, d), lambda i: (0, 0)),
            _mod_block(d, gate_chunk, mod_row_fn),
        ],
        out_specs=pl.BlockSpec((tm, d), lambda i: (i, 0)),
        out_shape=jax.ShapeDtypeStruct((rows, d), F32),
        compiler_params=_cparams("arbitrary"),
    )(a, x, w, b.reshape(1, d), gain.reshape(1, d), mods)


def _mlp_kernel(x_ref, pre_ref, sc_ref, sh_ref, g_ref, w1_ref, w2_ref, post_ref, o_ref, *, ff_chunk):
    x = x_ref[...]
    h = _norm_mod(x, pre_ref[...], sc_ref[0, 0], sh_ref[0, 0]).astype(BF16)
    d_ff = w1_ref.shape[-1]
    acc = None
    for c in range(d_ff // ff_chunk):
        cols = pl.ds(c * ff_chunk, ff_chunk)
        a = jnp.maximum(_dot(h, w1_ref[:, cols]), 0.0)
        part = _dot((a * a).astype(BF16), w2_ref[cols, :])
        acc = part if acc is None else acc + part
    o_ref[...] = x + g_ref[0, 0] * (_rms(acc) * post_ref[...])


def _mlp(x, pre, post, mods, mod_row_fn, w1, w2, *, tm, ff_chunk=1024):
    rows, d = x.shape
    d_ff = w1.shape[-1]
    kern = functools.partial(_mlp_kernel, ff_chunk=ff_chunk)
    vec = pl.BlockSpec((1, d), lambda i: (0, 0))
    return pl.pallas_call(
        kern,
        grid=(rows // tm,),
        in_specs=[
            pl.BlockSpec((tm, d), lambda i: (i, 0)),
            vec,
            _mod_block(d, 4, mod_row_fn),
            _mod_block(d, 3, mod_row_fn),
            _mod_block(d, 5, mod_row_fn),
            _const_spec((d, d_ff)),
            _const_spec((d_ff, d)),
            vec,
        ],
        out_specs=pl.BlockSpec((tm, d), lambda i: (i, 0)),
        out_shape=jax.ShapeDtypeStruct((rows, d), F32),
        compiler_params=_cparams("arbitrary"),
    )(x, pre.reshape(1, d), mods, mods, mods, w1, w2, post.reshape(1, d))


def _qkv_kernel(x_ref, pre_ref, sc_ref, sh_ref, w_ref, qg_ref, kg_ref, cos_ref, sin_ref,
                *out_refs, n_q, n_kv, rope):
    q_ref = out_refs[0] if n_q else None
    k_ref, v_ref = out_refs[-2:]
    h = _norm_mod(x_ref[...], pre_ref[...], sc_ref[0, 0], sh_ref[0, 0]).astype(BF16)
    qkv = _dot(h, w_ref[...])
    hd = HEAD_DIM

    def head(idx, gain, scale):
        t = _rms(qkv[:, idx * hd:(idx + 1) * hd]) * gain
        if rope:
            t = t * cos_ref[...] + pltpu.roll(t, hd // 2, 1) * sin_ref[...]
        if scale != 1.0:
            t = t * scale
        return t.astype(BF16)

    for i in range(n_q):
        q_ref[:, i * hd:(i + 1) * hd] = head(i, qg_ref[...], ATTN_SCALE)
    for i in range(n_kv):
        k_ref[:, i * hd:(i + 1) * hd] = head(n_q + i, kg_ref[...], 1.0)
    v0 = (n_q + n_kv) * hd
    v_ref[...] = qkv[:, v0:v0 + n_kv * hd].astype(BF16)


def _qkv(x, pre, mods, mod_row_fn, w, q_gain, k_gain, cos_t, sin_t, *, n_q, n_kv, rope, tm, seq):
    rows, d = x.shape
    hd = HEAD_DIM
    n = w.shape[-1]
    kern = functools.partial(_qkv_kernel, n_q=n_q, n_kv=n_kv, rope=rope)
    vec = pl.BlockSpec((1, d), lambda i: (0, 0))
    hvec = pl.BlockSpec((1, hd), lambda i: (0, 0))
    tiles_per_seq = seq // tm
    pos = pl.BlockSpec((tm, hd), lambda i: (i % tiles_per_seq, 0))
    widths = ([n_q * hd] if n_q else []) + [n_kv * hd] * 2
    return pl.pallas_call(
        kern,
        grid=(rows // tm,),
        in_specs=[
            pl.BlockSpec((tm, d), lambda i: (i, 0)),
            vec,
            _mod_block(d, 1, mod_row_fn),
            _mod_block(d, 0, mod_row_fn),
            _const_spec((d, n)),
            hvec, hvec, pos, pos,
        ],
        out_specs=[pl.BlockSpec((tm, wd), lambda i: (i, 0)) for wd in widths],
        out_shape=[jax.ShapeDtypeStruct((rows, wd), BF16) for wd in widths],
        compiler_params=_cparams("arbitrary"),
    )(x, pre.reshape(1, d), mods, mods, w, q_gain.reshape(1, hd), k_gain.reshape(1, hd), cos_t, sin_t)


def _attn_kernel(q_ref, kl_ref, vl_ref, kc_ref, vc_ref, o_ref, *, group):
    hd = HEAD_DIM
    tq = q_ref.shape[1]
    q = jnp.concatenate([q_ref[0, :, g * hd:(g + 1) * hd] for g in range(group)], axis=0)
    nt = (((1,), (1,)), ((), ()))
    s_l = lax.dot_general(q, kl_ref[0], nt, preferred_element_type=F32)
    s_c = lax.dot_general(q, kc_ref[0], nt, preferred_element_type=F32)
    m = jnp.maximum(jnp.max(s_l, axis=-1, keepdims=True), jnp.max(s_c, axis=-1, keepdims=True))
    p_l = jnp.exp(s_l - m)
    p_c = jnp.exp(s_c - m)
    denom = jnp.sum(p_l, axis=-1, keepdims=True) + jnp.sum(p_c, axis=-1, keepdims=True)
    o = _dot(p_l.astype(BF16), vl_ref[0]) + _dot(p_c.astype(BF16), vc_ref[0])
    o = (o / denom).astype(BF16)
    for g in range(group):
        o_ref[0, :, g * hd:(g + 1) * hd] = o[g * tq:(g + 1) * tq, :]


def _attention(q, k_l, v_l, k_c, v_c, *, tq=256):
    bsz, seq, dq = q.shape
    ctx_len = k_c.shape[1]
    hd = HEAD_DIM
    n_kv = k_l.shape[-1] // hd
    group = dq // hd // n_kv
    kern = functools.partial(_attn_kernel, group=group)
    qspec = pl.BlockSpec((1, tq, group * hd), lambda b, h, i: (b, i, h))
    return pl.pallas_call(
        kern,
        grid=(bsz, n_kv, seq // tq),
        in_specs=[
            qspec,
            pl.BlockSpec((1, seq, hd), lambda b, h, i: (b, 0, h)),
            pl.BlockSpec((1, seq, hd), lambda b, h, i: (b, 0, h)),
            pl.BlockSpec((1, ctx_len, hd), lambda b, h, i: (b, 0, h)),
            pl.BlockSpec((1, ctx_len, hd), lambda b, h, i: (b, 0, h)),
        ],
        out_specs=qspec,
        out_shape=jax.ShapeDtypeStruct((bsz, seq, dq), BF16),
        compiler_params=_cparams("arbitrary", "arbitrary", "arbitrary"),
    )(q, k_l, v_l, k_c, v_c)


def _rope_tables(seq):
    rows = seq // GRID_W
    pairs = HEAD_DIM // 4
    row = jnp.repeat(jnp.arange(rows, dtype=F32), GRID_W)
    col = jnp.tile(jnp.arange(GRID_W, dtype=F32), rows)
    inv = ROPE_THETA ** (-jnp.arange(pairs, dtype=F32) / pairs)
    ang = jnp.concatenate([row[:, None] * inv[None, :], col[:, None] * inv[None, :]], axis=-1)
    cos, sin = jnp.cos(ang), jnp.sin(ang)
    return jnp.concatenate([cos, cos], axis=-1), jnp.concatenate([-sin, sin], axis=-1)


def _deinterleave(n_heads):
    hd = HEAD_DIM
    within = jnp.concatenate([jnp.arange(0, hd, 2), jnp.arange(1, hd, 2)])
    return (jnp.arange(n_heads)[:, None] * hd + within[None, :]).reshape(-1)


def kernel(x, c, ctx, c_ctx, mod_w, mod_b, mix_norm_pre, mix_norm_post, mlp_norm_pre, mlp_norm_post, mlp_w1, mlp_w2, hy_w_in, hy_b_in, hy_conv_w, hy_conv_b, hy_filt_w1, hy_filt_b1, hy_filt_freq1, hy_filt_w2, hy_filt_b2, hy_filt_freq2, hy_filt_w3, hy_filt_bias, hy_w_out, hy_b_out, attn_w_qkv, attn_q_norm, attn_k_norm, attn_w_o):
    bsz, seq, d = x.shape
    ctx_len = ctx.shape[1]
    hd = HEAD_DIM
    n_heads = d // hd
    tm = 512
    tmc = ctx_len

    ctx_row = bsz
    n_rows = -(-(bsz + 1) // MOD_ROWS_PAD) * MOD_ROWS_PAD
    cond = jnp.concatenate([c, c_ctx[None, :], jnp.zeros((n_rows - bsz - 1, d), F32)], axis=0)
    mods_all = _modulation(cond, mod_w, mod_b).reshape(mod_w.shape[0], n_rows, 6, 1, d)

    x_row = lambda i: i // (seq // tm)
    c_row = lambda i: ctx_row

    xf = x.reshape(bsz * seq, d)
    cf = ctx.reshape(bsz * ctx_len, d)

    mods = mods_all[0]
    w_in = hy_w_in[0].astype(BF16)
    w_out = hy_w_out[0].astype(BF16)
    filt = (hy_filt_w1[0], hy_filt_b1[0], hy_filt_freq1[0], hy_filt_w2[0], hy_filt_b2[0], hy_filt_freq2[0],
            hy_filt_w3[0], hy_filt_bias[0])
    w1 = mlp_w1[0].astype(BF16)
    w2 = mlp_w2[0].astype(BF16)

    def hyena(tokens, length, mod_row_b, mod_row_tile, tile):
        u, x1 = _hyena_in(tokens.reshape(bsz, length, d), mix_norm_pre[0], mods, mod_row_b,
                          w_in, hy_b_in[0], hy_conv_w[0], hy_conv_b[0])
        p, q = _hyena_filter_spectrum(length, *filt, _dft_table(length, shifted=False))
        gated = _fftconv(u, x1, p, q, _dft_table(length, shifted=True))
        return _proj_res(gated.reshape(bsz * length, d), tokens, w_out, hy_b_out[0], mix_norm_post[0],
                         mods, 2, mod_row_tile, tm=tile)

    xf = hyena(xf, seq, lambda b: b, x_row, tm)
    cf = hyena(cf, ctx_len, lambda b: ctx_row, c_row, tmc)
    xf = _mlp(xf, mlp_norm_pre[0], mlp_norm_post[0], mods, x_row, w1, w2, tm=tm)
    cf = _mlp(cf, mlp_norm_pre[0], mlp_norm_post[0], mods, c_row, w1, w2, tm=tmc)

    mods = mods_all[1]
    w_qkv = attn_w_qkv[0]
    n_qk = n_heads + N_KV_HEADS
    perm = _deinterleave(n_qk)
    w_qk = w_qkv[:, :n_qk * hd][:, perm]
    w_lat = jnp.concatenate([w_qk, w_qkv[:, n_qk * hd:]], axis=1).astype(BF16)
    w_ctx = w_lat[:, n_heads * hd:]
    within = _deinterleave(1)
    q_gain = attn_q_norm[0][within]
    k_gain = attn_k_norm[0][within]
    cos_t, sin_t = _rope_tables(seq)

    q, k_l, v_l = _qkv(xf, mix_norm_pre[1], mods, x_row, w_lat, q_gain, k_gain, cos_t, sin_t,
                       n_q=n_heads, n_kv=N_KV_HEADS, rope=True, tm=tm, seq=seq)
    k_c, v_c = _qkv(cf, mix_norm_pre[1], mods, c_row, w_ctx, q_gain, k_gain, cos_t, sin_t,
                       n_q=0, n_kv=N_KV_HEADS, rope=False, tm=tmc, seq=tmc)
    kv = N_KV_HEADS * hd
    o = _attention(q.reshape(bsz, seq, d), k_l.reshape(bsz, seq, kv), v_l.reshape(bsz, seq, kv),
                   k_c.reshape(bsz, ctx_len, kv), v_c.reshape(bsz, ctx_len, kv))
    xf = _proj_res(o.reshape(bsz * seq, d), xf, attn_w_o[0].astype(BF16), jnp.zeros((d,), F32),
                   mix_norm_post[1], mods, 2, x_row, tm=tm)
    xf = _mlp(xf, mlp_norm_pre[1], mlp_norm_post[1], mods, x_row, mlp_w1[1].astype(BF16),
              mlp_w2[1].astype(BF16), tm=tm)
    return xf.reshape(bsz, seq, d)
```

```python
import functools
import math

import jax
import jax.numpy as jnp
from jax import lax
from jax.experimental import pallas as pl
from jax.experimental.pallas import tpu as pltpu

F32 = jnp.float32
BF16 = jnp.bfloat16

EPS = 1e-6
GRID_W = 64
HY_BANDS = 16
HY_DECAY_TARGET = 1e-2
HY_FAST = 0.3
HY_SLOW = 1.5
HY_SHIFT = 0.0
HEAD_DIM = 128
N_KV_HEADS = 2
ROPE_THETA = 10000.0
ATTN_SCALE = HEAD_DIM ** -0.5
LOG2_E = math.log2(math.e)

MOD_ROWS_PAD = 8
SUBLANES = 8

VMEM_LIMIT = 56 * 1024 * 1024


def _cparams(*sem):
    return pltpu.CompilerParams(dimension_semantics=sem, vmem_limit_bytes=VMEM_LIMIT)


def _const_spec(shape):
    nd = len(shape)
    return pl.BlockSpec(shape, lambda *_: (0,) * nd, pipeline_mode=pl.Buffered(1))


def _rms(x):
    return x * lax.rsqrt(jnp.mean(x * x, axis=-1, keepdims=True) + EPS)


def _norm_mod(x, gain, sc, sh):
    return _rms(x) * (gain * (1.0 + sc)) + sh


def _dot(a, b):
    return jnp.dot(a, b, preferred_element_type=F32)


def _dot_hi(a, b):
    return jnp.dot(a, b, preferred_element_type=F32, precision=lax.Precision.HIGHEST)


def _mod_kernel(cond_ref, w_ref, b_ref, o_ref):
    s = cond_ref[...]
    s = s * jax.nn.sigmoid(s)
    o_ref[0] = _dot(s.astype(BF16), w_ref[0].astype(BF16)) + b_ref[0]


def _modulation(cond, mod_w, mod_b):
    depth, d, n = mod_w.shape
    rows = cond.shape[0]
    tn = d
    return pl.pallas_call(
        _mod_kernel,
        grid=(depth, n // tn),
        in_specs=[
            pl.BlockSpec((rows, d), lambda i, j: (0, 0)),
            pl.BlockSpec((1, d, tn), lambda i, j: (i, 0, j)),
            pl.BlockSpec((1, 1, tn), lambda i, j: (i, 0, j)),
        ],
        out_specs=pl.BlockSpec((1, rows, tn), lambda i, j: (i, 0, j)),
        out_shape=jax.ShapeDtypeStruct((depth, rows, n), F32),
        compiler_params=_cparams("arbitrary", "arbitrary"),
    )(cond, mod_w, mod_b.reshape(depth, 1, n))


def _mod_block(d, chunk, row_fn):
    return pl.BlockSpec((1, 1, 1, d), lambda *idx: (row_fn(*idx), chunk, 0, 0))


def _hyena_in_kernel(x_ref, gain_ref, sc_ref, sh_ref, w0_ref, w1_ref, w2_ref, b_ref, cw_ref, cb_ref,
                     u_ref, x1_ref, h_buf, *, seq, row_chunk, conv_chunk):
    j = pl.program_id(1)
    w = u_ref.shape[-1]

    @pl.when(j == 0)
    def _():
        mult = gain_ref[...] * (1.0 + sc_ref[0, 0])
        sh = sh_ref[0, 0]

        def body(r, carry):
            rows = pl.ds(pl.multiple_of(r * row_chunk, row_chunk), row_chunk)
            h_buf[rows, :] = (_rms(x_ref[0, rows, :]) * mult + sh).astype(BF16)
            return carry

        lax.fori_loop(0, seq // row_chunk, body, 0)

    rc = conv_chunk
    n_chunks = seq // rc
    sub = lax.broadcasted_iota(jnp.int32, (SUBLANES, w), 0)
    w_refs = (w0_ref, w1_ref, w2_ref)
    taps = [[cw_ref[k, s:s + 1, :] for s in range(3)] for k in range(3)]
    bias = [cb_ref[s:s + 1, :] + b_ref[s:s + 1, :] * (taps[0][s] + taps[1][s] + taps[2][s]) for s in range(3)]

    def project(c):
        h = h_buf[c * rc:(c + 1) * rc, :]
        return [_dot(h, w_refs[s][...]) for s in range(3)]

    def conv(c, m_before, m_here, m_after):
        streams = []
        for s in range(3):
            m = m_here[s]
            row_before = -b_ref[s:s + 1, :] if m_before is None else m_before[s][rc - 1:rc, :]
            row_after = -b_ref[s:s + 1, :] if m_after is None else m_after[s][0:1, :]
            m_prev = pltpu.roll(m, 1, 0)
            m_prev = jnp.concatenate([jnp.where(sub == 0, row_before, m_prev[0:SUBLANES]),
                                      m_prev[SUBLANES:]], axis=0)
            m_next = pltpu.roll(m, rc - 1, 0)
            m_next = jnp.concatenate([m_next[:rc - SUBLANES],
                                      jnp.where(sub == SUBLANES - 1, row_after, m_next[rc - SUBLANES:])], axis=0)
            streams.append(m_prev * taps[0][s] + m * taps[1][s] + m_next * taps[2][s] + bias[s])
        x1, x2, v = streams
        u_ref[0, c * rc:(c + 1) * rc, :] = (v * x2).astype(BF16)
        x1_ref[0, c * rc:(c + 1) * rc, :] = x1.astype(BF16)

    zs = [project(0)]
    for c in range(1, n_chunks):
        zs.append(project(c))
        conv(c - 1, zs[c - 2] if c >= 2 else None, zs[c - 1], zs[c])
    conv(n_chunks - 1, zs[n_chunks - 2] if n_chunks >= 2 else None, zs[n_chunks - 1], None)


def _hyena_in(x, gain, mods, mod_row_fn, w_in, b_in, conv_w, conv_b, *, w=512):
    bsz, seq, d = x.shape
    w = min(w, d)
    nj = d // w
    row_chunk = min(seq, 256)
    kern = functools.partial(_hyena_in_kernel, seq=seq, row_chunk=row_chunk, conv_chunk=min(seq, 512))
    wspec = lambda s: pl.BlockSpec((d, w), lambda b, j: (0, s * nj + j))
    out_spec = pl.BlockSpec((1, seq, w), lambda b, j: (b, 0, j))
    return pl.pallas_call(
        kern,
        grid=(bsz, nj),
        in_specs=[
            pl.BlockSpec((1, seq, d), lambda b, j: (b, 0, 0)),
            pl.BlockSpec((1, d), lambda b, j: (0, 0)),
            _mod_block(d, 1, lambda b, j: mod_row_fn(b)),
            _mod_block(d, 0, lambda b, j: mod_row_fn(b)),
            wspec(0), wspec(1), wspec(2),
            pl.BlockSpec((3, w), lambda b, j: (0, j)),
            pl.BlockSpec((3, 3, w), lambda b, j: (0, 0, j)),
            pl.BlockSpec((3, w), lambda b, j: (0, j)),
        ],
        out_specs=[out_spec, out_spec],
        out_shape=[jax.ShapeDtypeStruct((bsz, seq, d), BF16)] * 2,
        scratch_shapes=[pltpu.VMEM((seq, d), BF16)],
        compiler_params=_cparams("arbitrary", "arbitrary"),
    )(x, gain.reshape(1, d), mods, mods, w_in, w_in, w_in,
      b_in.reshape(3, d), conv_w.reshape(3, 3, d), conv_b.reshape(3, d))


def _dft_table(seq, shifted, kb=32):
    if shifted:
        period = 8 * seq
        nn = 2 * jnp.arange(seq, dtype=jnp.int32)[None, :] + 1
    else:
        period = 4 * seq
        nn = jnp.arange(seq, dtype=jnp.int32)[None, :]
    kh = jnp.arange(seq // kb, dtype=jnp.int32)[:, None]
    kl = jnp.arange(kb, dtype=jnp.int32)[:, None]
    to_angle = lambda m: (m % period).astype(F32) * (2.0 * math.pi / period)
    alpha = to_angle(2 * kb * kh * nn)[:, None, :]
    beta = to_angle((2 * kl + 1) * nn)[None, :, :]
    ca, sa, cb, sb = jnp.cos(alpha), jnp.sin(alpha), jnp.cos(beta), jnp.sin(beta)
    cos_t = (ca * cb - sa * sb).reshape(seq, seq)
    sin_t = (sa * cb + ca * sb).reshape(seq, seq)
    return jnp.concatenate([cos_t, sin_t], axis=0).astype(BF16)


def _filter_kernel(bands_ref, w1t_ref, w1c_ref, w1s_ref, b1_ref, f1_ref, w2_ref, b2_ref, f2_ref,
                   w3f_ref, w3b_ref, deltas_ref, bias_ref, tab_ref, p_ref, q_ref, *, seq):
    w = p_ref.shape[-1]
    t64 = lax.broadcasted_iota(jnp.int32, (seq, w1t_ref.shape[-1]), 0).astype(F32) / seq
    t16 = lax.broadcasted_iota(jnp.int32, (seq, HY_BANDS), 0).astype(F32) / seq
    ang = (2.0 * math.pi * t16) * bands_ref[...]
    pre = t64 * w1t_ref[...] + _dot_hi(jnp.cos(ang), w1c_ref[...]) + _dot_hi(jnp.sin(ang), w1s_ref[...])
    h = jnp.sin(f1_ref[...] * (pre + b1_ref[...]))
    h = jnp.sin(f2_ref[...] * (_dot_hi(h, w2_ref[...]) + b2_ref[...]))
    tw = lax.broadcasted_iota(jnp.int32, (seq, w), 0).astype(F32) / seq
    decay = jnp.exp(-tw * deltas_ref[...]) + HY_SHIFT
    h_f = _dot_hi(h, w3f_ref[...]) * decay
    h_b = _dot_hi(h, w3b_ref[...]) * decay
    row = lax.broadcasted_iota(jnp.int32, (seq, w), 0)
    h_b = jnp.where(row == 0, 0.0, h_b)
    p_ref[...] = _dot(tab_ref[pl.ds(0, seq), :], (h_f + h_b).astype(BF16)) + bias_ref[...]
    q_ref[...] = _dot(tab_ref[pl.ds(seq, seq), :], (h_b - h_f).astype(BF16))


def _hyena_filter_spectrum(seq, fw1, fb1, ff1, fw2, fb2, ff2, fw3, fbias, tab_plain, *, w=256):
    d = fbias.shape[-1]
    fwid = fw2.shape[0]
    bands = jnp.linspace(1e-4, HY_BANDS - 1, HY_BANDS, dtype=F32).reshape(1, HY_BANDS)
    deltas = jnp.abs(jnp.linspace(math.log(HY_DECAY_TARGET) / HY_SLOW, math.log(HY_DECAY_TARGET) / HY_FAST,
                                  d, dtype=F32)).reshape(1, d)
    nj = d // w
    small = lambda shape: pl.BlockSpec(shape, lambda j: (0,) * len(shape))
    kern = functools.partial(_filter_kernel, seq=seq)
    return pl.pallas_call(
        kern,
        grid=(nj,),
        in_specs=[
            small((1, HY_BANDS)), small((1, fwid)), small((HY_BANDS, fwid)), small((HY_BANDS, fwid)),
            small((1, fwid)), small((1, fwid)), small((fwid, fwid)), small((1, fwid)), small((1, fwid)),
            pl.BlockSpec((fwid, w), lambda j: (0, j)),
            pl.BlockSpec((fwid, w), lambda j: (0, nj + j)),
            pl.BlockSpec((1, w), lambda j: (0, j)),
            pl.BlockSpec((1, w), lambda j: (0, j)),
            _const_spec((2 * seq, seq)),
        ],
        out_specs=[pl.BlockSpec((seq, w), lambda j: (0, j))] * 2,
        out_shape=[jax.ShapeDtypeStruct((seq, d), F32)] * 2,
        compiler_params=_cparams("arbitrary"),
    )(bands, fw1[0:1], fw1[1:1 + HY_BANDS], fw1[1 + HY_BANDS:], fb1.reshape(1, fwid), ff1.reshape(1, fwid),
      fw2, fb2.reshape(1, fwid), ff2.reshape(1, fwid), fw3, fw3, deltas, fbias.reshape(1, d), tab_plain)


def _fftconv_kernel(u_ref, x1_ref, p_ref, q_ref, tab_ref, o_ref, *, seq):
    u = u_ref[0]
    a = _dot(tab_ref[pl.ds(0, seq), :], u)
    b = _dot(tab_ref[pl.ds(seq, seq), :], u)
    p = p_ref[...]
    q = q_ref[...]
    y_re = (a * p + b * q).astype(BF16)
    y_im = (a * q - b * p).astype(BF16)
    y = _dot(tab_ref[pl.ds(0, seq), :], y_re) - _dot(tab_ref[pl.ds(seq, seq), :], y_im)
    o_ref[0] = (y * (1.0 / seq) * x1_ref[0].astype(F32)).astype(BF16)


def _fftconv(u, x1, p, q, tab_shift, *, w=256):
    bsz, seq, d = u.shape
    nj = d // w
    act = pl.BlockSpec((1, seq, w), lambda j, b: (b, 0, j))
    spec = pl.BlockSpec((seq, w), lambda j, b: (0, j))
    kern = functools.partial(_fftconv_kernel, seq=seq)
    return pl.pallas_call(
        kern,
        grid=(nj, bsz),
        in_specs=[act, act, spec, spec, _const_spec((2 * seq, seq))],
        out_specs=act,
        out_shape=jax.ShapeDtypeStruct((bsz, seq, d), BF16),
        compiler_params=_cparams("arbitrary", "arbitrary"),
    )(u, x1, p, q, tab_shift)


def _proj_res_kernel(a_ref, x_ref, w_ref, b_ref, gain_ref, g_ref, o_ref):
    y = _dot(a_ref[...], w_ref[...]) + b_ref[...]
    o_ref[...] = x_ref[...] + g_ref[0, 0] * (_rms(y) * gain_ref[...])


def _proj_res(a, x, w, b, gain, mods, gate_chunk, mod_row_fn, *, tm):
    rows, d = x.shape
    k = a.shape[-1]
    return pl.pallas_call(
        _proj_res_kernel,
        grid=(rows // tm,),
        in_specs=[
            pl.BlockSpec((tm, k), lambda i: (i, 0)),
            pl.BlockSpec((tm, d), lambda i: (i, 0)),
            _const_spec((k, d)),
            pl.BlockSpec((1, d), lambda i: (0, 0)),
            pl.BlockSpec((1, d), lambda i: (0, 0)),
            _mod_block(d, gate_chunk, mod_row_fn),
        ],
        out_specs=pl.BlockSpec((tm, d), lambda i: (i, 0)),
        out_shape=jax.ShapeDtypeStruct((rows, d), F32),
        compiler_params=_cparams("arbitrary"),
    )(a, x, w, b.reshape(1, d), gain.reshape(1, d), mods)


def _mlp_kernel(x_ref, pre_ref, sc_ref, sh_ref, g_ref, w1_ref, w2_ref, post_ref, o_ref, *, ff_chunk):
    x = x_ref[...]
    h = _norm_mod(x, pre_ref[...], sc_ref[0, 0], sh_ref[0, 0]).astype(BF16)
    d_ff = w1_ref.shape[-1]
    acc = None
    for c in range(d_ff // ff_chunk):
        cols = pl.ds(c * ff_chunk, ff_chunk)
        a = jnp.maximum(_dot(h, w1_ref[:, cols]), 0.0)
        part = _dot((a * a).astype(BF16), w2_ref[cols, :])
        acc = part if acc is None else acc + part
    o_ref[...] = x + g_ref[0, 0] * (_rms(acc) * post_ref[...])


def _mlp(x, pre, post, mods, mod_row_fn, w1, w2, *, tm, ff_chunk=1024):
    rows, d = x.shape
    d_ff = w1.shape[-1]
    kern = functools.partial(_mlp_kernel, ff_chunk=ff_chunk)
    vec = pl.BlockSpec((1, d), lambda i: (0, 0))
    return pl.pallas_call(
        kern,
        grid=(rows // tm,),
        in_specs=[
            pl.BlockSpec((tm, d), lambda i: (i, 0)),
            vec,
            _mod_block(d, 4, mod_row_fn),
            _mod_block(d, 3, mod_row_fn),
            _mod_block(d, 5, mod_row_fn),
            _const_spec((d, d_ff)),
            _const_spec((d_ff, d)),
            vec,
        ],
        out_specs=pl.BlockSpec((tm, d), lambda i: (i, 0)),
        out_shape=jax.ShapeDtypeStruct((rows, d), F32),
        compiler_params=_cparams("arbitrary"),
    )(x, pre.reshape(1, d), mods, mods, mods, w1, w2, post.reshape(1, d))


def _qkv_kernel(x_ref, pre_ref, sc_ref, sh_ref, w_ref, qg_ref, kg_ref, cos_ref, sin_ref,
                *out_refs, n_q, n_kv, rope, sub_rows):
    q_ref = out_refs[0] if n_q else None
    k_ref, v_ref = out_refs[-2:]
    hd = HEAD_DIM
    tm = x_ref.shape[0]
    mult = pre_ref[...] * (1.0 + sc_ref[0, 0])
    sh = sh_ref[0, 0]
    q_gain = qg_ref[...] * (ATTN_SCALE * LOG2_E)
    k_gain = kg_ref[...]

    def rows_of(i):
        return slice(i * sub_rows, (i + 1) * sub_rows)

    def project(i):
        h = (_rms(x_ref[rows_of(i), :]) * mult + sh).astype(BF16)
        return _dot(h, w_ref[...])

    def finish(i, qkv):
        rows = rows_of(i)
        if rope:
            cos_t, sin_t = cos_ref[rows, :], sin_ref[rows, :]
            tabs = [(g * cos_t, pltpu.roll(g, hd // 2, 1) * sin_t) for g in (q_gain, k_gain)]

        def head(idx, is_q):
            t = qkv[:, idx * hd:(idx + 1) * hd]
            r = lax.rsqrt(jnp.mean(t * t, axis=-1, keepdims=True) + EPS)
            if rope:
                c_t, s_t = tabs[0 if is_q else 1]
                return (r * (t * c_t + pltpu.roll(t, hd // 2, 1) * s_t)).astype(BF16)
            return (r * (t * (q_gain if is_q else k_gain))).astype(BF16)

        for j in range(n_q):
            q_ref[rows, j * hd:(j + 1) * hd] = head(j, True)
        for j in range(n_kv):
            k_ref[rows, j * hd:(j + 1) * hd] = head(n_q + j, False)
        v0 = (n_q + n_kv) * hd
        v_ref[rows, :] = qkv[:, v0:v0 + n_kv * hd].astype(BF16)

    n_sub = tm // sub_rows
    pending = project(0)
    for i in range(1, n_sub):
        nxt = project(i)
        finish(i - 1, pending)
        pending = nxt
    finish(n_sub - 1, pending)


def _qkv(x, pre, mods, mod_row_fn, w, q_gain, k_gain, cos_t, sin_t, *, n_q, n_kv, rope, tm, seq):
    rows, d = x.shape
    hd = HEAD_DIM
    n = w.shape[-1]
    kern = functools.partial(_qkv_kernel, n_q=n_q, n_kv=n_kv, rope=rope, sub_rows=min(tm, 256))
    vec = pl.BlockSpec((1, d), lambda i: (0, 0))
    hvec = pl.BlockSpec((1, hd), lambda i: (0, 0))
    tiles_per_seq = seq // tm
    pos = pl.BlockSpec((tm, hd), lambda i: (i % tiles_per_seq, 0))
    widths = ([n_q * hd] if n_q else []) + [n_kv * hd] * 2
    return pl.pallas_call(
        kern,
        grid=(rows // tm,),
        in_specs=[
            pl.BlockSpec((tm, d), lambda i: (i, 0)),
            vec,
            _mod_block(d, 1, mod_row_fn),
            _mod_block(d, 0, mod_row_fn),
            _const_spec((d, n)),
            hvec, hvec, pos, pos,
        ],
        out_specs=[pl.BlockSpec((tm, wd), lambda i: (i, 0)) for wd in widths],
        out_shape=[jax.ShapeDtypeStruct((rows, wd), BF16) for wd in widths],
        compiler_params=_cparams("arbitrary"),
    )(x, pre.reshape(1, d), mods, mods, w, q_gain.reshape(1, hd), k_gain.reshape(1, hd), cos_t, sin_t)


def _attn_kernel(q_ref, kl_ref, vl_ref, kc_ref, vc_ref, o_ref, k_all, v_ext, *, group, kv_block):
    hd = HEAD_DIM
    tq = q_ref.shape[1]
    seq = kl_ref.shape[1]
    total = k_all.shape[0]

    @pl.when(pl.program_id(2) == 0)
    def _():
        k_all[0:seq, :] = kl_ref[0]
        k_all[seq:total, :] = kc_ref[0]
        v_ext[0:seq, 0:hd] = vl_ref[0]
        v_ext[seq:total, 0:hd] = vc_ref[0]
        v_ext[:, hd:2 * hd] = jnp.ones((total, hd), BF16)

    q = jnp.concatenate([q_ref[0, :, g * hd:(g + 1) * hd] for g in range(group)], axis=0)
    nt = (((1,), (1,)), ((), ()))
    m = acc = None
    for s0 in range(0, total, kv_block):
        s = lax.dot_general(q, k_all[s0:s0 + kv_block, :], nt, preferred_element_type=F32)
        m_blk = jnp.max(s, axis=-1, keepdims=True)
        m_new = m_blk if m is None else jnp.maximum(m, m_blk)
        pv = _dot(jnp.exp2(s - m_new).astype(BF16), v_ext[s0:s0 + kv_block, :])
        acc = pv if m is None else acc * jnp.exp2(m - m_new) + pv
        m = m_new
    o = (acc[:, 0:hd] / acc[:, hd:2 * hd]).astype(BF16)
    for g in range(group):
        o_ref[0, :, g * hd:(g + 1) * hd] = o[g * tq:(g + 1) * tq, :]


def _attention(q, k_l, v_l, k_c, v_c, *, tq=256, kv_block=768):
    bsz, seq, dq = q.shape
    ctx_len = k_c.shape[1]
    hd = HEAD_DIM
    n_kv = k_l.shape[-1] // hd
    group = dq // hd // n_kv
    total = seq + ctx_len
    kv_block = math.gcd(kv_block, total)
    kern = functools.partial(_attn_kernel, group=group, kv_block=kv_block)
    qspec = pl.BlockSpec((1, tq, group * hd), lambda b, h, i: (b, i, h))
    return pl.pallas_call(
        kern,
        grid=(bsz, n_kv, seq // tq),
        in_specs=[
            qspec,
            pl.BlockSpec((1, seq, hd), lambda b, h, i: (b, 0, h)),
            pl.BlockSpec((1, seq, hd), lambda b, h, i: (b, 0, h)),
            pl.BlockSpec((1, ctx_len, hd), lambda b, h, i: (b, 0, h)),
            pl.BlockSpec((1, ctx_len, hd), lambda b, h, i: (b, 0, h)),
        ],
        out_specs=qspec,
        out_shape=jax.ShapeDtypeStruct((bsz, seq, dq), BF16),
        scratch_shapes=[pltpu.VMEM((total, hd), BF16), pltpu.VMEM((total, 2 * hd), BF16)],
        compiler_params=_cparams("arbitrary", "arbitrary", "arbitrary"),
    )(q, k_l, v_l, k_c, v_c)


def _rope_tables(seq):
    rows = seq // GRID_W
    pairs = HEAD_DIM // 4
    row = jnp.repeat(jnp.arange(rows, dtype=F32), GRID_W)
    col = jnp.tile(jnp.arange(GRID_W, dtype=F32), rows)
    inv = ROPE_THETA ** (-jnp.arange(pairs, dtype=F32) / pairs)
    ang = jnp.concatenate([row[:, None] * inv[None, :], col[:, None] * inv[None, :]], axis=-1)
    cos, sin = jnp.cos(ang), jnp.sin(ang)
    return jnp.concatenate([cos, cos], axis=-1), jnp.concatenate([-sin, sin], axis=-1)


def _deinterleave(n_heads):
    hd = HEAD_DIM
    within = jnp.concatenate([jnp.arange(0, hd, 2), jnp.arange(1, hd, 2)])
    return (jnp.arange(n_heads)[:, None] * hd + within[None, :]).reshape(-1)


def kernel(x, c, ctx, c_ctx, mod_w, mod_b, mix_norm_pre, mix_norm_post, mlp_norm_pre, mlp_norm_post, mlp_w1, mlp_w2, hy_w_in, hy_b_in, hy_conv_w, hy_conv_b, hy_filt_w1, hy_filt_b1, hy_filt_freq1, hy_filt_w2, hy_filt_b2, hy_filt_freq2, hy_filt_w3, hy_filt_bias, hy_w_out, hy_b_out, attn_w_qkv, attn_q_norm, attn_k_norm, attn_w_o):
    bsz, seq, d = x.shape
    ctx_len = ctx.shape[1]
    hd = HEAD_DIM
    n_heads = d // hd
    tm = 512
    tmc = ctx_len

    ctx_row = bsz
    n_rows = -(-(bsz + 1) // MOD_ROWS_PAD) * MOD_ROWS_PAD
    cond = jnp.concatenate([c, c_ctx[None, :], jnp.zeros((n_rows - bsz - 1, d), F32)], axis=0)
    mods_all = _modulation(cond, mod_w, mod_b).reshape(mod_w.shape[0], n_rows, 6, 1, d)

    x_row = lambda i: i // (seq // tm)
    c_row = lambda i: ctx_row

    xf = x.reshape(bsz * seq, d)
    cf = ctx.reshape(bsz * ctx_len, d)

    mods = mods_all[0]
    w_in = hy_w_in[0].astype(BF16)
    w_out = hy_w_out[0].astype(BF16)
    filt = (hy_filt_w1[0], hy_filt_b1[0], hy_filt_freq1[0], hy_filt_w2[0], hy_filt_b2[0], hy_filt_freq2[0],
            hy_filt_w3[0], hy_filt_bias[0])
    w1 = mlp_w1[0].astype(BF16)
    w2 = mlp_w2[0].astype(BF16)

    def hyena(tokens, length, mod_row_b, mod_row_tile, tile):
        u, x1 = _hyena_in(tokens.reshape(bsz, length, d), mix_norm_pre[0], mods, mod_row_b,
                          w_in, hy_b_in[0], hy_conv_w[0], hy_conv_b[0])
        p, q = _hyena_filter_spectrum(length, *filt, _dft_table(length, shifted=False))
        gated = _fftconv(u, x1, p, q, _dft_table(length, shifted=True))
        return _proj_res(gated.reshape(bsz * length, d), tokens, w_out, hy_b_out[0], mix_norm_post[0],
                         mods, 2, mod_row_tile, tm=tile)

    xf = hyena(xf, seq, lambda b: b, x_row, tm)
    cf = hyena(cf, ctx_len, lambda b: ctx_row, c_row, tmc)
    xf = _mlp(xf, mlp_norm_pre[0], mlp_norm_post[0], mods, x_row, w1, w2, tm=tm)
    cf = _mlp(cf, mlp_norm_pre[0], mlp_norm_post[0], mods, c_row, w1, w2, tm=tmc)

    mods = mods_all[1]
    w_qkv = attn_w_qkv[0]
    n_qk = n_heads + N_KV_HEADS
    perm = _deinterleave(n_qk)
    w_qk = w_qkv[:, :n_qk * hd][:, perm]
    w_lat = jnp.concatenate([w_qk, w_qkv[:, n_qk * hd:]], axis=1).astype(BF16)
    w_ctx = w_lat[:, n_heads * hd:]
    within = _deinterleave(1)
    q_gain = attn_q_norm[0][within]
    k_gain = attn_k_norm[0][within]
    cos_t, sin_t = _rope_tables(seq)

    q, k_l, v_l = _qkv(xf, mix_norm_pre[1], mods, x_row, w_lat, q_gain, k_gain, cos_t, sin_t,
                       n_q=n_heads, n_kv=N_KV_HEADS, rope=True, tm=tm, seq=seq)
    k_c, v_c = _qkv(cf, mix_norm_pre[1], mods, c_row, w_ctx, q_gain, k_gain, cos_t, sin_t,
                       n_q=0, n_kv=N_KV_HEADS, rope=False, tm=tmc, seq=tmc)
    kv = N_KV_HEADS * hd
    o = _attention(q.reshape(bsz, seq, d), k_l.reshape(bsz, seq, kv), v_l.reshape(bsz, seq, kv),
                   k_c.reshape(bsz, ctx_len, kv), v_c.reshape(bsz, ctx_len, kv))
    xf = _proj_res(o.reshape(bsz * seq, d), xf, attn_w_o[0].astype(BF16), jnp.zeros((d,), F32),
                   mix_norm_post[1], mods, 2, x_row, tm=tm)
    xf = _mlp(xf, mlp_norm_pre[1], mlp_norm_post[1], mods, x_row, mlp_w1[1].astype(BF16),
              mlp_w2[1].astype(BF16), tm=tm)
    return xf.reshape(bsz, seq, d)
```

```python
import functools
import math

import jax
import jax.numpy as jnp
from jax import lax
from jax.experimental import pallas as pl
from jax.experimental.pallas import tpu as pltpu

F32 = jnp.float32
BF16 = jnp.bfloat16

EPS = 1e-6
GRID_W = 64
HY_BANDS = 16
HY_DECAY_TARGET = 1e-2
HY_FAST = 0.3
HY_SLOW = 1.5
HY_SHIFT = 0.0
HEAD_DIM = 128
N_KV_HEADS = 2
ROPE_THETA = 10000.0
ATTN_SCALE = HEAD_DIM ** -0.5
LOG2_E = math.log2(math.e)

MOD_ROWS_PAD = 8
SUBLANES = 8
MXU_WIDTH = 256

VMEM_LIMIT = 56 * 1024 * 1024


def _cparams(*sem):
    return pltpu.CompilerParams(dimension_semantics=sem, vmem_limit_bytes=VMEM_LIMIT)


def _const_spec(shape):
    nd = len(shape)
    return pl.BlockSpec(shape, lambda *_: (0,) * nd, pipeline_mode=pl.Buffered(1))


def _rms(x):
    return x * lax.rsqrt(jnp.mean(x * x, axis=-1, keepdims=True) + EPS)


def _norm_mod(x, gain, sc, sh):
    return _rms(x) * (gain * (1.0 + sc)) + sh


def _dot(a, b):
    return jnp.dot(a, b, preferred_element_type=F32)


def _dot_hi(a, b):
    return jnp.dot(a, b, preferred_element_type=F32, precision=lax.Precision.HIGHEST)


def _mod_kernel(cond_ref, w_ref, b_ref, o_ref):
    s = cond_ref[...]
    s = s * jax.nn.sigmoid(s)
    o_ref[0] = _dot(s.astype(BF16), w_ref[0].astype(BF16)) + b_ref[0]


def _modulation(cond, mod_w, mod_b):
    depth, d, n = mod_w.shape
    rows = cond.shape[0]
    tn = d
    return pl.pallas_call(
        _mod_kernel,
        grid=(depth, n // tn),
        in_specs=[
            pl.BlockSpec((rows, d), lambda i, j: (0, 0)),
            pl.BlockSpec((1, d, tn), lambda i, j: (i, 0, j)),
            pl.BlockSpec((1, 1, tn), lambda i, j: (i, 0, j)),
        ],
        out_specs=pl.BlockSpec((1, rows, tn), lambda i, j: (i, 0, j)),
        out_shape=jax.ShapeDtypeStruct((depth, rows, n), F32),
        compiler_params=_cparams("arbitrary", "arbitrary"),
    )(cond, mod_w, mod_b.reshape(depth, 1, n))


def _mod_block(d, chunk, row_fn):
    return pl.BlockSpec((1, 1, 1, d), lambda *idx: (row_fn(*idx), chunk, 0, 0))


def _hyena_in_kernel(x_ref, gain_ref, sc_ref, sh_ref, w0_ref, w1_ref, w2_ref, b_ref, cw_ref, cb_ref,
                     u_ref, x1_ref, h_buf, *, seq, row_chunk, conv_chunk):
    j = pl.program_id(1)
    w = u_ref.shape[-1]

    @pl.when(j == 0)
    def _():
        mult = gain_ref[...] * (1.0 + sc_ref[0, 0])
        sh = sh_ref[0, 0]

        def body(r, carry):
            rows = pl.ds(pl.multiple_of(r * row_chunk, row_chunk), row_chunk)
            h_buf[rows, :] = (_rms(x_ref[0, rows, :]) * mult + sh).astype(BF16)
            return carry

        lax.fori_loop(0, seq // row_chunk, body, 0)

    rc = conv_chunk
    n_chunks = seq // rc
    sub = lax.broadcasted_iota(jnp.int32, (SUBLANES, w), 0)
    w_refs = (w0_ref, w1_ref, w2_ref)
    taps = [[cw_ref[k, s:s + 1, :] for s in range(3)] for k in range(3)]
    bias = [cb_ref[s:s + 1, :] + b_ref[s:s + 1, :] * (taps[0][s] + taps[1][s] + taps[2][s]) for s in range(3)]

    def project(c):
        h = h_buf[c * rc:(c + 1) * rc, :]
        return [_dot(h, w_refs[s][...]) for s in range(3)]

    def conv(c, m_before, m_here, m_after):
        streams = []
        for s in range(3):
            m = m_here[s]
            row_before = -b_ref[s:s + 1, :] if m_before is None else m_before[s][rc - 1:rc, :]
            row_after = -b_ref[s:s + 1, :] if m_after is None else m_after[s][0:1, :]
            m_prev = pltpu.roll(m, 1, 0)
            m_prev = jnp.concatenate([jnp.where(sub == 0, row_before, m_prev[0:SUBLANES]),
                                      m_prev[SUBLANES:]], axis=0)
            m_next = pltpu.roll(m, rc - 1, 0)
            m_next = jnp.concatenate([m_next[:rc - SUBLANES],
                                      jnp.where(sub == SUBLANES - 1, row_after, m_next[rc - SUBLANES:])], axis=0)
            streams.append(m_prev * taps[0][s] + m * taps[1][s] + m_next * taps[2][s] + bias[s])
        x1, x2, v = streams
        u_ref[0, c * rc:(c + 1) * rc, :] = (v * x2).astype(BF16)
        x1_ref[0, c * rc:(c + 1) * rc, :] = x1.astype(BF16)

    zs = [project(0)]
    for c in range(1, n_chunks):
        zs.append(project(c))
        conv(c - 1, zs[c - 2] if c >= 2 else None, zs[c - 1], zs[c])
    conv(n_chunks - 1, zs[n_chunks - 2] if n_chunks >= 2 else None, zs[n_chunks - 1], None)


def _hyena_in(x, gain, mods, mod_row_fn, w_in, b_in, conv_w, conv_b, *, w=512):
    bsz, seq, d = x.shape
    w = min(w, d)
    nj = d // w
    row_chunk = min(seq, 256)
    kern = functools.partial(_hyena_in_kernel, seq=seq, row_chunk=row_chunk, conv_chunk=min(seq, 512))
    wspec = lambda s: pl.BlockSpec((d, w), lambda b, j: (0, s * nj + j))
    out_spec = pl.BlockSpec((1, seq, w), lambda b, j: (b, 0, j))
    return pl.pallas_call(
        kern,
        grid=(bsz, nj),
        in_specs=[
            pl.BlockSpec((1, seq, d), lambda b, j: (b, 0, 0)),
            pl.BlockSpec((1, d), lambda b, j: (0, 0)),
            _mod_block(d, 1, lambda b, j: mod_row_fn(b)),
            _mod_block(d, 0, lambda b, j: mod_row_fn(b)),
            wspec(0), wspec(1), wspec(2),
            pl.BlockSpec((3, w), lambda b, j: (0, j)),
            pl.BlockSpec((3, 3, w), lambda b, j: (0, 0, j)),
            pl.BlockSpec((3, w), lambda b, j: (0, j)),
        ],
        out_specs=[out_spec, out_spec],
        out_shape=[jax.ShapeDtypeStruct((bsz, seq, d), BF16)] * 2,
        scratch_shapes=[pltpu.VMEM((seq, d), BF16)],
        compiler_params=_cparams("arbitrary", "arbitrary"),
    )(x, gain.reshape(1, d), mods, mods, w_in, w_in, w_in,
      b_in.reshape(3, d), conv_w.reshape(3, 3, d), conv_b.reshape(3, d))


def _dft_table(seq, shifted, kb=32):
    if shifted:
        period = 8 * seq
        nn = 2 * jnp.arange(seq, dtype=jnp.int32)[None, :] + 1
    else:
        period = 4 * seq
        nn = jnp.arange(seq, dtype=jnp.int32)[None, :]
    kh = jnp.arange(seq // kb, dtype=jnp.int32)[:, None]
    kl = jnp.arange(kb, dtype=jnp.int32)[:, None]
    to_angle = lambda m: (m % period).astype(F32) * (2.0 * math.pi / period)
    alpha = to_angle(2 * kb * kh * nn)[:, None, :]
    beta = to_angle((2 * kl + 1) * nn)[None, :, :]
    ca, sa, cb, sb = jnp.cos(alpha), jnp.sin(alpha), jnp.cos(beta), jnp.sin(beta)
    cos_t = (ca * cb - sa * sb).reshape(seq, seq)
    sin_t = (sa * cb + ca * sb).reshape(seq, seq)
    return jnp.concatenate([cos_t, sin_t], axis=0).astype(BF16)


def _filter_kernel(bands_ref, w1t_ref, w1c_ref, w1s_ref, b1_ref, f1_ref, w2_ref, b2_ref, f2_ref,
                   w3f_ref, w3b_ref, deltas_ref, bias_ref, tab_ref, p_ref, q_ref, h_buf, *, seq):
    w = p_ref.shape[-1]

    @pl.when(pl.program_id(0) == 0)
    def _():
        t64 = lax.broadcasted_iota(jnp.int32, (seq, w1t_ref.shape[-1]), 0).astype(F32) / seq
        t16 = lax.broadcasted_iota(jnp.int32, (seq, HY_BANDS), 0).astype(F32) / seq
        ang = (2.0 * math.pi * t16) * bands_ref[...]
        pre = t64 * w1t_ref[...] + _dot_hi(jnp.cos(ang), w1c_ref[...]) + _dot_hi(jnp.sin(ang), w1s_ref[...])
        h = jnp.sin(f1_ref[...] * (pre + b1_ref[...]))
        h_buf[...] = jnp.sin(f2_ref[...] * (_dot_hi(h, w2_ref[...]) + b2_ref[...]))

    h = h_buf[...]
    tw = lax.broadcasted_iota(jnp.int32, (seq, w), 0).astype(F32) / seq
    decay = jnp.exp(-tw * deltas_ref[...]) + HY_SHIFT
    h_f = _dot_hi(h, w3f_ref[...]) * decay
    h_b = _dot_hi(h, w3b_ref[...]) * decay
    row = lax.broadcasted_iota(jnp.int32, (seq, w), 0)
    h_b = jnp.where(row == 0, 0.0, h_b)
    p_ref[...] = _dot(tab_ref[pl.ds(0, seq), :], (h_f + h_b).astype(BF16)) + bias_ref[...]
    q_ref[...] = _dot(tab_ref[pl.ds(seq, seq), :], (h_b - h_f).astype(BF16))


def _hyena_filter_spectrum(seq, fw1, fb1, ff1, fw2, fb2, ff2, fw3, fbias, tab_plain, *, w=256):
    d = fbias.shape[-1]
    fwid = fw2.shape[0]
    bands = jnp.linspace(1e-4, HY_BANDS - 1, HY_BANDS, dtype=F32).reshape(1, HY_BANDS)
    deltas = jnp.abs(jnp.linspace(math.log(HY_DECAY_TARGET) / HY_SLOW, math.log(HY_DECAY_TARGET) / HY_FAST,
                                  d, dtype=F32)).reshape(1, d)
    nj = d // w
    small = lambda shape: pl.BlockSpec(shape, lambda j: (0,) * len(shape))
    kern = functools.partial(_filter_kernel, seq=seq)
    return pl.pallas_call(
        kern,
        grid=(nj,),
        in_specs=[
            small((1, HY_BANDS)), small((1, fwid)), small((HY_BANDS, fwid)), small((HY_BANDS, fwid)),
            small((1, fwid)), small((1, fwid)), small((fwid, fwid)), small((1, fwid)), small((1, fwid)),
            pl.BlockSpec((fwid, w), lambda j: (0, j)),
            pl.BlockSpec((fwid, w), lambda j: (0, nj + j)),
            pl.BlockSpec((1, w), lambda j: (0, j)),
            pl.BlockSpec((1, w), lambda j: (0, j)),
            _const_spec((2 * seq, seq)),
        ],
        out_specs=[pl.BlockSpec((seq, w), lambda j: (0, j))] * 2,
        out_shape=[jax.ShapeDtypeStruct((seq, d), F32)] * 2,
        scratch_shapes=[pltpu.VMEM((seq, fwid), F32)],
        compiler_params=_cparams("arbitrary"),
    )(bands, fw1[0:1], fw1[1:1 + HY_BANDS], fw1[1 + HY_BANDS:], fb1.reshape(1, fwid), ff1.reshape(1, fwid),
      fw2, fb2.reshape(1, fwid), ff2.reshape(1, fwid), fw3, fw3, deltas, fbias.reshape(1, d), tab_plain)


def _fftconv_kernel(u_ref, x1_ref, p_ref, q_ref, tab_ref, o_ref, *, seq):
    u = u_ref[0]
    a = _dot(tab_ref[pl.ds(0, seq), :], u)
    b = _dot(tab_ref[pl.ds(seq, seq), :], u)
    p = p_ref[...]
    q = q_ref[...]
    y_re = (a * p + b * q).astype(BF16)
    y_im = (a * q - b * p).astype(BF16)
    y = _dot(tab_ref[pl.ds(0, seq), :], y_re) - _dot(tab_ref[pl.ds(seq, seq), :], y_im)
    o_ref[0] = (y * (1.0 / seq) * x1_ref[0].astype(F32)).astype(BF16)


def _fftconv(u, x1, p, q, tab_shift, *, w=256):
    bsz, seq, d = u.shape
    nj = d // w
    act = pl.BlockSpec((1, seq, w), lambda j, b: (b, 0, j))
    spec = pl.BlockSpec((seq, w), lambda j, b: (0, j))
    kern = functools.partial(_fftconv_kernel, seq=seq)
    return pl.pallas_call(
        kern,
        grid=(nj, bsz),
        in_specs=[act, act, spec, spec, _const_spec((2 * seq, seq))],
        out_specs=act,
        out_shape=jax.ShapeDtypeStruct((bsz, seq, d), BF16),
        compiler_params=_cparams("arbitrary", "arbitrary"),
    )(u, x1, p, q, tab_shift)


def _proj_res_kernel(a_ref, x_ref, w_ref, b_ref, gain_ref, g_ref, o_ref):
    y = _dot(a_ref[...], w_ref[...]) + b_ref[...]
    o_ref[...] = x_ref[...] + g_ref[0, 0] * (_rms(y) * gain_ref[...])


def _proj_res(a, x, w, b, gain, mods, gate_chunk, mod_row_fn, *, tm):
    rows, d = x.shape
    k = a.shape[-1]
    return pl.pallas_call(
        _proj_res_kernel,
        grid=(rows // tm,),
        in_specs=[
            pl.BlockSpec((tm, k), lambda i: (i, 0)),
            pl.BlockSpec((tm, d), lambda i: (i, 0)),
            _const_spec((k, d)),
            pl.BlockSpec((1, d), lambda i: (0, 0)),
            pl.BlockSpec((1, d), lambda i: (0, 0)),
            _mod_block(d, gate_chunk, mod_row_fn),
        ],
        out_specs=pl.BlockSpec((tm, d), lambda i: (i, 0)),
        out_shape=jax.ShapeDtypeStruct((rows, d), F32),
        compiler_params=_cparams("arbitrary"),
    )(a, x, w, b.reshape(1, d), gain.reshape(1, d), mods)


def _mlp_kernel(x_ref, pre_ref, sc_ref, sh_ref, g_ref, w1_ref, w2_ref, post_ref, o_ref, *, ff_chunk):
    x = x_ref[...]
    h = _norm_mod(x, pre_ref[...], sc_ref[0, 0], sh_ref[0, 0]).astype(BF16)
    d_ff = w1_ref.shape[-1]
    acc = None
    for c in range(d_ff // ff_chunk):
        cols = pl.ds(c * ff_chunk, ff_chunk)
        a = jnp.maximum(_dot(h, w1_ref[:, cols]), 0.0)
        part = _dot((a * a).astype(BF16), w2_ref[cols, :])
        acc = part if acc is None else acc + part
    o_ref[...] = x + g_ref[0, 0] * (_rms(acc) * post_ref[...])


def _mlp(x, pre, post, mods, mod_row_fn, w1, w2, *, tm, ff_chunk=1024):
    rows, d = x.shape
    d_ff = w1.shape[-1]
    kern = functools.partial(_mlp_kernel, ff_chunk=ff_chunk)
    vec = pl.BlockSpec((1, d), lambda i: (0, 0))
    return pl.pallas_call(
        kern,
        grid=(rows // tm,),
        in_specs=[
            pl.BlockSpec((tm, d), lambda i: (i, 0)),
            vec,
            _mod_block(d, 4, mod_row_fn),
            _mod_block(d, 3, mod_row_fn),
            _mod_block(d, 5, mod_row_fn),
            _const_spec((d, d_ff)),
            _const_spec((d_ff, d)),
            vec,
        ],
        out_specs=pl.BlockSpec((tm, d), lambda i: (i, 0)),
        out_shape=jax.ShapeDtypeStruct((rows, d), F32),
        compiler_params=_cparams("arbitrary"),
    )(x, pre.reshape(1, d), mods, mods, mods, w1, w2, post.reshape(1, d))


def _qkv_kernel(x_ref, pre_ref, sc_ref, sh_ref, w_ref, gains_ref, cos_ref, sin_ref,
                *out_refs, n_q_pairs, rope, sub_rows):
    q_ref = out_refs[0] if n_q_pairs else None
    k_ref, v_ref = out_refs[-2:]
    hd = HEAD_DIM
    tm = x_ref.shape[0]
    mult = pre_ref[...] * (1.0 + sc_ref[0, 0])
    sh = sh_ref[0, 0]
    q_scale = ATTN_SCALE * LOG2_E
    gain_ab = [(gains_ref[0:1, :] * q_scale, gains_ref[1:2, :] * q_scale), (gains_ref[2:3, :], gains_ref[3:4, :])]
    half_r = lax.broadcasted_iota(jnp.int32, (hd, hd), 0) // (hd // 2)
    half_c = lax.broadcasted_iota(jnp.int32, (hd, hd), 1) // (hd // 2)
    same_head = (half_r == half_c).astype(BF16)

    def rows_of(i):
        return slice(i * sub_rows, (i + 1) * sub_rows)

    def project(i):
        h = (_rms(x_ref[rows_of(i), :]) * mult + sh).astype(BF16)
        return _dot(h, w_ref[...])

    def finish(i, qkv):
        rows = rows_of(i)
        if rope:
            cos_t, sin_t = cos_ref[rows, :], sin_ref[rows, :]
            tabs = [(ga * cos_t, gb * sin_t, ga * sin_t, gb * cos_t) for ga, gb in gain_ab]

        def pair(idx, kind):
            a = qkv[:, 2 * idx * hd:(2 * idx + 1) * hd]
            b = qkv[:, (2 * idx + 1) * hd:(2 * idx + 2) * hd]
            ssq = _dot((a * a + b * b).astype(BF16), same_head)
            r = lax.rsqrt(ssq * (1.0 / hd) + EPS)
            if rope:
                ca, sb, sa, cb = tabs[kind]
                return (r * (a * ca - b * sb)).astype(BF16), (r * (a * sa + b * cb)).astype(BF16)
            ga, gb = gain_ab[kind]
            return (r * (a * ga)).astype(BF16), (r * (b * gb)).astype(BF16)

        for p in range(n_q_pairs):
            qa, qb = pair(p, 0)
            q_ref[rows, 2 * p * hd:(2 * p + 1) * hd] = qa
            q_ref[rows, (2 * p + 1) * hd:(2 * p + 2) * hd] = qb
        ka, kb = pair(n_q_pairs, 1)
        k_ref[rows, 0:hd] = ka
        k_ref[rows, hd:2 * hd] = kb
        v0 = 2 * (n_q_pairs + 1) * hd
        v_ref[rows, :] = qkv[:, v0:v0 + N_KV_HEADS * hd].astype(BF16)

    n_sub = tm // sub_rows
    pending = project(0)
    for i in range(1, n_sub):
        nxt = project(i)
        finish(i - 1, pending)
        pending = nxt
    finish(n_sub - 1, pending)


def _qkv(x, pre, mods, mod_row_fn, w, gains, cos_t, sin_t, *, n_q_pairs, rope, tm, seq):
    rows, d = x.shape
    hd = HEAD_DIM
    n = w.shape[-1]
    kern = functools.partial(_qkv_kernel, n_q_pairs=n_q_pairs, rope=rope, sub_rows=min(tm, 256))
    vec = pl.BlockSpec((1, d), lambda i: (0, 0))
    tiles_per_seq = seq // tm
    pos = pl.BlockSpec((tm, hd), lambda i: (i % tiles_per_seq, 0))
    widths = ([2 * n_q_pairs * hd] if n_q_pairs else []) + [N_KV_HEADS * hd] * 2
    return pl.pallas_call(
        kern,
        grid=(rows // tm,),
        in_specs=[
            pl.BlockSpec((tm, d), lambda i: (i, 0)),
            vec,
            _mod_block(d, 1, mod_row_fn),
            _mod_block(d, 0, mod_row_fn),
            _const_spec((d, n)),
            pl.BlockSpec((4, hd), lambda i: (0, 0)),
            pos, pos,
        ],
        out_specs=[pl.BlockSpec((tm, wd), lambda i: (i, 0)) for wd in widths],
        out_shape=[jax.ShapeDtypeStruct((rows, wd), BF16) for wd in widths],
        compiler_params=_cparams("arbitrary"),
    )(x, pre.reshape(1, d), mods, mods, w, gains, cos_t, sin_t)


def _attn_kernel(q_ref, kl_ref, vl_ref, kc_ref, vc_ref, o_ref, k_all, v_ext, *, n_pairs, kv_blocks):
    hd = HEAD_DIM
    tq = q_ref.shape[1]
    seq = kl_ref.shape[1]
    total = k_all.shape[0]
    kv_head = pl.program_id(1)

    @pl.when(pl.program_id(2) == 0)
    def _():
        k_all[0:seq, :] = kl_ref[0]
        k_all[seq:total, :] = kc_ref[0]
        v_ext[0:seq, 0:hd] = vl_ref[0]
        v_ext[seq:total, 0:hd] = vc_ref[0]
        v_ext[:, hd:2 * hd] = jnp.ones((total, hd), BF16)

    lane_half = (lax.broadcasted_iota(jnp.int32, (1, 2 * hd), 1) % hd) // (hd // 2)
    keep = (lane_half == kv_head).astype(BF16)
    q = jnp.concatenate([q_ref[0, :, 2 * p * hd:(2 * p + 2) * hd] * keep for p in range(n_pairs)], axis=0)
    nt = (((1,), (1,)), ((), ()))
    m = acc = None
    scores = lambda blk: lax.dot_general(q, k_all[blk[0]:blk[1], :], nt, preferred_element_type=F32)
    s_next = scores(kv_blocks[0])
    for i, (s0, s1) in enumerate(kv_blocks):
        s = s_next
        if i + 1 < len(kv_blocks):
            s_next = scores(kv_blocks[i + 1])
        m_blk = jnp.max(s, axis=-1, keepdims=True)
        m_new = m_blk if m is None else jnp.maximum(m, m_blk)
        pv = _dot(jnp.exp2(s - m_new).astype(BF16), v_ext[s0:s1, :])
        acc = pv if m is None else acc * jnp.exp2(m - m_new) + pv
        m = m_new
    o = (acc[:, 0:hd] / acc[:, hd:2 * hd]).astype(BF16)
    for p in range(n_pairs):
        o_ref[0, :, p * hd:(p + 1) * hd] = o[p * tq:(p + 1) * tq, :]


def _kv_blocks(total, pattern):
    assert total % MXU_WIDTH == 0
    n_tiles = total // MXU_WIDTH
    blocks, start, i = [], 0, 0
    while start < n_tiles:
        size = min(pattern[i % len(pattern)], n_tiles - start)
        blocks.append((start * MXU_WIDTH, (start + size) * MXU_WIDTH))
        start += size
        i += 1
    return tuple(blocks)


def _attention(q, k_l, v_l, k_c, v_c, *, tq=256, kv_pattern=(5, 3, 1)):
    bsz, seq, dq = q.shape
    ctx_len = k_c.shape[1]
    hd = HEAD_DIM
    n_pairs = dq // (2 * hd)
    total = seq + ctx_len
    kern = functools.partial(_attn_kernel, n_pairs=n_pairs, kv_blocks=_kv_blocks(total, kv_pattern))
    return pl.pallas_call(
        kern,
        grid=(bsz, N_KV_HEADS, seq // tq),
        in_specs=[
            pl.BlockSpec((1, tq, dq), lambda b, h, i: (b, i, 0)),
            pl.BlockSpec((1, seq, 2 * hd), lambda b, h, i: (b, 0, 0)),
            pl.BlockSpec((1, seq, hd), lambda b, h, i: (b, 0, h)),
            pl.BlockSpec((1, ctx_len, 2 * hd), lambda b, h, i: (b, 0, 0)),
            pl.BlockSpec((1, ctx_len, hd), lambda b, h, i: (b, 0, h)),
        ],
        out_specs=pl.BlockSpec((1, tq, n_pairs * hd), lambda b, h, i: (b, i, h)),
        out_shape=jax.ShapeDtypeStruct((bsz, seq, dq), BF16),
        scratch_shapes=[pltpu.VMEM((total, 2 * hd), BF16), pltpu.VMEM((total, 2 * hd), BF16)],
        compiler_params=_cparams("arbitrary", "arbitrary", "arbitrary"),
    )(q, k_l, v_l, k_c, v_c)


def _rope_tables(seq):
    rows = seq // GRID_W
    pairs = HEAD_DIM // 4
    row = jnp.repeat(jnp.arange(rows, dtype=F32), GRID_W)
    col = jnp.tile(jnp.arange(GRID_W, dtype=F32), rows)
    inv = ROPE_THETA ** (-jnp.arange(pairs, dtype=F32) / pairs)
    ang = jnp.concatenate([row[:, None] * inv[None, :], col[:, None] * inv[None, :]], axis=-1)
    cos, sin = jnp.cos(ang), jnp.sin(ang)
    return jnp.concatenate([cos, cos], axis=-1), jnp.concatenate([sin, sin], axis=-1)


def _pair_columns(head_a, head_b):
    hd = HEAD_DIM
    even, odd = jnp.arange(0, hd, 2), jnp.arange(1, hd, 2)
    return jnp.concatenate([head_a * hd + even, head_b * hd + even, head_a * hd + odd, head_b * hd + odd])


def kernel(x, c, ctx, c_ctx, mod_w, mod_b, mix_norm_pre, mix_norm_post, mlp_norm_pre, mlp_norm_post, mlp_w1, mlp_w2, hy_w_in, hy_b_in, hy_conv_w, hy_conv_b, hy_filt_w1, hy_filt_b1, hy_filt_freq1, hy_filt_w2, hy_filt_b2, hy_filt_freq2, hy_filt_w3, hy_filt_bias, hy_w_out, hy_b_out, attn_w_qkv, attn_q_norm, attn_k_norm, attn_w_o):
    bsz, seq, d = x.shape
    ctx_len = ctx.shape[1]
    hd = HEAD_DIM
    n_heads = d // hd
    tm = 512
    tmc = ctx_len

    ctx_row = bsz
    n_rows = -(-(bsz + 1) // MOD_ROWS_PAD) * MOD_ROWS_PAD
    cond = jnp.concatenate([c, c_ctx[None, :], jnp.zeros((n_rows - bsz - 1, d), F32)], axis=0)
    mods_all = _modulation(cond, mod_w, mod_b).reshape(mod_w.shape[0], n_rows, 6, 1, d)

    x_row = lambda i: i // (seq // tm)
    c_row = lambda i: ctx_row

    xf = x.reshape(bsz * seq, d)
    cf = ctx.reshape(bsz * ctx_len, d)

    mods = mods_all[0]
    w_in = hy_w_in[0].astype(BF16)
    w_out = hy_w_out[0].astype(BF16)
    filt = (hy_filt_w1[0], hy_filt_b1[0], hy_filt_freq1[0], hy_filt_w2[0], hy_filt_b2[0], hy_filt_freq2[0],
            hy_filt_w3[0], hy_filt_bias[0])
    w1 = mlp_w1[0].astype(BF16)
    w2 = mlp_w2[0].astype(BF16)

    def hyena(tokens, length, mod_row_b, mod_row_tile, tile):
        u, x1 = _hyena_in(tokens.reshape(bsz, length, d), mix_norm_pre[0], mods, mod_row_b,
                          w_in, hy_b_in[0], hy_conv_w[0], hy_conv_b[0])
        p, q = _hyena_filter_spectrum(length, *filt, _dft_table(length, shifted=False))
        gated = _fftconv(u, x1, p, q, _dft_table(length, shifted=True))
        return _proj_res(gated.reshape(bsz * length, d), tokens, w_out, hy_b_out[0], mix_norm_post[0],
                         mods, 2, mod_row_tile, tm=tile)

    xf = hyena(xf, seq, lambda b: b, x_row, tm)
    cf = hyena(cf, ctx_len, lambda b: ctx_row, c_row, tmc)
    xf = _mlp(xf, mlp_norm_pre[0], mlp_norm_post[0], mods, x_row, w1, w2, tm=tm)
    cf = _mlp(cf, mlp_norm_pre[0], mlp_norm_post[0], mods, c_row, w1, w2, tm=tmc)

    mods = mods_all[1]
    w_qkv = attn_w_qkv[0]
    assert N_KV_HEADS == 2 and n_heads % 2 == 0
    group = n_heads // N_KV_HEADS
    pair_cols = [_pair_columns(p, group + p) for p in range(group)] + [_pair_columns(n_heads, n_heads + 1)]
    v_cols = jnp.arange((n_heads + N_KV_HEADS) * hd, (n_heads + 2 * N_KV_HEADS) * hd)
    w_lat = w_qkv[:, jnp.concatenate(pair_cols + [v_cols])].astype(BF16)
    w_ctx = w_lat[:, n_heads * hd:]
    even, odd = jnp.arange(0, hd, 2), jnp.arange(1, hd, 2)
    gains = jnp.stack([jnp.tile(g[idx], 2) for g in (attn_q_norm[0], attn_k_norm[0]) for idx in (even, odd)])
    cos_t, sin_t = _rope_tables(seq)

    q, k_l, v_l = _qkv(xf, mix_norm_pre[1], mods, x_row, w_lat, gains, cos_t, sin_t,
                       n_q_pairs=group, rope=True, tm=tm, seq=seq)
    k_c, v_c = _qkv(cf, mix_norm_pre[1], mods, c_row, w_ctx, gains, cos_t, sin_t,
                    n_q_pairs=0, rope=False, tm=tmc, seq=tmc)
    kv = N_KV_HEADS * hd
    o = _attention(q.reshape(bsz, seq, d), k_l.reshape(bsz, seq, kv), v_l.reshape(bsz, seq, kv),
                   k_c.reshape(bsz, ctx_len, kv), v_c.reshape(bsz, ctx_len, kv))
    xf = _proj_res(o.reshape(bsz * seq, d), xf, attn_w_o[0].astype(BF16), jnp.zeros((d,), F32),
                   mix_norm_post[1], mods, 2, x_row, tm=tm)
    xf = _mlp(xf, mlp_norm_pre[1], mlp_norm_post[1], mods, x_row, mlp_w1[1].astype(BF16),
              mlp_w2[1].astype(BF16), tm=tm)
    return xf.reshape(bsz, seq, d)
```

```python
import functools
import math

import jax
import jax.numpy as jnp
from jax import lax
from jax.experimental import pallas as pl
from jax.experimental.pallas import tpu as pltpu

F32 = jnp.float32
BF16 = jnp.bfloat16

EPS = 1e-6
GRID_W = 64
HY_BANDS = 16
HY_DECAY_TARGET = 1e-2
HY_FAST = 0.3
HY_SLOW = 1.5
HY_SHIFT = 0.0
HEAD_DIM = 128
N_KV_HEADS = 2
ROPE_THETA = 10000.0
ATTN_SCALE = HEAD_DIM ** -0.5
LOG2_E = math.log2(math.e)

MOD_ROWS_PAD = 8
SUBLANES = 8
LANES = 128
MXU_WIDTH = 256

VMEM_LIMIT = 56 * 1024 * 1024


def _cparams(*sem):
    return pltpu.CompilerParams(dimension_semantics=sem, vmem_limit_bytes=VMEM_LIMIT)


def _const_spec(shape):
    nd = len(shape)
    return pl.BlockSpec(shape, lambda *_: (0,) * nd, pipeline_mode=pl.Buffered(1))


def _rms(x):
    return x * lax.rsqrt(jnp.mean(x * x, axis=-1, keepdims=True) + EPS)


def _norm_mod(x, gain, sc, sh):
    return _rms(x) * (gain * (1.0 + sc)) + sh


def _dot(a, b):
    return jnp.dot(a, b, preferred_element_type=F32)


def _dot_hi(a, b):
    return jnp.dot(a, b, preferred_element_type=F32, precision=lax.Precision.HIGHEST)


def _mod_kernel(cond_ref, w_ref, b_ref, o_ref):
    s = cond_ref[...]
    s = s * jax.nn.sigmoid(s)
    o_ref[0] = _dot(s.astype(BF16), w_ref[0].astype(BF16)) + b_ref[0]


def _modulation(cond, mod_w, mod_b):
    depth, d, n = mod_w.shape
    rows = cond.shape[0]
    tn = d
    return pl.pallas_call(
        _mod_kernel,
        grid=(depth, n // tn),
        in_specs=[
            pl.BlockSpec((rows, d), lambda i, j: (0, 0)),
            pl.BlockSpec((1, d, tn), lambda i, j: (i, 0, j)),
            pl.BlockSpec((1, 1, tn), lambda i, j: (i, 0, j)),
        ],
        out_specs=pl.BlockSpec((1, rows, tn), lambda i, j: (i, 0, j)),
        out_shape=jax.ShapeDtypeStruct((depth, rows, n), F32),
        compiler_params=_cparams("arbitrary", "arbitrary"),
    )(cond, mod_w, mod_b.reshape(depth, 1, n))


def _mod_block(d, chunk, row_fn):
    return pl.BlockSpec((1, 1, 1, d), lambda *idx: (row_fn(*idx), chunk, 0, 0))


def _hyena_in_kernel(x_ref, gain_ref, sc_ref, sh_ref, w0_ref, w1_ref, w2_ref, b_ref, cw_ref, cb_ref,
                     u_ref, x1_ref, h_buf, x_slabs, *, seq, rc):
    j = pl.program_id(1)
    w = u_ref.shape[-1]
    half = seq // 2
    n_chunks = half // rc
    n_slabs = x_slabs.shape[0]

    @pl.when(j == 0)
    def _():
        mult = gain_ref[...] * (1.0 + sc_ref[0, 0])
        sh = sh_ref[0, 0]
        for k in range(n_slabs):
            x_slabs[k] = x_ref[0, :, k * LANES:(k + 1) * LANES]
        for parity in range(2):
            def body(c, carry):
                src = pl.ds(2 * c * rc + parity, rc, stride=2)
                dst = pl.ds(pl.multiple_of(2 * c * rc + parity * rc, rc), rc)
                xs = jnp.concatenate([x_slabs[k, src, :] for k in range(n_slabs)], axis=1)
                h_buf[dst, :] = (_rms(xs) * mult + sh).astype(BF16)
                return carry

            lax.fori_loop(0, n_chunks, body, 0)

    sub = lax.broadcasted_iota(jnp.int32, (SUBLANES, w), 0)
    w_refs = (w0_ref, w1_ref, w2_ref)
    taps = [[cw_ref[k, s:s + 1, :] for s in range(3)] for k in range(3)]
    bias = [cb_ref[s:s + 1, :] + b_ref[s:s + 1, :] * (taps[0][s] + taps[1][s] + taps[2][s]) for s in range(3)]

    def project(c):
        h = h_buf[2 * c * rc:2 * (c + 1) * rc, :]
        return [_dot(h, w_refs[s][...]) for s in range(3)]

    def conv(c, m_before, m_here, m_after):
        even_streams, odd_streams = [], []
        for s in range(3):
            m_even, m_odd = m_here[s][0:rc], m_here[s][rc:2 * rc]
            row_before = -b_ref[s:s + 1, :] if m_before is None else m_before[s][2 * rc - 1:2 * rc, :]
            row_after = -b_ref[s:s + 1, :] if m_after is None else m_after[s][0:1, :]
            odd_prev = pltpu.roll(m_odd, 1, 0)
            odd_prev = jnp.concatenate([jnp.where(sub == 0, row_before, odd_prev[0:SUBLANES]),
                                        odd_prev[SUBLANES:]], axis=0)
            even_next = pltpu.roll(m_even, rc - 1, 0)
            even_next = jnp.concatenate([even_next[:rc - SUBLANES],
                                         jnp.where(sub == SUBLANES - 1, row_after, even_next[rc - SUBLANES:])], axis=0)
            even_streams.append(odd_prev * taps[0][s] + m_even * taps[1][s] + m_odd * taps[2][s] + bias[s])
            odd_streams.append(m_even * taps[0][s] + m_odd * taps[1][s] + even_next * taps[2][s] + bias[s])
        for base, (x1, x2, v) in ((0, even_streams), (half, odd_streams)):
            u_ref[0, base + c * rc:base + (c + 1) * rc, :] = (v * x2).astype(BF16)
            x1_ref[0, base + c * rc:base + (c + 1) * rc, :] = x1.astype(BF16)

    zs = [project(0)]
    for c in range(1, n_chunks):
        zs.append(project(c))
        conv(c - 1, zs[c - 2] if c >= 2 else None, zs[c - 1], zs[c])
    conv(n_chunks - 1, zs[n_chunks - 2] if n_chunks >= 2 else None, zs[n_chunks - 1], None)


def _hyena_in(x, gain, mods, mod_row_fn, w_in, b_in, conv_w, conv_b, *, w=512, rc=256):
    bsz, seq, d = x.shape
    w = min(w, d)
    nj = d // w
    rc = min(rc, seq // 2)
    kern = functools.partial(_hyena_in_kernel, seq=seq, rc=rc)
    wspec = lambda s: pl.BlockSpec((d, w), lambda b, j: (0, s * nj + j))
    out_spec = pl.BlockSpec((1, seq, w), lambda b, j: (b, 0, j))
    return pl.pallas_call(
        kern,
        grid=(bsz, nj),
        in_specs=[
            pl.BlockSpec((1, seq, d), lambda b, j: (b, 0, 0)),
            pl.BlockSpec((1, d), lambda b, j: (0, 0)),
            _mod_block(d, 1, lambda b, j: mod_row_fn(b)),
            _mod_block(d, 0, lambda b, j: mod_row_fn(b)),
            wspec(0), wspec(1), wspec(2),
            pl.BlockSpec((3, w), lambda b, j: (0, j)),
            pl.BlockSpec((3, 3, w), lambda b, j: (0, 0, j)),
            pl.BlockSpec((3, w), lambda b, j: (0, j)),
        ],
        out_specs=[out_spec, out_spec],
        out_shape=[jax.ShapeDtypeStruct((bsz, seq, d), BF16)] * 2,
        scratch_shapes=[pltpu.VMEM((seq, d), BF16), pltpu.VMEM((d // LANES, seq, LANES), F32)],
        compiler_params=_cparams("arbitrary", "arbitrary"),
    )(x, gain.reshape(1, d), mods, mods, w_in, w_in, w_in,
      b_in.reshape(3, d), conv_w.reshape(3, 3, d), conv_b.reshape(3, d))


def _dft_table(length, n_freq, shifted, kb=32):
    if shifted:
        period = 8 * length
        nn = 2 * jnp.arange(length, dtype=jnp.int32)[None, :] + 1
    else:
        period = 4 * length
        nn = jnp.arange(length, dtype=jnp.int32)[None, :]
    kh = jnp.arange(n_freq // kb, dtype=jnp.int32)[:, None]
    kl = jnp.arange(kb, dtype=jnp.int32)[:, None]
    to_angle = lambda m: (m % period).astype(F32) * (2.0 * math.pi / period)
    alpha = to_angle(2 * kb * kh * nn)[:, None, :]
    beta = to_angle((2 * kl + 1) * nn)[None, :, :]
    ca, sa, cb, sb = jnp.cos(alpha), jnp.sin(alpha), jnp.cos(beta), jnp.sin(beta)
    cos_t = (ca * cb - sa * sb).reshape(n_freq, length)
    sin_t = (sa * cb + ca * sb).reshape(n_freq, length)
    return jnp.concatenate([cos_t, sin_t], axis=0).astype(BF16)


def _filter_kernel(bands_ref, w1t_ref, w1c_ref, w1s_ref, b1_ref, f1_ref, w2_ref, b2_ref, f2_ref,
                   w3f_ref, w3b_ref, deltas_ref, bias_ref, psi_ref, tab_ref, o_ref, h_buf, *, seq):
    w = o_ref.shape[-1]
    half = seq // 2

    @pl.when(pl.program_id(0) == 0)
    def _():
        t64 = lax.broadcasted_iota(jnp.int32, (seq, w1t_ref.shape[-1]), 0).astype(F32) / seq
        t16 = lax.broadcasted_iota(jnp.int32, (seq, HY_BANDS), 0).astype(F32) / seq
        ang = (2.0 * math.pi * t16) * bands_ref[...]
        pre = t64 * w1t_ref[...] + _dot_hi(jnp.cos(ang), w1c_ref[...]) + _dot_hi(jnp.sin(ang), w1s_ref[...])
        h = jnp.sin(f1_ref[...] * (pre + b1_ref[...]))
        h_buf[...] = jnp.sin(f2_ref[...] * (_dot_hi(h, w2_ref[...]) + b2_ref[...]))

    h = h_buf[...]
    row = lax.broadcasted_iota(jnp.int32, (seq, w), 0)
    decay = jnp.exp(-(row.astype(F32) / seq) * deltas_ref[...]) + HY_SHIFT
    h_f = _dot_hi(h, w3f_ref[...]) * decay
    h_b = _dot_hi(h, w3b_ref[...]) * decay
    h_b = jnp.where(row == 0, 0.0, h_b)
    h_sum = h_f + h_b
    h_dif = h_b - h_f
    alt = jnp.where(row % 2 == 0, 1.0, -1.0)
    cos_lo = tab_ref[pl.ds(0, half), :]
    sin_lo = tab_ref[pl.ds(half, half), :]
    bias = bias_ref[...]
    p_lo = _dot(cos_lo, h_sum.astype(BF16)) + bias
    p_hi = _dot(cos_lo, (h_sum * alt).astype(BF16)) + bias
    q_lo = _dot(sin_lo, h_dif.astype(BF16))
    q_hi = -_dot(sin_lo, (h_dif * alt).astype(BF16))
    d_re = p_lo - p_hi
    d_im = q_lo + q_hi
    psi_c = jnp.concatenate([psi_ref[0]] * (w // psi_ref.shape[-1]), axis=1)
    psi_s = jnp.concatenate([psi_ref[1]] * (w // psi_ref.shape[-1]), axis=1)
    o_ref[0] = p_lo + p_hi
    o_ref[1] = q_lo - q_hi
    o_ref[2] = d_re * psi_c - d_im * psi_s
    o_ref[3] = d_re * psi_s + d_im * psi_c
    o_ref[4] = d_re * psi_c + d_im * psi_s
    o_ref[5] = d_im * psi_c - d_re * psi_s


def _hyena_filter_tables(seq, fw1, fb1, ff1, fw2, fb2, ff2, fw3, fbias, *, w=256):
    d = fbias.shape[-1]
    fwid = fw2.shape[0]
    half = seq // 2
    lanes = LANES
    bands = jnp.linspace(1e-4, HY_BANDS - 1, HY_BANDS, dtype=F32).reshape(1, HY_BANDS)
    deltas = jnp.abs(jnp.linspace(math.log(HY_DECAY_TARGET) / HY_SLOW, math.log(HY_DECAY_TARGET) / HY_FAST,
                                  d, dtype=F32)).reshape(1, d)
    theta = (2 * jnp.arange(half, dtype=F32) + 1.0) * (math.pi / (2 * seq))
    psi = jnp.broadcast_to(jnp.stack([jnp.cos(theta), jnp.sin(theta)])[:, :, None], (2, half, lanes))
    tab_plain = _dft_table(seq, half, shifted=False)
    nj = d // w
    small = lambda shape: pl.BlockSpec(shape, lambda j: (0,) * len(shape))
    kern = functools.partial(_filter_kernel, seq=seq)
    return pl.pallas_call(
        kern,
        grid=(nj,),
        in_specs=[
            small((1, HY_BANDS)), small((1, fwid)), small((HY_BANDS, fwid)), small((HY_BANDS, fwid)),
            small((1, fwid)), small((1, fwid)), small((fwid, fwid)), small((1, fwid)), small((1, fwid)),
            pl.BlockSpec((fwid, w), lambda j: (0, j)),
            pl.BlockSpec((fwid, w), lambda j: (0, nj + j)),
            pl.BlockSpec((1, w), lambda j: (0, j)),
            pl.BlockSpec((1, w), lambda j: (0, j)),
            small((2, half, lanes)),
            _const_spec((2 * half, seq)),
        ],
        out_specs=pl.BlockSpec((6, half, w), lambda j: (0, 0, j)),
        out_shape=jax.ShapeDtypeStruct((6, half, d), F32),
        scratch_shapes=[pltpu.VMEM((seq, fwid), F32)],
        compiler_params=_cparams("arbitrary"),
    )(bands, fw1[0:1], fw1[1:1 + HY_BANDS], fw1[1 + HY_BANDS:], fb1.reshape(1, fwid), ff1.reshape(1, fwid),
      fw2, fb2.reshape(1, fwid), ff2.reshape(1, fwid), fw3, fw3, deltas, fbias.reshape(1, d), psi, tab_plain)


def _fftconv_kernel(u_ref, x1_ref, sef_ref, tab_ref, o_ref, *, seq):
    half = seq // 2
    cos_t = tab_ref[pl.ds(0, half), :]
    sin_t = tab_ref[pl.ds(half, half), :]
    u0 = u_ref[0, 0:half, :]
    u1 = u_ref[0, half:seq, :]
    a0, b0 = _dot(cos_t, u0), _dot(sin_t, u0)
    a1, b1 = _dot(cos_t, u1), _dot(sin_t, u1)
    s_re, s_im, e_re, e_im, f_re, f_im = (sef_ref[i] for i in range(6))
    v0_re = (s_re * a0 + s_im * b0 + f_re * a1 + f_im * b1).astype(BF16)
    v0_im = (s_im * a0 - s_re * b0 + f_im * a1 - f_re * b1).astype(BF16)
    v1_re = (e_re * a0 + e_im * b0 + s_re * a1 + s_im * b1).astype(BF16)
    v1_im = (e_im * a0 - e_re * b0 + s_im * a1 - s_re * b1).astype(BF16)
    y0 = _dot(cos_t, v0_re) - _dot(sin_t, v0_im)
    y1 = _dot(cos_t, v1_re) - _dot(sin_t, v1_im)
    o_ref[0, 0:half, :] = (y0 * (1.0 / seq) * x1_ref[0, 0:half, :].astype(F32)).astype(BF16)
    o_ref[0, half:seq, :] = (y1 * (1.0 / seq) * x1_ref[0, half:seq, :].astype(F32)).astype(BF16)


def _fftconv(u, x1, sef, *, w=256):
    bsz, seq, d = u.shape
    half = seq // 2
    nj = d // w
    act = pl.BlockSpec((1, seq, w), lambda j, b: (b, 0, j))
    kern = functools.partial(_fftconv_kernel, seq=seq)
    return pl.pallas_call(
        kern,
        grid=(nj, bsz),
        in_specs=[act, act, pl.BlockSpec((6, half, w), lambda j, b: (0, 0, j)), _const_spec((2 * half, half))],
        out_specs=act,
        out_shape=jax.ShapeDtypeStruct((bsz, seq, d), BF16),
        compiler_params=_cparams("arbitrary", "arbitrary"),
    )(u, x1, sef, _dft_table(half, half, shifted=True))


def _proj_res_kernel(a_ref, x_ref, w_ref, b_ref, gain_ref, g_ref, o_ref):
    y = _dot(a_ref[...], w_ref[...]) + b_ref[...]
    o_ref[...] = x_ref[...] + g_ref[0, 0] * (_rms(y) * gain_ref[...])


def _proj_res(a, x, w, b, gain, mods, gate_chunk, mod_row_fn, *, tm):
    rows, d = x.shape
    k = a.shape[-1]
    return pl.pallas_call(
        _proj_res_kernel,
        grid=(rows // tm,),
        in_specs=[
            pl.BlockSpec((tm, k), lambda i: (i, 0)),
            pl.BlockSpec((tm, d), lambda i: (i, 0)),
            _const_spec((k, d)),
            pl.BlockSpec((1, d), lambda i: (0, 0)),
            pl.BlockSpec((1, d), lambda i: (0, 0)),
            _mod_block(d, gate_chunk, mod_row_fn),
        ],
        out_specs=pl.BlockSpec((tm, d), lambda i: (i, 0)),
        out_shape=jax.ShapeDtypeStruct((rows, d), F32),
        compiler_params=_cparams("arbitrary"),
    )(a, x, w, b.reshape(1, d), gain.reshape(1, d), mods)


def _proj_res_parity_kernel(a_ref, x_ref, w_ref, b_ref, gain_ref, g_ref, o_ref, x_slabs, o_slabs):
    th = a_ref.shape[2]
    n_slabs = x_slabs.shape[0]
    for k in range(n_slabs):
        x_slabs[k] = x_ref[0, :, k * LANES:(k + 1) * LANES]
    for parity in range(2):
        rows = pl.ds(parity, th, stride=2)
        y = _dot(a_ref[0, parity], w_ref[...]) + b_ref[...]
        x_par = jnp.concatenate([x_slabs[k, rows, :] for k in range(n_slabs)], axis=1)
        out = x_par + g_ref[0, 0] * (_rms(y) * gain_ref[...])
        for k in range(n_slabs):
            o_slabs[k, rows, :] = out[:, k * LANES:(k + 1) * LANES]
    for k in range(n_slabs):
        o_ref[0, :, k * LANES:(k + 1) * LANES] = o_slabs[k]


def _proj_res_parity(a, x, w, b, gain, mods, gate_chunk, mod_row_fn, *, th):
    bsz, seq, d = x.shape
    half = seq // 2
    th = min(th, half)
    return pl.pallas_call(
        _proj_res_parity_kernel,
        grid=(bsz, half // th),
        in_specs=[
            pl.BlockSpec((1, 2, th, d), lambda b, i: (b, 0, i, 0)),
            pl.BlockSpec((1, 2 * th, d), lambda b, i: (b, i, 0)),
            _const_spec((d, d)),
            pl.BlockSpec((1, d), lambda b, i: (0, 0)),
            pl.BlockSpec((1, d), lambda b, i: (0, 0)),
            _mod_block(d, gate_chunk, lambda b, i: mod_row_fn(b)),
        ],
        out_specs=pl.BlockSpec((1, 2 * th, d), lambda b, i: (b, i, 0)),
        out_shape=jax.ShapeDtypeStruct((bsz, seq, d), F32),
        scratch_shapes=[pltpu.VMEM((d // LANES, 2 * th, LANES), F32)] * 2,
        compiler_params=_cparams("arbitrary", "arbitrary"),
    )(a.reshape(bsz, 2, half, d), x, w, b.reshape(1, d), gain.reshape(1, d), mods)


def _mlp_kernel(x_ref, pre_ref, sc_ref, sh_ref, g_ref, w1_ref, w2_ref, post_ref, o_ref, *, ff_chunk):
    x = x_ref[...]
    h = _norm_mod(x, pre_ref[...], sc_ref[0, 0], sh_ref[0, 0]).astype(BF16)
    d_ff = w1_ref.shape[-1]
    acc = None
    for c in range(d_ff // ff_chunk):
        cols = pl.ds(c * ff_chunk, ff_chunk)
        a = jnp.maximum(_dot(h, w1_ref[:, cols]), 0.0)
        part = _dot((a * a).astype(BF16), w2_ref[cols, :])
        acc = part if acc is None else acc + part
    o_ref[...] = x + g_ref[0, 0] * (_rms(acc) * post_ref[...])


def _mlp(x, pre, post, mods, mod_row_fn, w1, w2, *, tm, ff_chunk=1024):
    rows, d = x.shape
    d_ff = w1.shape[-1]
    kern = functools.partial(_mlp_kernel, ff_chunk=ff_chunk)
    vec = pl.BlockSpec((1, d), lambda i: (0, 0))
    return pl.pallas_call(
        kern,
        grid=(rows // tm,),
        in_specs=[
            pl.BlockSpec((tm, d), lambda i: (i, 0)),
            vec,
            _mod_block(d, 4, mod_row_fn),
            _mod_block(d, 3, mod_row_fn),
            _mod_block(d, 5, mod_row_fn),
            _const_spec((d, d_ff)),
            _const_spec((d_ff, d)),
            vec,
        ],
        out_specs=pl.BlockSpec((tm, d), lambda i: (i, 0)),
        out_shape=jax.ShapeDtypeStruct((rows, d), F32),
        compiler_params=_cparams("arbitrary"),
    )(x, pre.reshape(1, d), mods, mods, mods, w1, w2, post.reshape(1, d))


def _qkv_kernel(x_ref, pre_ref, sc_ref, sh_ref, w_ref, gains_ref, cos_ref, sin_ref,
                *out_refs, n_q_pairs, rope, sub_rows):
    q_ref = out_refs[0] if n_q_pairs else None
    k_ref, v_ref = out_refs[-2:]
    hd = HEAD_DIM
    tm = x_ref.shape[0]
    mult = pre_ref[...] * (1.0 + sc_ref[0, 0])
    sh = sh_ref[0, 0]
    q_scale = ATTN_SCALE * LOG2_E
    gain_ab = [(gains_ref[0:1, :] * q_scale, gains_ref[1:2, :] * q_scale), (gains_ref[2:3, :], gains_ref[3:4, :])]
    half_r = lax.broadcasted_iota(jnp.int32, (hd, hd), 0) // (hd // 2)
    half_c = lax.broadcasted_iota(jnp.int32, (hd, hd), 1) // (hd // 2)
    same_head = (half_r == half_c).astype(BF16)

    def rows_of(i):
        return slice(i * sub_rows, (i + 1) * sub_rows)

    def project(i):
        h = (_rms(x_ref[rows_of(i), :]) * mult + sh).astype(BF16)
        return _dot(h, w_ref[...])

    def finish(i, qkv):
        rows = rows_of(i)
        if rope:
            cos_t, sin_t = cos_ref[rows, :], sin_ref[rows, :]
            tabs = [(ga * cos_t, gb * sin_t, ga * sin_t, gb * cos_t) for ga, gb in gain_ab]

        def pair(idx, kind):
            a = qkv[:, 2 * idx * hd:(2 * idx + 1) * hd]
            b = qkv[:, (2 * idx + 1) * hd:(2 * idx + 2) * hd]
            ssq = _dot((a * a + b * b).astype(BF16), same_head)
            r = lax.rsqrt(ssq * (1.0 / hd) + EPS)
            if rope:
                ca, sb, sa, cb = tabs[kind]
                return (r * (a * ca - b * sb)).astype(BF16), (r * (a * sa + b * cb)).astype(BF16)
            ga, gb = gain_ab[kind]
            return (r * (a * ga)).astype(BF16), (r * (b * gb)).astype(BF16)

        for p in range(n_q_pairs):
            qa, qb = pair(p, 0)
            q_ref[rows, 2 * p * hd:(2 * p + 1) * hd] = qa
            q_ref[rows, (2 * p + 1) * hd:(2 * p + 2) * hd] = qb
        ka, kb = pair(n_q_pairs, 1)
        k_ref[rows, 0:hd] = ka
        k_ref[rows, hd:2 * hd] = kb
        v0 = 2 * (n_q_pairs + 1) * hd
        v_ref[rows, :] = qkv[:, v0:v0 + N_KV_HEADS * hd].astype(BF16)

    n_sub = tm // sub_rows
    pending = project(0)
    for i in range(1, n_sub):
        nxt = project(i)
        finish(i - 1, pending)
        pending = nxt
    finish(n_sub - 1, pending)


def _qkv(x, pre, mods, mod_row_fn, w, gains, cos_t, sin_t, *, n_q_pairs, rope, tm, seq):
    rows, d = x.shape
    hd = HEAD_DIM
    n = w.shape[-1]
    kern = functools.partial(_qkv_kernel, n_q_pairs=n_q_pairs, rope=rope, sub_rows=min(tm, 256))
    vec = pl.BlockSpec((1, d), lambda i: (0, 0))
    tiles_per_seq = seq // tm
    pos = pl.BlockSpec((tm, hd), lambda i: (i % tiles_per_seq, 0))
    widths = ([2 * n_q_pairs * hd] if n_q_pairs else []) + [N_KV_HEADS * hd] * 2
    return pl.pallas_call(
        kern,
        grid=(rows // tm,),
        in_specs=[
            pl.BlockSpec((tm, d), lambda i: (i, 0)),
            vec,
            _mod_block(d, 1, mod_row_fn),
            _mod_block(d, 0, mod_row_fn),
            _const_spec((d, n)),
            pl.BlockSpec((4, hd), lambda i: (0, 0)),
            pos, pos,
        ],
        out_specs=[pl.BlockSpec((tm, wd), lambda i: (i, 0)) for wd in widths],
        out_shape=[jax.ShapeDtypeStruct((rows, wd), BF16) for wd in widths],
        compiler_params=_cparams("arbitrary"),
    )(x, pre.reshape(1, d), mods, mods, w, gains, cos_t, sin_t)


def _attn_kernel(q_ref, kl_ref, vl_ref, kc_ref, vc_ref, o_ref, k_all, v_ext, *, n_pairs, kv_blocks):
    hd = HEAD_DIM
    tq = q_ref.shape[1]
    seq = kl_ref.shape[1]
    total = k_all.shape[0]
    kv_head = pl.program_id(1)

    @pl.when(pl.program_id(2) == 0)
    def _():
        k_all[0:seq, :] = kl_ref[0]
        k_all[seq:total, :] = kc_ref[0]
        v_ext[0:seq, 0:hd] = vl_ref[0]
        v_ext[seq:total, 0:hd] = vc_ref[0]
        v_ext[:, hd:2 * hd] = jnp.ones((total, hd), BF16)

    lane_half = (lax.broadcasted_iota(jnp.int32, (1, 2 * hd), 1) % hd) // (hd // 2)
    keep = (lane_half == kv_head).astype(BF16)
    q = jnp.concatenate([q_ref[0, :, 2 * p * hd:(2 * p + 2) * hd] * keep for p in range(n_pairs)], axis=0)
    nt = (((1,), (1,)), ((), ()))
    m = acc = None
    scores = lambda blk: lax.dot_general(q, k_all[blk[0]:blk[1], :], nt, preferred_element_type=F32)
    s_next = scores(kv_blocks[0])
    for i, (s0, s1) in enumerate(kv_blocks):
        s = s_next
        if i + 1 < len(kv_blocks):
            s_next = scores(kv_blocks[i + 1])
        m_blk = jnp.max(s, axis=-1, keepdims=True)
        m_new = m_blk if m is None else jnp.maximum(m, m_blk)
        pv = _dot(jnp.exp2(s - m_new).astype(BF16), v_ext[s0:s1, :])
        acc = pv if m is None else acc * jnp.exp2(m - m_new) + pv
        m = m_new
    o = (acc[:, 0:hd] / acc[:, hd:2 * hd]).astype(BF16)
    for p in range(n_pairs):
        o_ref[0, :, p * hd:(p + 1) * hd] = o[p * tq:(p + 1) * tq, :]


def _kv_blocks(total, pattern):
    assert total % MXU_WIDTH == 0
    n_tiles = total // MXU_WIDTH
    blocks, start, i = [], 0, 0
    while start < n_tiles:
        size = min(pattern[i % len(pattern)], n_tiles - start)
        blocks.append((start * MXU_WIDTH, (start + size) * MXU_WIDTH))
        start += size
        i += 1
    return tuple(blocks)


def _attention(q, k_l, v_l, k_c, v_c, *, tq=256, kv_pattern=(5, 3, 1)):
    bsz, seq, dq = q.shape
    ctx_len = k_c.shape[1]
    hd = HEAD_DIM
    n_pairs = dq // (2 * hd)
    total = seq + ctx_len
    kern = functools.partial(_attn_kernel, n_pairs=n_pairs, kv_blocks=_kv_blocks(total, kv_pattern))
    return pl.pallas_call(
        kern,
        grid=(bsz, N_KV_HEADS, seq // tq),
        in_specs=[
            pl.BlockSpec((1, tq, dq), lambda b, h, i: (b, i, 0)),
            pl.BlockSpec((1, seq, 2 * hd), lambda b, h, i: (b, 0, 0)),
            pl.BlockSpec((1, seq, hd), lambda b, h, i: (b, 0, h)),
            pl.BlockSpec((1, ctx_len, 2 * hd), lambda b, h, i: (b, 0, 0)),
            pl.BlockSpec((1, ctx_len, hd), lambda b, h, i: (b, 0, h)),
        ],
        out_specs=pl.BlockSpec((1, tq, n_pairs * hd), lambda b, h, i: (b, i, h)),
        out_shape=jax.ShapeDtypeStruct((bsz, seq, dq), BF16),
        scratch_shapes=[pltpu.VMEM((total, 2 * hd), BF16), pltpu.VMEM((total, 2 * hd), BF16)],
        compiler_params=_cparams("arbitrary", "arbitrary", "arbitrary"),
    )(q, k_l, v_l, k_c, v_c)


def _rope_tables(seq):
    rows = seq // GRID_W
    pairs = HEAD_DIM // 4
    row = jnp.repeat(jnp.arange(rows, dtype=F32), GRID_W)
    col = jnp.tile(jnp.arange(GRID_W, dtype=F32), rows)
    inv = ROPE_THETA ** (-jnp.arange(pairs, dtype=F32) / pairs)
    ang = jnp.concatenate([row[:, None] * inv[None, :], col[:, None] * inv[None, :]], axis=-1)
    cos, sin = jnp.cos(ang), jnp.sin(ang)
    return jnp.concatenate([cos, cos], axis=-1), jnp.concatenate([sin, sin], axis=-1)


def _pair_columns(head_a, head_b):
    hd = HEAD_DIM
    even, odd = jnp.arange(0, hd, 2), jnp.arange(1, hd, 2)
    return jnp.concatenate([head_a * hd + even, head_b * hd + even, head_a * hd + odd, head_b * hd + odd])


def kernel(x, c, ctx, c_ctx, mod_w, mod_b, mix_norm_pre, mix_norm_post, mlp_norm_pre, mlp_norm_post, mlp_w1, mlp_w2, hy_w_in, hy_b_in, hy_conv_w, hy_conv_b, hy_filt_w1, hy_filt_b1, hy_filt_freq1, hy_filt_w2, hy_filt_b2, hy_filt_freq2, hy_filt_w3, hy_filt_bias, hy_w_out, hy_b_out, attn_w_qkv, attn_q_norm, attn_k_norm, attn_w_o):
    bsz, seq, d = x.shape
    ctx_len = ctx.shape[1]
    hd = HEAD_DIM
    n_heads = d // hd
    tm = 512
    tmc = ctx_len

    ctx_row = bsz
    n_rows = -(-(bsz + 1) // MOD_ROWS_PAD) * MOD_ROWS_PAD
    cond = jnp.concatenate([c, c_ctx[None, :], jnp.zeros((n_rows - bsz - 1, d), F32)], axis=0)
    mods_all = _modulation(cond, mod_w, mod_b).reshape(mod_w.shape[0], n_rows, 6, 1, d)

    x_row = lambda i: i // (seq // tm)
    c_row = lambda i: ctx_row

    xf = x.reshape(bsz * seq, d)
    cf = ctx.reshape(bsz * ctx_len, d)

    mods = mods_all[0]
    w_in = hy_w_in[0].astype(BF16)
    w_out = hy_w_out[0].astype(BF16)
    filt = (hy_filt_w1[0], hy_filt_b1[0], hy_filt_freq1[0], hy_filt_w2[0], hy_filt_b2[0], hy_filt_freq2[0],
            hy_filt_w3[0], hy_filt_bias[0])
    w1 = mlp_w1[0].astype(BF16)
    w2 = mlp_w2[0].astype(BF16)

    def hyena(tokens, length, mod_row_b):
        tokens = tokens.reshape(bsz, length, d)
        u, x1 = _hyena_in(tokens, mix_norm_pre[0], mods, mod_row_b, w_in, hy_b_in[0], hy_conv_w[0], hy_conv_b[0])
        gated = _fftconv(u, x1, _hyena_filter_tables(length, *filt))
        out = _proj_res_parity(gated, tokens, w_out, hy_b_out[0], mix_norm_post[0], mods, 2, mod_row_b, th=256)
        return out.reshape(bsz * length, d)

    xf = hyena(xf, seq, lambda b: b)
    cf = hyena(cf, ctx_len, lambda b: ctx_row)
    xf = _mlp(xf, mlp_norm_pre[0], mlp_norm_post[0], mods, x_row, w1, w2, tm=tm)
    cf = _mlp(cf, mlp_norm_pre[0], mlp_norm_post[0], mods, c_row, w1, w2, tm=tmc)

    mods = mods_all[1]
    w_qkv = attn_w_qkv[0]
    assert N_KV_HEADS == 2 and n_heads % 2 == 0
    group = n_heads // N_KV_HEADS
    pair_cols = [_pair_columns(p, group + p) for p in range(group)] + [_pair_columns(n_heads, n_heads + 1)]
    v_cols = jnp.arange((n_heads + N_KV_HEADS) * hd, (n_heads + 2 * N_KV_HEADS) * hd)
    w_lat = w_qkv[:, jnp.concatenate(pair_cols + [v_cols])].astype(BF16)
    w_ctx = w_lat[:, n_heads * hd:]
    even, odd = jnp.arange(0, hd, 2), jnp.arange(1, hd, 2)
    gains = jnp.stack([jnp.tile(g[idx], 2) for g in (attn_q_norm[0], attn_k_norm[0]) for idx in (even, odd)])
    cos_t, sin_t = _rope_tables(seq)

    q, k_l, v_l = _qkv(xf, mix_norm_pre[1], mods, x_row, w_lat, gains, cos_t, sin_t,
                       n_q_pairs=group, rope=True, tm=tm, seq=seq)
    k_c, v_c = _qkv(cf, mix_norm_pre[1], mods, c_row, w_ctx, gains, cos_t, sin_t,
                    n_q_pairs=0, rope=False, tm=tmc, seq=tmc)
    kv = N_KV_HEADS * hd
    o = _attention(q.reshape(bsz, seq, d), k_l.reshape(bsz, seq, kv), v_l.reshape(bsz, seq, kv),
                   k_c.reshape(bsz, ctx_len, kv), v_c.reshape(bsz, ctx_len, kv))
    xf = _proj_res(o.reshape(bsz * seq, d), xf, attn_w_o[0].astype(BF16), jnp.zeros((d,), F32),
                   mix_norm_post[1], mods, 2, x_row, tm=tm)
    xf = _mlp(xf, mlp_norm_pre[1], mlp_norm_post[1], mods, x_row, mlp_w1[1].astype(BF16),
              mlp_w2[1].astype(BF16), tm=tm)
    return xf.reshape(bsz, seq, d)
```

```python
import functools
import math

import jax
import jax.numpy as jnp
from jax import lax
from jax.experimental import pallas as pl
from jax.experimental.pallas import tpu as pltpu

F32 = jnp.float32
BF16 = jnp.bfloat16

EPS = 1e-6
GRID_W = 64
HY_BANDS = 16
HY_DECAY_TARGET = 1e-2
HY_FAST = 0.3
HY_SLOW = 1.5
HY_SHIFT = 0.0
HEAD_DIM = 128
N_KV_HEADS = 2
ROPE_THETA = 10000.0
ATTN_SCALE = HEAD_DIM ** -0.5
LOG2_E = math.log2(math.e)

MOD_ROWS_PAD = 8
SUBLANES = 8
LANES = 128
MXU_WIDTH = 256

VMEM_LIMIT = 56 * 1024 * 1024


def _cparams(*sem):
    return pltpu.CompilerParams(dimension_semantics=sem, vmem_limit_bytes=VMEM_LIMIT)


def _const_spec(shape):
    nd = len(shape)
    return pl.BlockSpec(shape, lambda *_: (0,) * nd, pipeline_mode=pl.Buffered(1))


def _rms(x):
    return x * lax.rsqrt(jnp.mean(x * x, axis=-1, keepdims=True) + EPS)


def _norm_mod(x, gain, sc, sh):
    return _rms(x) * (gain * (1.0 + sc)) + sh


def _dot(a, b):
    return jnp.dot(a, b, preferred_element_type=F32)


def _dot_hi(a, b):
    return jnp.dot(a, b, preferred_element_type=F32, precision=lax.Precision.HIGHEST)


def _mod_kernel(cond_ref, w_ref, b_ref, o_ref):
    s = cond_ref[...]
    s = s * jax.nn.sigmoid(s)
    o_ref[0] = _dot(s.astype(BF16), w_ref[0].astype(BF16)) + b_ref[0]


def _modulation(cond, mod_w, mod_b):
    depth, d, n = mod_w.shape
    rows = cond.shape[0]
    tn = d
    return pl.pallas_call(
        _mod_kernel,
        grid=(depth, n // tn),
        in_specs=[
            pl.BlockSpec((rows, d), lambda i, j: (0, 0)),
            pl.BlockSpec((1, d, tn), lambda i, j: (i, 0, j)),
            pl.BlockSpec((1, 1, tn), lambda i, j: (i, 0, j)),
        ],
        out_specs=pl.BlockSpec((1, rows, tn), lambda i, j: (i, 0, j)),
        out_shape=jax.ShapeDtypeStruct((depth, rows, n), F32),
        compiler_params=_cparams("arbitrary", "arbitrary"),
    )(cond, mod_w, mod_b.reshape(depth, 1, n))


def _mod_block(d, chunk, row_fn):
    return pl.BlockSpec((1, 1, 1, d), lambda *idx: (row_fn(*idx), chunk, 0, 0))


def _hyena_in_kernel(x_ref, gain_ref, sc_ref, sh_ref, w0_ref, w1_ref, w2_ref, b_ref, cw_ref, cb_ref,
                     u_ref, x1_ref, h_buf, x_slabs, *, seq, rc):
    j = pl.program_id(1)
    w = u_ref.shape[-1]
    half = seq // 2
    n_chunks = half // rc
    n_slabs = x_slabs.shape[0]

    @pl.when(j == 0)
    def _():
        mult = gain_ref[...] * (1.0 + sc_ref[0, 0])
        sh = sh_ref[0, 0]
        for k in range(n_slabs):
            x_slabs[k] = x_ref[0, :, k * LANES:(k + 1) * LANES]
        for parity in range(2):
            def body(c, carry):
                src = pl.ds(2 * c * rc + parity, rc, stride=2)
                dst = pl.ds(pl.multiple_of(2 * c * rc + parity * rc, rc), rc)
                xs = jnp.concatenate([x_slabs[k, src, :] for k in range(n_slabs)], axis=1)
                h_buf[dst, :] = (_rms(xs) * mult + sh).astype(BF16)
                return carry

            lax.fori_loop(0, n_chunks, body, 0)

    sub = lax.broadcasted_iota(jnp.int32, (SUBLANES, w), 0)
    w_refs = (w0_ref, w1_ref, w2_ref)
    taps = [[cw_ref[k, s:s + 1, :] for s in range(3)] for k in range(3)]
    bias = [cb_ref[s:s + 1, :] + b_ref[s:s + 1, :] * (taps[0][s] + taps[1][s] + taps[2][s]) for s in range(3)]

    def project(c):
        h = h_buf[2 * c * rc:2 * (c + 1) * rc, :]
        return [_dot(h, w_refs[s][...]) for s in range(3)]

    def conv(c, m_before, m_here, m_after):
        even_streams, odd_streams = [], []
        for s in range(3):
            m_even, m_odd = m_here[s][0:rc], m_here[s][rc:2 * rc]
            row_before = -b_ref[s:s + 1, :] if m_before is None else m_before[s][2 * rc - 1:2 * rc, :]
            row_after = -b_ref[s:s + 1, :] if m_after is None else m_after[s][0:1, :]
            odd_prev = pltpu.roll(m_odd, 1, 0)
            odd_prev = jnp.concatenate([jnp.where(sub == 0, row_before, odd_prev[0:SUBLANES]),
                                        odd_prev[SUBLANES:]], axis=0)
            even_next = pltpu.roll(m_even, rc - 1, 0)
            even_next = jnp.concatenate([even_next[:rc - SUBLANES],
                                         jnp.where(sub == SUBLANES - 1, row_after, even_next[rc - SUBLANES:])], axis=0)
            even_streams.append(odd_prev * taps[0][s] + m_even * taps[1][s] + m_odd * taps[2][s] + bias[s])
            odd_streams.append(m_even * taps[0][s] + m_odd * taps[1][s] + even_next * taps[2][s] + bias[s])
        for base, (x1, x2, v) in ((0, even_streams), (half, odd_streams)):
            u_ref[0, base + c * rc:base + (c + 1) * rc, :] = (v * x2).astype(BF16)
            x1_ref[0, base + c * rc:base + (c + 1) * rc, :] = x1.astype(BF16)

    zs = [project(0)]
    for c in range(1, n_chunks):
        zs.append(project(c))
        conv(c - 1, zs[c - 2] if c >= 2 else None, zs[c - 1], zs[c])
    conv(n_chunks - 1, zs[n_chunks - 2] if n_chunks >= 2 else None, zs[n_chunks - 1], None)


def _hyena_in(x, gain, mods, mod_row_fn, w_in, b_in, conv_w, conv_b, *, w=512, rc=256):
    bsz, seq, d = x.shape
    w = min(w, d)
    nj = d // w
    rc = min(rc, seq // 2)
    kern = functools.partial(_hyena_in_kernel, seq=seq, rc=rc)
    wspec = lambda s: pl.BlockSpec((d, w), lambda b, j: (0, s * nj + j))
    out_spec = pl.BlockSpec((1, seq, w), lambda b, j: (b, 0, j))
    return pl.pallas_call(
        kern,
        grid=(bsz, nj),
        in_specs=[
            pl.BlockSpec((1, seq, d), lambda b, j: (b, 0, 0)),
            pl.BlockSpec((1, d), lambda b, j: (0, 0)),
            _mod_block(d, 1, lambda b, j: mod_row_fn(b)),
            _mod_block(d, 0, lambda b, j: mod_row_fn(b)),
            wspec(0), wspec(1), wspec(2),
            pl.BlockSpec((3, w), lambda b, j: (0, j)),
            pl.BlockSpec((3, 3, w), lambda b, j: (0, 0, j)),
            pl.BlockSpec((3, w), lambda b, j: (0, j)),
        ],
        out_specs=[out_spec, out_spec],
        out_shape=[jax.ShapeDtypeStruct((bsz, seq, d), BF16)] * 2,
        scratch_shapes=[pltpu.VMEM((seq, d), BF16), pltpu.VMEM((d // LANES, seq, LANES), F32)],
        compiler_params=_cparams("arbitrary", "arbitrary"),
    )(x, gain.reshape(1, d), mods, mods, w_in, w_in, w_in,
      b_in.reshape(3, d), conv_w.reshape(3, 3, d), conv_b.reshape(3, d))


def _dft_table(length, n_freq, shifted, kb=32):
    if shifted:
        period = 8 * length
        nn = 2 * jnp.arange(length, dtype=jnp.int32)[None, :] + 1
    else:
        period = 4 * length
        nn = jnp.arange(length, dtype=jnp.int32)[None, :]
    kh = jnp.arange(n_freq // kb, dtype=jnp.int32)[:, None]
    kl = jnp.arange(kb, dtype=jnp.int32)[:, None]
    to_angle = lambda m: (m % period).astype(F32) * (2.0 * math.pi / period)
    alpha = to_angle(2 * kb * kh * nn)[:, None, :]
    beta = to_angle((2 * kl + 1) * nn)[None, :, :]
    ca, sa, cb, sb = jnp.cos(alpha), jnp.sin(alpha), jnp.cos(beta), jnp.sin(beta)
    cos_t = (ca * cb - sa * sb).reshape(n_freq, length)
    sin_t = (sa * cb + ca * sb).reshape(n_freq, length)
    return jnp.concatenate([cos_t, sin_t], axis=0).astype(BF16)


def _filter_kernel(bands_ref, w1t_ref, w1c_ref, w1s_ref, b1_ref, f1_ref, w2_ref, b2_ref, f2_ref,
                   w3f_ref, w3b_ref, deltas_ref, bias_ref, psi_ref, tab_ref, o_ref, h_buf, *, seq):
    w = o_ref.shape[-1]
    half = seq // 2

    @pl.when(pl.program_id(0) == 0)
    def _():
        t64 = lax.broadcasted_iota(jnp.int32, (seq, w1t_ref.shape[-1]), 0).astype(F32) / seq
        t16 = lax.broadcasted_iota(jnp.int32, (seq, HY_BANDS), 0).astype(F32) / seq
        ang = (2.0 * math.pi * t16) * bands_ref[...]
        pre = t64 * w1t_ref[...] + _dot_hi(jnp.cos(ang), w1c_ref[...]) + _dot_hi(jnp.sin(ang), w1s_ref[...])
        h = jnp.sin(f1_ref[...] * (pre + b1_ref[...]))
        h_buf[...] = jnp.sin(f2_ref[...] * (_dot_hi(h, w2_ref[...]) + b2_ref[...]))

    h = h_buf[...]
    row = lax.broadcasted_iota(jnp.int32, (seq, w), 0)
    decay = jnp.exp(-(row.astype(F32) / seq) * deltas_ref[...]) + HY_SHIFT
    h_f = _dot_hi(h, w3f_ref[...]) * decay
    h_b = _dot_hi(h, w3b_ref[...]) * decay
    h_b = jnp.where(row == 0, 0.0, h_b)
    h_sum = h_f + h_b
    h_dif = h_b - h_f
    alt = jnp.where(row % 2 == 0, 1.0, -1.0)
    cos_lo = tab_ref[pl.ds(0, half), :]
    sin_lo = tab_ref[pl.ds(half, half), :]
    bias = bias_ref[...]
    p_lo = _dot(cos_lo, h_sum.astype(BF16)) + bias
    p_hi = _dot(cos_lo, (h_sum * alt).astype(BF16)) + bias
    q_lo = _dot(sin_lo, h_dif.astype(BF16))
    q_hi = -_dot(sin_lo, (h_dif * alt).astype(BF16))
    d_re = p_lo - p_hi
    d_im = q_lo + q_hi
    psi_c = jnp.concatenate([psi_ref[0]] * (w // psi_ref.shape[-1]), axis=1)
    psi_s = jnp.concatenate([psi_ref[1]] * (w // psi_ref.shape[-1]), axis=1)
    o_ref[0] = p_lo + p_hi
    o_ref[1] = q_lo - q_hi
    o_ref[2] = d_re * psi_c - d_im * psi_s
    o_ref[3] = d_re * psi_s + d_im * psi_c
    o_ref[4] = d_re * psi_c + d_im * psi_s
    o_ref[5] = d_im * psi_c - d_re * psi_s


def _hyena_filter_tables(seq, fw1, fb1, ff1, fw2, fb2, ff2, fw3, fbias, *, w=256):
    d = fbias.shape[-1]
    fwid = fw2.shape[0]
    half = seq // 2
    lanes = LANES
    bands = jnp.linspace(1e-4, HY_BANDS - 1, HY_BANDS, dtype=F32).reshape(1, HY_BANDS)
    deltas = jnp.abs(jnp.linspace(math.log(HY_DECAY_TARGET) / HY_SLOW, math.log(HY_DECAY_TARGET) / HY_FAST,
                                  d, dtype=F32)).reshape(1, d)
    theta = (2 * jnp.arange(half, dtype=F32) + 1.0) * (math.pi / (2 * seq))
    psi = jnp.broadcast_to(jnp.stack([jnp.cos(theta), jnp.sin(theta)])[:, :, None], (2, half, lanes))
    tab_plain = _dft_table(seq, half, shifted=False)
    nj = d // w
    small = lambda shape: pl.BlockSpec(shape, lambda j: (0,) * len(shape))
    kern = functools.partial(_filter_kernel, seq=seq)
    return pl.pallas_call(
        kern,
        grid=(nj,),
        in_specs=[
            small((1, HY_BANDS)), small((1, fwid)), small((HY_BANDS, fwid)), small((HY_BANDS, fwid)),
            small((1, fwid)), small((1, fwid)), small((fwid, fwid)), small((1, fwid)), small((1, fwid)),
            pl.BlockSpec((fwid, w), lambda j: (0, j)),
            pl.BlockSpec((fwid, w), lambda j: (0, nj + j)),
            pl.BlockSpec((1, w), lambda j: (0, j)),
            pl.BlockSpec((1, w), lambda j: (0, j)),
            small((2, half, lanes)),
            _const_spec((2 * half, seq)),
        ],
        out_specs=pl.BlockSpec((6, half, w), lambda j: (0, 0, j)),
        out_shape=jax.ShapeDtypeStruct((6, half, d), F32),
        scratch_shapes=[pltpu.VMEM((seq, fwid), F32)],
        compiler_params=_cparams("arbitrary"),
    )(bands, fw1[0:1], fw1[1:1 + HY_BANDS], fw1[1 + HY_BANDS:], fb1.reshape(1, fwid), ff1.reshape(1, fwid),
      fw2, fb2.reshape(1, fwid), ff2.reshape(1, fwid), fw3, fw3, deltas, fbias.reshape(1, d), psi, tab_plain)


def _fftconv_kernel(u_ref, x1_ref, sef_ref, tab_ref, o_ref, *, seq):
    half = seq // 2
    cos_t = tab_ref[pl.ds(0, half), :]
    sin_t = tab_ref[pl.ds(half, half), :]
    u0 = u_ref[0, 0:half, :]
    u1 = u_ref[0, half:seq, :]
    a0, b0 = _dot(cos_t, u0), _dot(sin_t, u0)
    a1, b1 = _dot(cos_t, u1), _dot(sin_t, u1)
    s_re, s_im, e_re, e_im, f_re, f_im = (sef_ref[i] for i in range(6))
    v0_re = (s_re * a0 + s_im * b0 + f_re * a1 + f_im * b1).astype(BF16)
    v0_im = (s_im * a0 - s_re * b0 + f_im * a1 - f_re * b1).astype(BF16)
    v1_re = (e_re * a0 + e_im * b0 + s_re * a1 + s_im * b1).astype(BF16)
    v1_im = (e_im * a0 - e_re * b0 + s_im * a1 - s_re * b1).astype(BF16)
    y0 = _dot(cos_t, v0_re) - _dot(sin_t, v0_im)
    y1 = _dot(cos_t, v1_re) - _dot(sin_t, v1_im)
    o_ref[0, 0:half, :] = (y0 * (1.0 / seq) * x1_ref[0, 0:half, :].astype(F32)).astype(BF16)
    o_ref[0, half:seq, :] = (y1 * (1.0 / seq) * x1_ref[0, half:seq, :].astype(F32)).astype(BF16)


def _fftconv(u, x1, sef, *, block_elems=2048 * 256):
    bsz, seq, d = u.shape
    half = seq // 2
    w = min(d, max(MXU_WIDTH, block_elems // seq // MXU_WIDTH * MXU_WIDTH))
    nj = d // w
    act = pl.BlockSpec((1, seq, w), lambda j, b: (b, 0, j))
    kern = functools.partial(_fftconv_kernel, seq=seq)
    return pl.pallas_call(
        kern,
        grid=(nj, bsz),
        in_specs=[act, act, pl.BlockSpec((6, half, w), lambda j, b: (0, 0, j)), _const_spec((2 * half, half))],
        out_specs=act,
        out_shape=jax.ShapeDtypeStruct((bsz, seq, d), BF16),
        compiler_params=_cparams("arbitrary", "arbitrary"),
    )(u, x1, sef, _dft_table(half, half, shifted=True))


def _proj_res_parity_kernel(a_ref, x_ref, w_ref, b_ref, gain_ref, g_ref, o_ref, x_slabs, o_slabs):
    th = a_ref.shape[2]
    n_slabs = x_slabs.shape[0]
    for k in range(n_slabs):
        x_slabs[k] = x_ref[0, :, k * LANES:(k + 1) * LANES]
    for parity in range(2):
        rows = pl.ds(parity, th, stride=2)
        y = _dot(a_ref[0, parity], w_ref[...]) + b_ref[...]
        x_par = jnp.concatenate([x_slabs[k, rows, :] for k in range(n_slabs)], axis=1)
        out = x_par + g_ref[0, 0] * (_rms(y) * gain_ref[...])
        for k in range(n_slabs):
            o_slabs[k, rows, :] = out[:, k * LANES:(k + 1) * LANES]
    for k in range(n_slabs):
        o_ref[0, :, k * LANES:(k + 1) * LANES] = o_slabs[k]


def _proj_res_parity(a, x, w, b, gain, mods, gate_chunk, mod_row_fn, *, th):
    bsz, seq, d = x.shape
    half = seq // 2
    th = min(th, half)
    return pl.pallas_call(
        _proj_res_parity_kernel,
        grid=(bsz, half // th),
        in_specs=[
            pl.BlockSpec((1, 2, th, d), lambda b, i: (b, 0, i, 0)),
            pl.BlockSpec((1, 2 * th, d), lambda b, i: (b, i, 0)),
            _const_spec((d, d)),
            pl.BlockSpec((1, d), lambda b, i: (0, 0)),
            pl.BlockSpec((1, d), lambda b, i: (0, 0)),
            _mod_block(d, gate_chunk, lambda b, i: mod_row_fn(b)),
        ],
        out_specs=pl.BlockSpec((1, 2 * th, d), lambda b, i: (b, i, 0)),
        out_shape=jax.ShapeDtypeStruct((bsz, seq, d), F32),
        scratch_shapes=[pltpu.VMEM((d // LANES, 2 * th, LANES), F32)] * 2,
        compiler_params=_cparams("arbitrary", "arbitrary"),
    )(a.reshape(bsz, 2, half, d), x, w, b.reshape(1, d), gain.reshape(1, d), mods)


def _mlp_kernel(x_ref, pre_ref, sc_ref, sh_ref, g_ref, w1_ref, w2_ref, post_ref, o_ref, *, ff_chunk):
    x = x_ref[...]
    h = _norm_mod(x, pre_ref[...], sc_ref[0, 0], sh_ref[0, 0]).astype(BF16)
    d_ff = w1_ref.shape[-1]
    acc = None
    for c in range(d_ff // ff_chunk):
        cols = pl.ds(c * ff_chunk, ff_chunk)
        a = jnp.maximum(_dot(h, w1_ref[:, cols]), 0.0)
        part = _dot((a * a).astype(BF16), w2_ref[cols, :])
        acc = part if acc is None else acc + part
    o_ref[...] = x + g_ref[0, 0] * (_rms(acc) * post_ref[...])


def _mlp(x, pre, post, mods, mod_row_fn, w1, w2, *, tm, ff_chunk=1024):
    rows, d = x.shape
    d_ff = w1.shape[-1]
    kern = functools.partial(_mlp_kernel, ff_chunk=ff_chunk)
    vec = pl.BlockSpec((1, d), lambda i: (0, 0))
    return pl.pallas_call(
        kern,
        grid=(rows // tm,),
        in_specs=[
            pl.BlockSpec((tm, d), lambda i: (i, 0)),
            vec,
            _mod_block(d, 4, mod_row_fn),
            _mod_block(d, 3, mod_row_fn),
            _mod_block(d, 5, mod_row_fn),
            _const_spec((d, d_ff)),
            _const_spec((d_ff, d)),
            vec,
        ],
        out_specs=pl.BlockSpec((tm, d), lambda i: (i, 0)),
        out_shape=jax.ShapeDtypeStruct((rows, d), F32),
        compiler_params=_cparams("arbitrary"),
    )(x, pre.reshape(1, d), mods, mods, mods, w1, w2, post.reshape(1, d))


def _qkv_kernel(x_ref, pre_ref, sc_ref, sh_ref, w_ref, gains_ref, cos_ref, sin_ref,
                *out_refs, n_q_pairs, rope, sub_rows):
    q_ref = out_refs[0] if n_q_pairs else None
    k_ref, v_ref = out_refs[-2:]
    hd = HEAD_DIM
    tm = x_ref.shape[0]
    mult = pre_ref[...] * (1.0 + sc_ref[0, 0])
    sh = sh_ref[0, 0]
    q_scale = ATTN_SCALE * LOG2_E
    gain_ab = [(gains_ref[0:1, :] * q_scale, gains_ref[1:2, :] * q_scale), (gains_ref[2:3, :], gains_ref[3:4, :])]
    half_r = lax.broadcasted_iota(jnp.int32, (hd, hd), 0) // (hd // 2)
    half_c = lax.broadcasted_iota(jnp.int32, (hd, hd), 1) // (hd // 2)
    same_head = (half_r == half_c).astype(BF16)

    def rows_of(i):
        return slice(i * sub_rows, (i + 1) * sub_rows)

    def project(i):
        h = (_rms(x_ref[rows_of(i), :]) * mult + sh).astype(BF16)
        return _dot(h, w_ref[...])

    def finish(i, qkv):
        rows = rows_of(i)
        if rope:
            cos_t, sin_t = cos_ref[rows, :], sin_ref[rows, :]
            tabs = [(ga * cos_t, gb * sin_t, ga * sin_t, gb * cos_t) for ga, gb in gain_ab]

        def pair(idx, kind):
            a = qkv[:, 2 * idx * hd:(2 * idx + 1) * hd]
            b = qkv[:, (2 * idx + 1) * hd:(2 * idx + 2) * hd]
            ssq = _dot((a * a + b * b).astype(BF16), same_head)
            r = lax.rsqrt(ssq * (1.0 / hd) + EPS)
            if rope:
                ca, sb, sa, cb = tabs[kind]
                return (r * (a * ca - b * sb)).astype(BF16), (r * (a * sa + b * cb)).astype(BF16)
            ga, gb = gain_ab[kind]
            return (r * (a * ga)).astype(BF16), (r * (b * gb)).astype(BF16)

        for p in range(n_q_pairs):
            qa, qb = pair(p, 0)
            q_ref[rows, 2 * p * hd:(2 * p + 1) * hd] = qa
            q_ref[rows, (2 * p + 1) * hd:(2 * p + 2) * hd] = qb
        ka, kb = pair(n_q_pairs, 1)
        k_ref[rows, 0:hd] = ka
        k_ref[rows, hd:2 * hd] = kb
        v0 = 2 * (n_q_pairs + 1) * hd
        v_ref[rows, :] = qkv[:, v0:v0 + N_KV_HEADS * hd].astype(BF16)

    n_sub = tm // sub_rows
    pending = project(0)
    for i in range(1, n_sub):
        nxt = project(i)
        finish(i - 1, pending)
        pending = nxt
    finish(n_sub - 1, pending)


def _qkv(x, pre, mods, mod_row_fn, w, gains, cos_t, sin_t, *, n_q_pairs, rope, tm, seq):
    rows, d = x.shape
    hd = HEAD_DIM
    n = w.shape[-1]
    kern = functools.partial(_qkv_kernel, n_q_pairs=n_q_pairs, rope=rope, sub_rows=min(tm, 256))
    vec = pl.BlockSpec((1, d), lambda i: (0, 0))
    tiles_per_seq = seq // tm
    pos = pl.BlockSpec((tm, hd), lambda i: (i % tiles_per_seq, 0))
    widths = ([2 * n_q_pairs * hd] if n_q_pairs else []) + [N_KV_HEADS * hd] * 2
    return pl.pallas_call(
        kern,
        grid=(rows // tm,),
        in_specs=[
            pl.BlockSpec((tm, d), lambda i: (i, 0)),
            vec,
            _mod_block(d, 1, mod_row_fn),
            _mod_block(d, 0, mod_row_fn),
            _const_spec((d, n)),
            pl.BlockSpec((4, hd), lambda i: (0, 0)),
            pos, pos,
        ],
        out_specs=[pl.BlockSpec((tm, wd), lambda i: (i, 0)) for wd in widths],
        out_shape=[jax.ShapeDtypeStruct((rows, wd), BF16) for wd in widths],
        compiler_params=_cparams("arbitrary"),
    )(x, pre.reshape(1, d), mods, mods, w, gains, cos_t, sin_t)


def _attn_kernel(q_ref, kl_ref, vl_ref, kc_ref, vc_ref, x_ref, wo_ref, gain_ref, g_ref, o_ref, k_all, v_ext,
                 *, n_pairs, kv_blocks):
    hd = HEAD_DIM
    tq = q_ref.shape[1]
    seq = kl_ref.shape[1]
    total = k_all.shape[0]

    @pl.when(pl.program_id(1) == 0)
    def _():
        k_all[0:seq, :] = kl_ref[0]
        k_all[seq:total, :] = kc_ref[0]
        for h in range(N_KV_HEADS):
            v_ext[h, 0:seq, 0:hd] = vl_ref[0, :, h * hd:(h + 1) * hd]
            v_ext[h, seq:total, 0:hd] = vc_ref[0, :, h * hd:(h + 1) * hd]
            v_ext[h, :, hd:2 * hd] = jnp.ones((total, hd), BF16)

    nt = (((1,), (1,)), ((), ()))
    lane_half = (lax.broadcasted_iota(jnp.int32, (1, 2 * hd), 1) % hd) // (hd // 2)

    def start(kv_head):
        keep = (lane_half == kv_head).astype(BF16)
        q = jnp.concatenate([q_ref[0, :, 2 * p * hd:(2 * p + 2) * hd] * keep for p in range(n_pairs)], axis=0)
        st = dict(h=kv_head, q=q, m=None, acc=None)
        st["s_next"] = lax.dot_general(q, k_all[kv_blocks[0][0]:kv_blocks[0][1], :], nt, preferred_element_type=F32)
        return st

    def step(st, i):
        s0, s1 = kv_blocks[i]
        s = st["s_next"]
        if i + 1 < len(kv_blocks):
            n0, n1 = kv_blocks[i + 1]
            st["s_next"] = lax.dot_general(st["q"], k_all[n0:n1, :], nt, preferred_element_type=F32)
        m, acc = st["m"], st["acc"]
        m_blk = jnp.max(s, axis=-1, keepdims=True)
        m_new = m_blk if m is None else jnp.maximum(m, m_blk)
        pv = _dot(jnp.exp2(s - m_new).astype(BF16), v_ext[st["h"], s0:s1, :])
        st["acc"] = pv if m is None else acc * jnp.exp2(m - m_new) + pv
        st["m"] = m_new

    def heads_of(st):
        acc = st["acc"]
        o = (acc[:, 0:hd] / acc[:, hd:2 * hd]).astype(BF16)
        return [o[p * tq:(p + 1) * tq, :] for p in range(n_pairs)]

    last = len(kv_blocks) - 1
    first, second = start(0), None
    for i in range(last):
        step(first, i)
    second = start(1)
    step(first, last)
    for i in range(last + 1):
        step(second, i)
    attn = jnp.concatenate(heads_of(first) + heads_of(second), axis=1)
    y = _dot(attn, wo_ref[...])
    o_ref[0] = x_ref[0] + g_ref[0, 0] * (_rms(y) * gain_ref[...])


def _kv_blocks(total, pattern):
    assert total % MXU_WIDTH == 0
    n_tiles = total // MXU_WIDTH
    blocks, start, i = [], 0, 0
    while start < n_tiles:
        size = min(pattern[i % len(pattern)], n_tiles - start)
        blocks.append((start * MXU_WIDTH, (start + size) * MXU_WIDTH))
        start += size
        i += 1
    return tuple(blocks)


def _attention(q, k_l, v_l, k_c, v_c, x, w_o, gain, mods, gate_chunk, mod_row_fn, *, tq=256, kv_pattern=(5, 3, 1)):
    bsz, seq, dq = q.shape
    d = x.shape[-1]
    ctx_len = k_c.shape[1]
    hd = HEAD_DIM
    kv = N_KV_HEADS * hd
    n_pairs = dq // (2 * hd)
    total = seq + ctx_len
    kern = functools.partial(_attn_kernel, n_pairs=n_pairs, kv_blocks=_kv_blocks(total, kv_pattern))
    whole = lambda rows, width: pl.BlockSpec((1, rows, width), lambda b, i: (b, 0, 0))
    tile = lambda width: pl.BlockSpec((1, tq, width), lambda b, i: (b, i, 0))
    return pl.pallas_call(
        kern,
        grid=(bsz, seq // tq),
        in_specs=[
            tile(dq), whole(seq, 2 * hd), whole(seq, kv), whole(ctx_len, 2 * hd), whole(ctx_len, kv),
            tile(d),
            _const_spec((dq, d)),
            pl.BlockSpec((1, d), lambda b, i: (0, 0)),
            _mod_block(d, gate_chunk, lambda b, i: mod_row_fn(b)),
        ],
        out_specs=tile(d),
        out_shape=jax.ShapeDtypeStruct((bsz, seq, d), F32),
        scratch_shapes=[pltpu.VMEM((total, 2 * hd), BF16), pltpu.VMEM((N_KV_HEADS, total, 2 * hd), BF16)],
        compiler_params=_cparams("arbitrary", "arbitrary"),
    )(q, k_l, v_l, k_c, v_c, x, w_o, gain.reshape(1, d), mods)


def _rope_tables(seq):
    rows = seq // GRID_W
    pairs = HEAD_DIM // 4
    row = jnp.repeat(jnp.arange(rows, dtype=F32), GRID_W)
    col = jnp.tile(jnp.arange(GRID_W, dtype=F32), rows)
    inv = ROPE_THETA ** (-jnp.arange(pairs, dtype=F32) / pairs)
    ang = jnp.concatenate([row[:, None] * inv[None, :], col[:, None] * inv[None, :]], axis=-1)
    cos, sin = jnp.cos(ang), jnp.sin(ang)
    return jnp.concatenate([cos, cos], axis=-1), jnp.concatenate([sin, sin], axis=-1)


def _pair_columns(head_a, head_b):
    hd = HEAD_DIM
    even, odd = jnp.arange(0, hd, 2), jnp.arange(1, hd, 2)
    return jnp.concatenate([head_a * hd + even, head_b * hd + even, head_a * hd + odd, head_b * hd + odd])


def kernel(x, c, ctx, c_ctx, mod_w, mod_b, mix_norm_pre, mix_norm_post, mlp_norm_pre, mlp_norm_post, mlp_w1, mlp_w2, hy_w_in, hy_b_in, hy_conv_w, hy_conv_b, hy_filt_w1, hy_filt_b1, hy_filt_freq1, hy_filt_w2, hy_filt_b2, hy_filt_freq2, hy_filt_w3, hy_filt_bias, hy_w_out, hy_b_out, attn_w_qkv, attn_q_norm, attn_k_norm, attn_w_o):
    bsz, seq, d = x.shape
    ctx_len = ctx.shape[1]
    hd = HEAD_DIM
    n_heads = d // hd
    tm = 512
    tmc = ctx_len

    ctx_row = bsz
    n_rows = -(-(bsz + 1) // MOD_ROWS_PAD) * MOD_ROWS_PAD
    cond = jnp.concatenate([c, c_ctx[None, :], jnp.zeros((n_rows - bsz - 1, d), F32)], axis=0)
    mods_all = _modulation(cond, mod_w, mod_b).reshape(mod_w.shape[0], n_rows, 6, 1, d)

    x_row = lambda i: i // (seq // tm)
    c_row = lambda i: ctx_row

    xf = x.reshape(bsz * seq, d)
    cf = ctx.reshape(bsz * ctx_len, d)

    mods = mods_all[0]
    w_in = hy_w_in[0].astype(BF16)
    w_out = hy_w_out[0].astype(BF16)
    filt = (hy_filt_w1[0], hy_filt_b1[0], hy_filt_freq1[0], hy_filt_w2[0], hy_filt_b2[0], hy_filt_freq2[0],
            hy_filt_w3[0], hy_filt_bias[0])
    w1 = mlp_w1[0].astype(BF16)
    w2 = mlp_w2[0].astype(BF16)

    def hyena(tokens, length, mod_row_b):
        tokens = tokens.reshape(bsz, length, d)
        u, x1 = _hyena_in(tokens, mix_norm_pre[0], mods, mod_row_b, w_in, hy_b_in[0], hy_conv_w[0], hy_conv_b[0])
        gated = _fftconv(u, x1, _hyena_filter_tables(length, *filt))
        out = _proj_res_parity(gated, tokens, w_out, hy_b_out[0], mix_norm_post[0], mods, 2, mod_row_b, th=256)
        return out.reshape(bsz * length, d)

    xf = hyena(xf, seq, lambda b: b)
    cf = hyena(cf, ctx_len, lambda b: ctx_row)
    xf = _mlp(xf, mlp_norm_pre[0], mlp_norm_post[0], mods, x_row, w1, w2, tm=tm)
    cf = _mlp(cf, mlp_norm_pre[0], mlp_norm_post[0], mods, c_row, w1, w2, tm=tmc)

    mods = mods_all[1]
    w_qkv = attn_w_qkv[0]
    assert N_KV_HEADS == 2 and n_heads % 2 == 0
    group = n_heads // N_KV_HEADS
    pair_cols = [_pair_columns(p, group + p) for p in range(group)] + [_pair_columns(n_heads, n_heads + 1)]
    v_cols = jnp.arange((n_heads + N_KV_HEADS) * hd, (n_heads + 2 * N_KV_HEADS) * hd)
    w_lat = w_qkv[:, jnp.concatenate(pair_cols + [v_cols])].astype(BF16)
    w_ctx = w_lat[:, n_heads * hd:]
    even, odd = jnp.arange(0, hd, 2), jnp.arange(1, hd, 2)
    gains = jnp.stack([jnp.tile(g[idx], 2) for g in (attn_q_norm[0], attn_k_norm[0]) for idx in (even, odd)])
    cos_t, sin_t = _rope_tables(seq)

    q, k_l, v_l = _qkv(xf, mix_norm_pre[1], mods, x_row, w_lat, gains, cos_t, sin_t,
                       n_q_pairs=group, rope=True, tm=tm, seq=seq)
    k_c, v_c = _qkv(cf, mix_norm_pre[1], mods, c_row, w_ctx, gains, cos_t, sin_t,
                    n_q_pairs=0, rope=False, tm=tmc, seq=tmc)
    kv = N_KV_HEADS * hd
    xf = _attention(q.reshape(bsz, seq, d), k_l.reshape(bsz, seq, kv), v_l.reshape(bsz, seq, kv),
                    k_c.reshape(bsz, ctx_len, kv), v_c.reshape(bsz, ctx_len, kv), xf.reshape(bsz, seq, d),
                    attn_w_o[0].astype(BF16), mix_norm_post[1], mods, 2, lambda b: b).reshape(bsz * seq, d)
    xf = _mlp(xf, mlp_norm_pre[1], mlp_norm_post[1], mods, x_row, mlp_w1[1].astype(BF16),
              mlp_w2[1].astype(BF16), tm=tm)
    return xf.reshape(bsz, seq, d)
```

```python
import functools
import math

import jax
import jax.numpy as jnp
from jax import lax
from jax.experimental import pallas as pl
from jax.experimental.pallas import tpu as pltpu

F32 = jnp.float32
BF16 = jnp.bfloat16

EPS = 1e-6
GRID_W = 64
HY_BANDS = 16
HY_DECAY_TARGET = 1e-2
HY_FAST = 0.3
HY_SLOW = 1.5
HY_SHIFT = 0.0
HEAD_DIM = 128
N_KV_HEADS = 2
ROPE_THETA = 10000.0
ATTN_SCALE = HEAD_DIM ** -0.5
LOG2_E = math.log2(math.e)

MOD_ROWS_PAD = 8
SUBLANES = 8
LANES = 128
MXU_WIDTH = 256

VMEM_LIMIT = 56 * 1024 * 1024


def _cparams(*sem):
    return pltpu.CompilerParams(dimension_semantics=sem, vmem_limit_bytes=VMEM_LIMIT)


def _const_spec(shape):
    nd = len(shape)
    return pl.BlockSpec(shape, lambda *_: (0,) * nd, pipeline_mode=pl.Buffered(1))


def _rms(x):
    return x * lax.rsqrt(jnp.mean(x * x, axis=-1, keepdims=True) + EPS)


def _norm_mod(x, gain, sc, sh):
    return _rms(x) * (gain * (1.0 + sc)) + sh


def _dot(a, b):
    return jnp.dot(a, b, preferred_element_type=F32)


def _dot_hi(a, b):
    return jnp.dot(a, b, preferred_element_type=F32, precision=lax.Precision.HIGHEST)


def _mod_kernel(cond_ref, w_ref, b_ref, o_ref):
    s = cond_ref[...]
    s = s * jax.nn.sigmoid(s)
    o_ref[0] = _dot(s.astype(BF16), w_ref[0].astype(BF16)) + b_ref[0]


def _modulation(cond, mod_w, mod_b):
    depth, d, n = mod_w.shape
    rows = cond.shape[0]
    tn = d
    return pl.pallas_call(
        _mod_kernel,
        grid=(depth, n // tn),
        in_specs=[
            pl.BlockSpec((rows, d), lambda i, j: (0, 0)),
            pl.BlockSpec((1, d, tn), lambda i, j: (i, 0, j)),
            pl.BlockSpec((1, 1, tn), lambda i, j: (i, 0, j)),
        ],
        out_specs=pl.BlockSpec((1, rows, tn), lambda i, j: (i, 0, j)),
        out_shape=jax.ShapeDtypeStruct((depth, rows, n), F32),
        compiler_params=_cparams("arbitrary", "arbitrary"),
    )(cond, mod_w, mod_b.reshape(depth, 1, n))


def _mod_block(d, chunk, row_fn):
    return pl.BlockSpec((1, 1, 1, d), lambda *idx: (row_fn(*idx), chunk, 0, 0))


def _hyena_in_kernel(x_ref, gain_ref, sc_ref, sh_ref, w0_ref, w1_ref, w2_ref, b_ref, cw_ref, cb_ref,
                     u_ref, x1_ref, h_buf, x_slabs, *, seq, rc):
    j = pl.program_id(1)
    w = u_ref.shape[-1]
    half = seq // 2
    n_chunks = half // rc
    n_slabs = x_slabs.shape[0]

    @pl.when(j == 0)
    def _():
        mult = gain_ref[...] * (1.0 + sc_ref[0, 0])
        sh = sh_ref[0, 0]
        for k in range(n_slabs):
            x_slabs[k] = x_ref[0, :, k * LANES:(k + 1) * LANES]
        for parity in range(2):
            def body(c, carry):
                src = pl.ds(2 * c * rc + parity, rc, stride=2)
                dst = pl.ds(pl.multiple_of(2 * c * rc + parity * rc, rc), rc)
                xs = jnp.concatenate([x_slabs[k, src, :] for k in range(n_slabs)], axis=1)
                h_buf[dst, :] = (_rms(xs) * mult + sh).astype(BF16)
                return carry

            lax.fori_loop(0, n_chunks, body, 0)

    sub = lax.broadcasted_iota(jnp.int32, (SUBLANES, w), 0)
    w_refs = (w0_ref, w1_ref, w2_ref)
    taps = [[cw_ref[k, s:s + 1, :] for s in range(3)] for k in range(3)]
    bias = [cb_ref[s:s + 1, :] + b_ref[s:s + 1, :] * (taps[0][s] + taps[1][s] + taps[2][s]) for s in range(3)]

    def project(c):
        h = h_buf[2 * c * rc:2 * (c + 1) * rc, :]
        return [_dot(h, w_refs[s][...]) for s in range(3)]

    def conv(c, m_before, m_here, m_after):
        even_streams, odd_streams = [], []
        for s in range(3):
            m_even, m_odd = m_here[s][0:rc], m_here[s][rc:2 * rc]
            row_before = -b_ref[s:s + 1, :] if m_before is None else m_before[s][2 * rc - 1:2 * rc, :]
            row_after = -b_ref[s:s + 1, :] if m_after is None else m_after[s][0:1, :]
            odd_prev = pltpu.roll(m_odd, 1, 0)
            odd_prev = jnp.concatenate([jnp.where(sub == 0, row_before, odd_prev[0:SUBLANES]),
                                        odd_prev[SUBLANES:]], axis=0)
            even_next = pltpu.roll(m_even, rc - 1, 0)
            even_next = jnp.concatenate([even_next[:rc - SUBLANES],
                                         jnp.where(sub == SUBLANES - 1, row_after, even_next[rc - SUBLANES:])], axis=0)
            even_streams.append(odd_prev * taps[0][s] + m_even * taps[1][s] + m_odd * taps[2][s] + bias[s])
            odd_streams.append(m_even * taps[0][s] + m_odd * taps[1][s] + even_next * taps[2][s] + bias[s])
        for base, (x1, x2, v) in ((0, even_streams), (half, odd_streams)):
            u_ref[0, base + c * rc:base + (c + 1) * rc, :] = (v * x2).astype(BF16)
            x1_ref[0, base + c * rc:base + (c + 1) * rc, :] = x1.astype(BF16)

    zs = [project(0)]
    for c in range(1, n_chunks):
        zs.append(project(c))
        conv(c - 1, zs[c - 2] if c >= 2 else None, zs[c - 1], zs[c])
    conv(n_chunks - 1, zs[n_chunks - 2] if n_chunks >= 2 else None, zs[n_chunks - 1], None)


def _hyena_in(x, gain, mods, mod_row_fn, w_in, b_in, conv_w, conv_b, *, block_elems=2048 * 512, rc=256):
    bsz, seq, d = x.shape
    w = min(d, max(MXU_WIDTH, block_elems // seq // MXU_WIDTH * MXU_WIDTH))
    nj = d // w
    rc = min(rc, seq // 2)
    kern = functools.partial(_hyena_in_kernel, seq=seq, rc=rc)
    wspec = lambda s: pl.BlockSpec((d, w), lambda b, j: (0, s * nj + j))
    out_spec = pl.BlockSpec((1, seq, w), lambda b, j: (b, 0, j))
    return pl.pallas_call(
        kern,
        grid=(bsz, nj),
        in_specs=[
            pl.BlockSpec((1, seq, d), lambda b, j: (b, 0, 0)),
            pl.BlockSpec((1, d), lambda b, j: (0, 0)),
            _mod_block(d, 1, lambda b, j: mod_row_fn(b)),
            _mod_block(d, 0, lambda b, j: mod_row_fn(b)),
            wspec(0), wspec(1), wspec(2),
            pl.BlockSpec((3, w), lambda b, j: (0, j)),
            pl.BlockSpec((3, 3, w), lambda b, j: (0, 0, j)),
            pl.BlockSpec((3, w), lambda b, j: (0, j)),
        ],
        out_specs=[out_spec, out_spec],
        out_shape=[jax.ShapeDtypeStruct((bsz, seq, d), BF16)] * 2,
        scratch_shapes=[pltpu.VMEM((seq, d), BF16), pltpu.VMEM((d // LANES, seq, LANES), F32)],
        compiler_params=_cparams("arbitrary", "arbitrary"),
    )(x, gain.reshape(1, d), mods, mods, w_in, w_in, w_in,
      b_in.reshape(3, d), conv_w.reshape(3, 3, d), conv_b.reshape(3, d))


def _dft_table(length, n_freq, shifted, kb=32):
    if shifted:
        period = 8 * length
        nn = 2 * jnp.arange(length, dtype=jnp.int32)[None, :] + 1
    else:
        period = 4 * length
        nn = jnp.arange(length, dtype=jnp.int32)[None, :]
    kh = jnp.arange(n_freq // kb, dtype=jnp.int32)[:, None]
    kl = jnp.arange(kb, dtype=jnp.int32)[:, None]
    to_angle = lambda m: (m % period).astype(F32) * (2.0 * math.pi / period)
    alpha = to_angle(2 * kb * kh * nn)[:, None, :]
    beta = to_angle((2 * kl + 1) * nn)[None, :, :]
    ca, sa, cb, sb = jnp.cos(alpha), jnp.sin(alpha), jnp.cos(beta), jnp.sin(beta)
    cos_t = (ca * cb - sa * sb).reshape(n_freq, length)
    sin_t = (sa * cb + ca * sb).reshape(n_freq, length)
    return jnp.concatenate([cos_t, sin_t], axis=0).astype(BF16)


def _filter_kernel(bands_ref, w1t_ref, w1c_ref, w1s_ref, b1_ref, f1_ref, w2_ref, b2_ref, f2_ref,
                   w3f_ref, w3b_ref, deltas_ref, bias_ref, psi_ref, tab_ref, o_ref, h_buf, *, seq):
    w = o_ref.shape[-1]
    half = seq // 2

    @pl.when(pl.program_id(0) == 0)
    def _():
        t64 = lax.broadcasted_iota(jnp.int32, (seq, w1t_ref.shape[-1]), 0).astype(F32) / seq
        t16 = lax.broadcasted_iota(jnp.int32, (seq, HY_BANDS), 0).astype(F32) / seq
        ang = (2.0 * math.pi * t16) * bands_ref[...]
        pre = t64 * w1t_ref[...] + _dot_hi(jnp.cos(ang), w1c_ref[...]) + _dot_hi(jnp.sin(ang), w1s_ref[...])
        h = jnp.sin(f1_ref[...] * (pre + b1_ref[...]))
        h_buf[...] = jnp.sin(f2_ref[...] * (_dot_hi(h, w2_ref[...]) + b2_ref[...]))

    h = h_buf[...]
    row = lax.broadcasted_iota(jnp.int32, (seq, w), 0)
    decay = jnp.exp(-(row.astype(F32) / seq) * deltas_ref[...]) + HY_SHIFT
    h_f = _dot_hi(h, w3f_ref[...]) * decay
    h_b = _dot_hi(h, w3b_ref[...]) * decay
    h_b = jnp.where(row == 0, 0.0, h_b)
    h_sum = h_f + h_b
    h_dif = h_b - h_f
    alt = jnp.where(row % 2 == 0, 1.0, -1.0)
    cos_lo = tab_ref[pl.ds(0, half), :]
    sin_lo = tab_ref[pl.ds(half, half), :]
    bias = bias_ref[...]
    p_lo = _dot(cos_lo, h_sum.astype(BF16)) + bias
    p_hi = _dot(cos_lo, (h_sum * alt).astype(BF16)) + bias
    q_lo = _dot(sin_lo, h_dif.astype(BF16))
    q_hi = -_dot(sin_lo, (h_dif * alt).astype(BF16))
    d_re = p_lo - p_hi
    d_im = q_lo + q_hi
    psi_c = jnp.concatenate([psi_ref[0]] * (w // psi_ref.shape[-1]), axis=1)
    psi_s = jnp.concatenate([psi_ref[1]] * (w // psi_ref.shape[-1]), axis=1)
    o_ref[0] = p_lo + p_hi
    o_ref[1] = q_lo - q_hi
    o_ref[2] = d_re * psi_c - d_im * psi_s
    o_ref[3] = d_re * psi_s + d_im * psi_c
    o_ref[4] = d_re * psi_c + d_im * psi_s
    o_ref[5] = d_im * psi_c - d_re * psi_s


def _hyena_filter_tables(seq, fw1, fb1, ff1, fw2, fb2, ff2, fw3, fbias, *, w=256):
    d = fbias.shape[-1]
    fwid = fw2.shape[0]
    half = seq // 2
    lanes = LANES
    bands = jnp.linspace(1e-4, HY_BANDS - 1, HY_BANDS, dtype=F32).reshape(1, HY_BANDS)
    deltas = jnp.abs(jnp.linspace(math.log(HY_DECAY_TARGET) / HY_SLOW, math.log(HY_DECAY_TARGET) / HY_FAST,
                                  d, dtype=F32)).reshape(1, d)
    theta = (2 * jnp.arange(half, dtype=F32) + 1.0) * (math.pi / (2 * seq))
    psi = jnp.broadcast_to(jnp.stack([jnp.cos(theta), jnp.sin(theta)])[:, :, None], (2, half, lanes))
    tab_plain = _dft_table(seq, half, shifted=False)
    nj = d // w
    small = lambda shape: pl.BlockSpec(shape, lambda j: (0,) * len(shape))
    kern = functools.partial(_filter_kernel, seq=seq)
    return pl.pallas_call(
        kern,
        grid=(nj,),
        in_specs=[
            small((1, HY_BANDS)), small((1, fwid)), small((HY_BANDS, fwid)), small((HY_BANDS, fwid)),
            small((1, fwid)), small((1, fwid)), small((fwid, fwid)), small((1, fwid)), small((1, fwid)),
            pl.BlockSpec((fwid, w), lambda j: (0, j)),
            pl.BlockSpec((fwid, w), lambda j: (0, nj + j)),
            pl.BlockSpec((1, w), lambda j: (0, j)),
            pl.BlockSpec((1, w), lambda j: (0, j)),
            small((2, half, lanes)),
            _const_spec((2 * half, seq)),
        ],
        out_specs=pl.BlockSpec((6, half, w), lambda j: (0, 0, j)),
        out_shape=jax.ShapeDtypeStruct((6, half, d), F32),
        scratch_shapes=[pltpu.VMEM((seq, fwid), F32)],
        compiler_params=_cparams("arbitrary"),
    )(bands, fw1[0:1], fw1[1:1 + HY_BANDS], fw1[1 + HY_BANDS:], fb1.reshape(1, fwid), ff1.reshape(1, fwid),
      fw2, fb2.reshape(1, fwid), ff2.reshape(1, fwid), fw3, fw3, deltas, fbias.reshape(1, d), psi, tab_plain)


def _fftconv_kernel(u_ref, x1_ref, sef_ref, tab_ref, o_ref, *, seq):
    half = seq // 2
    cos_t = tab_ref[pl.ds(0, half), :]
    sin_t = tab_ref[pl.ds(half, half), :]
    u0 = u_ref[0, 0:half, :]
    u1 = u_ref[0, half:seq, :]
    a0, b0 = _dot(cos_t, u0), _dot(sin_t, u0)
    a1, b1 = _dot(cos_t, u1), _dot(sin_t, u1)
    s_re, s_im, e_re, e_im, f_re, f_im = (sef_ref[i] for i in range(6))
    v0_re = (s_re * a0 + s_im * b0 + f_re * a1 + f_im * b1).astype(BF16)
    v0_im = (s_im * a0 - s_re * b0 + f_im * a1 - f_re * b1).astype(BF16)
    v1_re = (e_re * a0 + e_im * b0 + s_re * a1 + s_im * b1).astype(BF16)
    v1_im = (e_im * a0 - e_re * b0 + s_im * a1 - s_re * b1).astype(BF16)
    y0 = _dot(cos_t, v0_re) - _dot(sin_t, v0_im)
    y1 = _dot(cos_t, v1_re) - _dot(sin_t, v1_im)
    o_ref[0, 0:half, :] = (y0 * (1.0 / seq) * x1_ref[0, 0:half, :].astype(F32)).astype(BF16)
    o_ref[0, half:seq, :] = (y1 * (1.0 / seq) * x1_ref[0, half:seq, :].astype(F32)).astype(BF16)


def _fftconv(u, x1, sef, *, block_elems=2048 * 256):
    bsz, seq, d = u.shape
    half = seq // 2
    w = min(d, max(MXU_WIDTH, block_elems // seq // MXU_WIDTH * MXU_WIDTH))
    nj = d // w
    act = pl.BlockSpec((1, seq, w), lambda j, b: (b, 0, j))
    kern = functools.partial(_fftconv_kernel, seq=seq)
    return pl.pallas_call(
        kern,
        grid=(nj, bsz),
        in_specs=[act, act, pl.BlockSpec((6, half, w), lambda j, b: (0, 0, j)), _const_spec((2 * half, half))],
        out_specs=act,
        out_shape=jax.ShapeDtypeStruct((bsz, seq, d), BF16),
        compiler_params=_cparams("arbitrary", "arbitrary"),
    )(u, x1, sef, _dft_table(half, half, shifted=True))


def _proj_res_parity_kernel(a_ref, x_ref, w_ref, b_ref, gain_ref, g_ref, o_ref, x_slabs, o_slabs):
    th = a_ref.shape[2]
    n_slabs = x_slabs.shape[0]
    for k in range(n_slabs):
        x_slabs[k] = x_ref[0, :, k * LANES:(k + 1) * LANES]
    for parity in range(2):
        rows = pl.ds(parity, th, stride=2)
        y = _dot(a_ref[0, parity], w_ref[...]) + b_ref[...]
        x_par = jnp.concatenate([x_slabs[k, rows, :] for k in range(n_slabs)], axis=1)
        out = x_par + g_ref[0, 0] * (_rms(y) * gain_ref[...])
        for k in range(n_slabs):
            o_slabs[k, rows, :] = out[:, k * LANES:(k + 1) * LANES]
    for k in range(n_slabs):
        o_ref[0, :, k * LANES:(k + 1) * LANES] = o_slabs[k]


def _proj_res_parity(a, x, w, b, gain, mods, gate_chunk, mod_row_fn, *, th):
    bsz, seq, d = x.shape
    half = seq // 2
    th = min(th, half)
    return pl.pallas_call(
        _proj_res_parity_kernel,
        grid=(bsz, half // th),
        in_specs=[
            pl.BlockSpec((1, 2, th, d), lambda b, i: (b, 0, i, 0)),
            pl.BlockSpec((1, 2 * th, d), lambda b, i: (b, i, 0)),
            _const_spec((d, d)),
            pl.BlockSpec((1, d), lambda b, i: (0, 0)),
            pl.BlockSpec((1, d), lambda b, i: (0, 0)),
            _mod_block(d, gate_chunk, lambda b, i: mod_row_fn(b)),
        ],
        out_specs=pl.BlockSpec((1, 2 * th, d), lambda b, i: (b, i, 0)),
        out_shape=jax.ShapeDtypeStruct((bsz, seq, d), F32),
        scratch_shapes=[pltpu.VMEM((d // LANES, 2 * th, LANES), F32)] * 2,
        compiler_params=_cparams("arbitrary", "arbitrary"),
    )(a.reshape(bsz, 2, half, d), x, w, b.reshape(1, d), gain.reshape(1, d), mods)


def _mlp_kernel(x_ref, pre_ref, sc_ref, sh_ref, g_ref, w1_ref, w2_ref, post_ref, o_ref, *, ff_chunk, sub_rows):
    d_ff = w1_ref.shape[-1]
    mult = pre_ref[...] * (1.0 + sc_ref[0, 0])
    sh = sh_ref[0, 0]

    def rows_of(i):
        return slice(i * sub_rows, (i + 1) * sub_rows)

    def hidden(i):
        return (_rms(x_ref[rows_of(i), :]) * mult + sh).astype(BF16)

    def mix(h):
        acc = None
        for c in range(d_ff // ff_chunk):
            cols = pl.ds(c * ff_chunk, ff_chunk)
            a = jnp.maximum(_dot(h, w1_ref[:, cols]), 0.0)
            part = _dot((a * a).astype(BF16), w2_ref[cols, :])
            acc = part if acc is None else acc + part
        return acc

    def finish(i, acc):
        o_ref[rows_of(i), :] = x_ref[rows_of(i), :] + g_ref[0, 0] * (_rms(acc) * post_ref[...])

    n_sub = x_ref.shape[0] // sub_rows
    pending = mix(hidden(0))
    for i in range(1, n_sub):
        nxt = mix(hidden(i))
        finish(i - 1, pending)
        pending = nxt
    finish(n_sub - 1, pending)


def _mlp(x, pre, post, mods, mod_row_fn, w1, w2, *, tm, ff_chunk=1024, sub_rows=512):
    rows, d = x.shape
    d_ff = w1.shape[-1]
    kern = functools.partial(_mlp_kernel, ff_chunk=ff_chunk, sub_rows=min(sub_rows, tm))
    vec = pl.BlockSpec((1, d), lambda i: (0, 0))
    return pl.pallas_call(
        kern,
        grid=(rows // tm,),
        in_specs=[
            pl.BlockSpec((tm, d), lambda i: (i, 0)),
            vec,
            _mod_block(d, 4, mod_row_fn),
            _mod_block(d, 3, mod_row_fn),
            _mod_block(d, 5, mod_row_fn),
            _const_spec((d, d_ff)),
            _const_spec((d_ff, d)),
            vec,
        ],
        out_specs=pl.BlockSpec((tm, d), lambda i: (i, 0)),
        out_shape=jax.ShapeDtypeStruct((rows, d), F32),
        compiler_params=_cparams("arbitrary"),
    )(x, pre.reshape(1, d), mods, mods, mods, w1, w2, post.reshape(1, d))


def _qkv_kernel(x_ref, pre_ref, sc_ref, sh_ref, w_ref, gains_ref, cos_ref, sin_ref,
                *out_refs, n_q_pairs, rope, sub_rows):
    q_ref = out_refs[0] if n_q_pairs else None
    k_ref, v_ref = out_refs[-2:]
    hd = HEAD_DIM
    tm = x_ref.shape[0]
    mult = pre_ref[...] * (1.0 + sc_ref[0, 0])
    sh = sh_ref[0, 0]
    q_scale = ATTN_SCALE * LOG2_E
    gain_ab = [(gains_ref[0:1, :] * q_scale, gains_ref[1:2, :] * q_scale), (gains_ref[2:3, :], gains_ref[3:4, :])]
    half_r = lax.broadcasted_iota(jnp.int32, (hd, hd), 0) // (hd // 2)
    half_c = lax.broadcasted_iota(jnp.int32, (hd, hd), 1) // (hd // 2)
    same_head = (half_r == half_c).astype(BF16)

    def rows_of(i):
        return slice(i * sub_rows, (i + 1) * sub_rows)

    def project(i):
        h = (_rms(x_ref[rows_of(i), :]) * mult + sh).astype(BF16)
        return _dot(h, w_ref[...])

    def finish(i, qkv):
        rows = rows_of(i)
        if rope:
            cos_t, sin_t = cos_ref[rows, :], sin_ref[rows, :]
            tabs = [(ga * cos_t, gb * sin_t, ga * sin_t, gb * cos_t) for ga, gb in gain_ab]

        def pair(idx, kind):
            a = qkv[:, 2 * idx * hd:(2 * idx + 1) * hd]
            b = qkv[:, (2 * idx + 1) * hd:(2 * idx + 2) * hd]
            ssq = _dot((a * a + b * b).astype(BF16), same_head)
            r = lax.rsqrt(ssq * (1.0 / hd) + EPS)
            if rope:
                ca, sb, sa, cb = tabs[kind]
                return (r * (a * ca - b * sb)).astype(BF16), (r * (a * sa + b * cb)).astype(BF16)
            ga, gb = gain_ab[kind]
            return (r * (a * ga)).astype(BF16), (r * (b * gb)).astype(BF16)

        for p in range(n_q_pairs):
            qa, qb = pair(p, 0)
            q_ref[rows, 2 * p * hd:(2 * p + 1) * hd] = qa
            q_ref[rows, (2 * p + 1) * hd:(2 * p + 2) * hd] = qb
        ka, kb = pair(n_q_pairs, 1)
        k_ref[rows, 0:hd] = ka
        k_ref[rows, hd:2 * hd] = kb
        v0 = 2 * (n_q_pairs + 1) * hd
        v_ref[rows, :] = qkv[:, v0:v0 + N_KV_HEADS * hd].astype(BF16)

    n_sub = tm // sub_rows
    pending = project(0)
    for i in range(1, n_sub):
        nxt = project(i)
        finish(i - 1, pending)
        pending = nxt
    finish(n_sub - 1, pending)


def _qkv(x, pre, mods, mod_row_fn, w, gains, cos_t, sin_t, *, n_q_pairs, rope, tm, seq):
    rows, d = x.shape
    hd = HEAD_DIM
    n = w.shape[-1]
    kern = functools.partial(_qkv_kernel, n_q_pairs=n_q_pairs, rope=rope, sub_rows=min(tm, 256))
    vec = pl.BlockSpec((1, d), lambda i: (0, 0))
    tiles_per_seq = seq // tm
    pos = pl.BlockSpec((tm, hd), lambda i: (i % tiles_per_seq, 0))
    widths = ([2 * n_q_pairs * hd] if n_q_pairs else []) + [N_KV_HEADS * hd] * 2
    return pl.pallas_call(
        kern,
        grid=(rows // tm,),
        in_specs=[
            pl.BlockSpec((tm, d), lambda i: (i, 0)),
            vec,
            _mod_block(d, 1, mod_row_fn),
            _mod_block(d, 0, mod_row_fn),
            _const_spec((d, n)),
            pl.BlockSpec((4, hd), lambda i: (0, 0)),
            pos, pos,
        ],
        out_specs=[pl.BlockSpec((tm, wd), lambda i: (i, 0)) for wd in widths],
        out_shape=[jax.ShapeDtypeStruct((rows, wd), BF16) for wd in widths],
        compiler_params=_cparams("arbitrary"),
    )(x, pre.reshape(1, d), mods, mods, w, gains, cos_t, sin_t)


def _attn_kernel(q_ref, kl_ref, vl_ref, kc_ref, vc_ref, x_ref, wo_ref, gain_ref, g_ref, o_ref, k_all, v_ext,
                 *, n_pairs, kv_blocks):
    hd = HEAD_DIM
    tq = q_ref.shape[1]
    seq = kl_ref.shape[1]
    total = k_all.shape[0]

    @pl.when(pl.program_id(1) == 0)
    def _():
        k_all[0:seq, :] = kl_ref[0]
        k_all[seq:total, :] = kc_ref[0]
        for h in range(N_KV_HEADS):
            v_ext[h, 0:seq, 0:hd] = vl_ref[0, :, h * hd:(h + 1) * hd]
            v_ext[h, seq:total, 0:hd] = vc_ref[0, :, h * hd:(h + 1) * hd]
            v_ext[h, :, hd:2 * hd] = jnp.ones((total, hd), BF16)

    nt = (((1,), (1,)), ((), ()))
    lane_half = (lax.broadcasted_iota(jnp.int32, (1, 2 * hd), 1) % hd) // (hd // 2)

    def start(kv_head):
        keep = (lane_half == kv_head).astype(BF16)
        q = jnp.concatenate([q_ref[0, :, 2 * p * hd:(2 * p + 2) * hd] * keep for p in range(n_pairs)], axis=0)
        st = dict(h=kv_head, q=q, m=None, acc=None)
        st["s_next"] = lax.dot_general(q, k_all[kv_blocks[0][0]:kv_blocks[0][1], :], nt, preferred_element_type=F32)
        return st

    def step(st, i):
        s0, s1 = kv_blocks[i]
        s = st["s_next"]
        if i + 1 < len(kv_blocks):
            n0, n1 = kv_blocks[i + 1]
            st["s_next"] = lax.dot_general(st["q"], k_all[n0:n1, :], nt, preferred_element_type=F32)
        m, acc = st["m"], st["acc"]
        m_blk = jnp.max(s, axis=-1, keepdims=True)
        m_new = m_blk if m is None else jnp.maximum(m, m_blk)
        pv = _dot(jnp.exp2(s - m_new).astype(BF16), v_ext[st["h"], s0:s1, :])
        st["acc"] = pv if m is None else acc * jnp.exp2(m - m_new) + pv
        st["m"] = m_new

    def heads_of(st):
        acc = st["acc"]
        o = (acc[:, 0:hd] / acc[:, hd:2 * hd]).astype(BF16)
        return [o[p * tq:(p + 1) * tq, :] for p in range(n_pairs)]

    last = len(kv_blocks) - 1
    first, second = start(0), None
    for i in range(last):
        step(first, i)
    second = start(1)
    step(first, last)
    for i in range(last + 1):
        step(second, i)
    attn = jnp.concatenate(heads_of(first) + heads_of(second), axis=1)
    y = _dot(attn, wo_ref[...])
    o_ref[0] = x_ref[0] + g_ref[0, 0] * (_rms(y) * gain_ref[...])


def _kv_blocks(total, pattern):
    assert total % MXU_WIDTH == 0
    n_tiles = total // MXU_WIDTH
    blocks, start, i = [], 0, 0
    while start < n_tiles:
        size = min(pattern[i % len(pattern)], n_tiles - start)
        blocks.append((start * MXU_WIDTH, (start + size) * MXU_WIDTH))
        start += size
        i += 1
    return tuple(blocks)


def _attention(q, k_l, v_l, k_c, v_c, x, w_o, gain, mods, gate_chunk, mod_row_fn, *, tq=256, kv_pattern=(5, 3, 1)):
    bsz, seq, dq = q.shape
    d = x.shape[-1]
    ctx_len = k_c.shape[1]
    hd = HEAD_DIM
    kv = N_KV_HEADS * hd
    n_pairs = dq // (2 * hd)
    total = seq + ctx_len
    kern = functools.partial(_attn_kernel, n_pairs=n_pairs, kv_blocks=_kv_blocks(total, kv_pattern))
    whole = lambda rows, width: pl.BlockSpec((1, rows, width), lambda b, i: (b, 0, 0))
    tile = lambda width: pl.BlockSpec((1, tq, width), lambda b, i: (b, i, 0))
    return pl.pallas_call(
        kern,
        grid=(bsz, seq // tq),
        in_specs=[
            tile(dq), whole(seq, 2 * hd), whole(seq, kv), whole(ctx_len, 2 * hd), whole(ctx_len, kv),
            tile(d),
            _const_spec((dq, d)),
            pl.BlockSpec((1, d), lambda b, i: (0, 0)),
            _mod_block(d, gate_chunk, lambda b, i: mod_row_fn(b)),
        ],
        out_specs=tile(d),
        out_shape=jax.ShapeDtypeStruct((bsz, seq, d), F32),
        scratch_shapes=[pltpu.VMEM((total, 2 * hd), BF16), pltpu.VMEM((N_KV_HEADS, total, 2 * hd), BF16)],
        compiler_params=_cparams("arbitrary", "arbitrary"),
    )(q, k_l, v_l, k_c, v_c, x, w_o, gain.reshape(1, d), mods)


def _rope_tables(seq):
    rows = seq // GRID_W
    pairs = HEAD_DIM // 4
    row = jnp.repeat(jnp.arange(rows, dtype=F32), GRID_W)
    col = jnp.tile(jnp.arange(GRID_W, dtype=F32), rows)
    inv = ROPE_THETA ** (-jnp.arange(pairs, dtype=F32) / pairs)
    ang = jnp.concatenate([row[:, None] * inv[None, :], col[:, None] * inv[None, :]], axis=-1)
    cos, sin = jnp.cos(ang), jnp.sin(ang)
    return jnp.concatenate([cos, cos], axis=-1), jnp.concatenate([sin, sin], axis=-1)


def _pair_columns(head_a, head_b):
    hd = HEAD_DIM
    even, odd = jnp.arange(0, hd, 2), jnp.arange(1, hd, 2)
    return jnp.concatenate([head_a * hd + even, head_b * hd + even, head_a * hd + odd, head_b * hd + odd])


def kernel(x, c, ctx, c_ctx, mod_w, mod_b, mix_norm_pre, mix_norm_post, mlp_norm_pre, mlp_norm_post, mlp_w1, mlp_w2, hy_w_in, hy_b_in, hy_conv_w, hy_conv_b, hy_filt_w1, hy_filt_b1, hy_filt_freq1, hy_filt_w2, hy_filt_b2, hy_filt_freq2, hy_filt_w3, hy_filt_bias, hy_w_out, hy_b_out, attn_w_qkv, attn_q_norm, attn_k_norm, attn_w_o):
    bsz, seq, d = x.shape
    ctx_len = ctx.shape[1]
    hd = HEAD_DIM
    n_heads = d // hd
    tm = 512
    tmc = ctx_len

    ctx_row = bsz
    n_rows = -(-(bsz + 1) // MOD_ROWS_PAD) * MOD_ROWS_PAD
    cond = jnp.concatenate([c, c_ctx[None, :], jnp.zeros((n_rows - bsz - 1, d), F32)], axis=0)
    mods_all = _modulation(cond, mod_w, mod_b).reshape(mod_w.shape[0], n_rows, 6, 1, d)

    x_row = lambda i: i // (seq // tm)
    tm_mlp = 2 * tm
    x_row_mlp = lambda i: i // (seq // tm_mlp)
    c_row = lambda i: ctx_row

    xf = x.reshape(bsz * seq, d)
    cf = ctx.reshape(bsz * ctx_len, d)

    mods = mods_all[0]
    w_in = hy_w_in[0].astype(BF16)
    w_out = hy_w_out[0].astype(BF16)
    filt = (hy_filt_w1[0], hy_filt_b1[0], hy_filt_freq1[0], hy_filt_w2[0], hy_filt_b2[0], hy_filt_freq2[0],
            hy_filt_w3[0], hy_filt_bias[0])
    w1 = mlp_w1[0].astype(BF16)
    w2 = mlp_w2[0].astype(BF16)

    def hyena(tokens, length, mod_row_b):
        tokens = tokens.reshape(bsz, length, d)
        u, x1 = _hyena_in(tokens, mix_norm_pre[0], mods, mod_row_b, w_in, hy_b_in[0], hy_conv_w[0], hy_conv_b[0])
        gated = _fftconv(u, x1, _hyena_filter_tables(length, *filt))
        out = _proj_res_parity(gated, tokens, w_out, hy_b_out[0], mix_norm_post[0], mods, 2, mod_row_b, th=512)
        return out.reshape(bsz * length, d)

    xf = hyena(xf, seq, lambda b: b)
    cf = hyena(cf, ctx_len, lambda b: ctx_row)
    xf = _mlp(xf, mlp_norm_pre[0], mlp_norm_post[0], mods, x_row_mlp, w1, w2, tm=tm_mlp)
    cf = _mlp(cf, mlp_norm_pre[0], mlp_norm_post[0], mods, c_row, w1, w2, tm=min(tm_mlp, bsz * ctx_len))

    mods = mods_all[1]
    w_qkv = attn_w_qkv[0]
    assert N_KV_HEADS == 2 and n_heads % 2 == 0
    group = n_heads // N_KV_HEADS
    pair_cols = [_pair_columns(p, group + p) for p in range(group)] + [_pair_columns(n_heads, n_heads + 1)]
    v_cols = jnp.arange((n_heads + N_KV_HEADS) * hd, (n_heads + 2 * N_KV_HEADS) * hd)
    w_lat = w_qkv[:, jnp.concatenate(pair_cols + [v_cols])].astype(BF16)
    w_ctx = w_lat[:, n_heads * hd:]
    even, odd = jnp.arange(0, hd, 2), jnp.arange(1, hd, 2)
    gains = jnp.stack([jnp.tile(g[idx], 2) for g in (attn_q_norm[0], attn_k_norm[0]) for idx in (even, odd)])
    cos_t, sin_t = _rope_tables(seq)

    q, k_l, v_l = _qkv(xf, mix_norm_pre[1], mods, x_row, w_lat, gains, cos_t, sin_t,
                       n_q_pairs=group, rope=True, tm=tm, seq=seq)
    k_c, v_c = _qkv(cf, mix_norm_pre[1], mods, c_row, w_ctx, gains, cos_t, sin_t,
                    n_q_pairs=0, rope=False, tm=tmc, seq=tmc)
    kv = N_KV_HEADS * hd
    xf = _attention(q.reshape(bsz, seq, d), k_l.reshape(bsz, seq, kv), v_l.reshape(bsz, seq, kv),
                    k_c.reshape(bsz, ctx_len, kv), v_c.reshape(bsz, ctx_len, kv), xf.reshape(bsz, seq, d),
                    attn_w_o[0].astype(BF16), mix_norm_post[1], mods, 2, lambda b: b).reshape(bsz * seq, d)
    xf = _mlp(xf, mlp_norm_pre[1], mlp_norm_post[1], mods, x_row_mlp, mlp_w1[1].astype(BF16),
              mlp_w2[1].astype(BF16), tm=tm_mlp)
    return xf.reshape(bsz, seq, d)
```

```python
import functools
import math

import jax
import jax.numpy as jnp
from jax import lax
from jax.experimental import pallas as pl
from jax.experimental.pallas import tpu as pltpu

F32 = jnp.float32
BF16 = jnp.bfloat16

EPS = 1e-6
GRID_W = 64
HY_BANDS = 16
HY_DECAY_TARGET = 1e-2
HY_FAST = 0.3
HY_SLOW = 1.5
HY_SHIFT = 0.0
HEAD_DIM = 128
N_KV_HEADS = 2
ROPE_THETA = 10000.0
ATTN_SCALE = HEAD_DIM ** -0.5
LOG2_E = math.log2(math.e)

MOD_ROWS_PAD = 8
SUBLANES = 8
LANES = 128
MXU_WIDTH = 256

VMEM_LIMIT = 56 * 1024 * 1024


def _cparams(*sem):
    return pltpu.CompilerParams(dimension_semantics=sem, vmem_limit_bytes=VMEM_LIMIT)


def _const_spec(shape):
    nd = len(shape)
    return pl.BlockSpec(shape, lambda *_: (0,) * nd, pipeline_mode=pl.Buffered(1))


def _rms(x):
    return x * lax.rsqrt(jnp.mean(x * x, axis=-1, keepdims=True) + EPS)


def _norm_mod(x, gain, sc, sh):
    return _rms(x) * (gain * (1.0 + sc)) + sh


def _dot(a, b):
    return jnp.dot(a, b, preferred_element_type=F32)


def _dot_hi(a, b):
    return jnp.dot(a, b, preferred_element_type=F32, precision=lax.Precision.HIGHEST)


def _mod_kernel(cond_ref, w_ref, b_ref, o_ref):
    s = cond_ref[...]
    s = s * jax.nn.sigmoid(s)
    o_ref[0] = _dot(s.astype(BF16), w_ref[0].astype(BF16)) + b_ref[0]


def _modulation(cond, mod_w, mod_b):
    depth, d, n = mod_w.shape
    rows = cond.shape[0]
    tn = d
    return pl.pallas_call(
        _mod_kernel,
        grid=(depth, n // tn),
        in_specs=[
            pl.BlockSpec((rows, d), lambda i, j: (0, 0)),
            pl.BlockSpec((1, d, tn), lambda i, j: (i, 0, j)),
            pl.BlockSpec((1, 1, tn), lambda i, j: (i, 0, j)),
        ],
        out_specs=pl.BlockSpec((1, rows, tn), lambda i, j: (i, 0, j)),
        out_shape=jax.ShapeDtypeStruct((depth, rows, n), F32),
        compiler_params=_cparams("arbitrary", "arbitrary"),
    )(cond, mod_w, mod_b.reshape(depth, 1, n))


def _mod_block(d, chunk, row_fn):
    return pl.BlockSpec((1, 1, 1, d), lambda *idx: (row_fn(*idx), chunk, 0, 0))


def _hyena_in_kernel(x_ref, gain_ref, sc_ref, sh_ref, w0_ref, w1_ref, w2_ref, b_ref, cw_ref, cb_ref,
                     u_ref, x1_ref, h_buf, x_slabs, *, seq, rc):
    j = pl.program_id(1)
    w = u_ref.shape[-1]
    half = seq // 2
    n_chunks = half // rc
    n_slabs = x_slabs.shape[0]

    @pl.when(j == 0)
    def _():
        mult = gain_ref[...] * (1.0 + sc_ref[0, 0])
        sh = sh_ref[0, 0]
        for k in range(n_slabs):
            x_slabs[k] = x_ref[0, :, k * LANES:(k + 1) * LANES]
        for parity in range(2):
            def body(c, carry):
                src = pl.ds(2 * c * rc + parity, rc, stride=2)
                dst = pl.ds(pl.multiple_of(2 * c * rc + parity * rc, rc), rc)
                xs = jnp.concatenate([x_slabs[k, src, :] for k in range(n_slabs)], axis=1)
                h_buf[dst, :] = (_rms(xs) * mult + sh).astype(BF16)
                return carry

            lax.fori_loop(0, n_chunks, body, 0)

    sub = lax.broadcasted_iota(jnp.int32, (SUBLANES, w), 0)
    w_refs = (w0_ref, w1_ref, w2_ref)
    taps = [[cw_ref[k, s:s + 1, :] for s in range(3)] for k in range(3)]
    bias = [cb_ref[s:s + 1, :] + b_ref[s:s + 1, :] * (taps[0][s] + taps[1][s] + taps[2][s]) for s in range(3)]

    def project(c):
        h = h_buf[2 * c * rc:2 * (c + 1) * rc, :]
        return [_dot(h, w_refs[s][...]) for s in range(3)]

    def conv(c, m_before, m_here, m_after):
        even_streams, odd_streams = [], []
        for s in range(3):
            m_even, m_odd = m_here[s][0:rc], m_here[s][rc:2 * rc]
            row_before = -b_ref[s:s + 1, :] if m_before is None else m_before[s][2 * rc - 1:2 * rc, :]
            row_after = -b_ref[s:s + 1, :] if m_after is None else m_after[s][0:1, :]
            odd_prev = pltpu.roll(m_odd, 1, 0)
            odd_prev = jnp.concatenate([jnp.where(sub == 0, row_before, odd_prev[0:SUBLANES]),
                                        odd_prev[SUBLANES:]], axis=0)
            even_next = pltpu.roll(m_even, rc - 1, 0)
            even_next = jnp.concatenate([even_next[:rc - SUBLANES],
                                         jnp.where(sub == SUBLANES - 1, row_after, even_next[rc - SUBLANES:])], axis=0)
            even_streams.append(odd_prev * taps[0][s] + m_even * taps[1][s] + m_odd * taps[2][s] + bias[s])
            odd_streams.append(m_even * taps[0][s] + m_odd * taps[1][s] + even_next * taps[2][s] + bias[s])
        for base, (x1, x2, v) in ((0, even_streams), (half, odd_streams)):
            u_ref[0, base + c * rc:base + (c + 1) * rc, :] = (v * x2).astype(BF16)
            x1_ref[0, base + c * rc:base + (c + 1) * rc, :] = x1.astype(BF16)

    zs = [project(0)]
    for c in range(1, n_chunks):
        zs.append(project(c))
        conv(c - 1, zs[c - 2] if c >= 2 else None, zs[c - 1], zs[c])
    conv(n_chunks - 1, zs[n_chunks - 2] if n_chunks >= 2 else None, zs[n_chunks - 1], None)


def _hyena_in(x, gain, mods, mod_row_fn, w_in, b_in, conv_w, conv_b, *, block_elems=2048 * 512, rc=128):
    bsz, seq, d = x.shape
    w = min(d, max(MXU_WIDTH, block_elems // seq // MXU_WIDTH * MXU_WIDTH))
    nj = d // w
    rc = min(rc, seq // 2)
    kern = functools.partial(_hyena_in_kernel, seq=seq, rc=rc)
    wspec = lambda s: pl.BlockSpec((d, w), lambda b, j: (0, s * nj + j))
    out_spec = pl.BlockSpec((1, seq, w), lambda b, j: (b, 0, j))
    return pl.pallas_call(
        kern,
        grid=(bsz, nj),
        in_specs=[
            pl.BlockSpec((1, seq, d), lambda b, j: (b, 0, 0)),
            pl.BlockSpec((1, d), lambda b, j: (0, 0)),
            _mod_block(d, 1, lambda b, j: mod_row_fn(b)),
            _mod_block(d, 0, lambda b, j: mod_row_fn(b)),
            wspec(0), wspec(1), wspec(2),
            pl.BlockSpec((3, w), lambda b, j: (0, j)),
            pl.BlockSpec((3, 3, w), lambda b, j: (0, 0, j)),
            pl.BlockSpec((3, w), lambda b, j: (0, j)),
        ],
        out_specs=[out_spec, out_spec],
        out_shape=[jax.ShapeDtypeStruct((bsz, seq, d), BF16)] * 2,
        scratch_shapes=[pltpu.VMEM((seq, d), BF16), pltpu.VMEM((d // LANES, seq, LANES), F32)],
        compiler_params=_cparams("arbitrary", "arbitrary"),
    )(x, gain.reshape(1, d), mods, mods, w_in, w_in, w_in,
      b_in.reshape(3, d), conv_w.reshape(3, 3, d), conv_b.reshape(3, d))


def _dft_table(length, n_freq, shifted, kb=32):
    if shifted:
        period = 8 * length
        nn = 2 * jnp.arange(length, dtype=jnp.int32)[None, :] + 1
    else:
        period = 4 * length
        nn = jnp.arange(length, dtype=jnp.int32)[None, :]
    kh = jnp.arange(n_freq // kb, dtype=jnp.int32)[:, None]
    kl = jnp.arange(kb, dtype=jnp.int32)[:, None]
    to_angle = lambda m: (m % period).astype(F32) * (2.0 * math.pi / period)
    alpha = to_angle(2 * kb * kh * nn)[:, None, :]
    beta = to_angle((2 * kl + 1) * nn)[None, :, :]
    ca, sa, cb, sb = jnp.cos(alpha), jnp.sin(alpha), jnp.cos(beta), jnp.sin(beta)
    cos_t = (ca * cb - sa * sb).reshape(n_freq, length)
    sin_t = (sa * cb + ca * sb).reshape(n_freq, length)
    return jnp.concatenate([cos_t, sin_t], axis=0).astype(BF16)


def _filter_kernel(bands_ref, w1t_ref, w1c_ref, w1s_ref, b1_ref, f1_ref, w2_ref, b2_ref, f2_ref,
                   w3f_ref, w3b_ref, deltas_ref, bias_ref, psi_ref, tab_ref, o_ref, h_buf, *, seq):
    w = o_ref.shape[-1]
    half = seq // 2

    @pl.when(pl.program_id(0) == 0)
    def _():
        t64 = lax.broadcasted_iota(jnp.int32, (seq, w1t_ref.shape[-1]), 0).astype(F32) / seq
        t16 = lax.broadcasted_iota(jnp.int32, (seq, HY_BANDS), 0).astype(F32) / seq
        ang = (2.0 * math.pi * t16) * bands_ref[...]
        pre = t64 * w1t_ref[...] + _dot_hi(jnp.cos(ang), w1c_ref[...]) + _dot_hi(jnp.sin(ang), w1s_ref[...])
        h = jnp.sin(f1_ref[...] * (pre + b1_ref[...]))
        h_buf[...] = jnp.sin(f2_ref[...] * (_dot_hi(h, w2_ref[...]) + b2_ref[...]))

    h = h_buf[...]
    row = lax.broadcasted_iota(jnp.int32, (seq, w), 0)
    decay = jnp.exp(-(row.astype(F32) / seq) * deltas_ref[...]) + HY_SHIFT
    h_f = _dot_hi(h, w3f_ref[...]) * decay
    h_b = _dot_hi(h, w3b_ref[...]) * decay
    h_b = jnp.where(row == 0, 0.0, h_b)
    h_sum = h_f + h_b
    h_dif = h_b - h_f
    alt = jnp.where(row % 2 == 0, 1.0, -1.0)
    cos_lo = tab_ref[pl.ds(0, half), :]
    sin_lo = tab_ref[pl.ds(half, half), :]
    bias = bias_ref[...]
    p_lo = _dot(cos_lo, h_sum.astype(BF16)) + bias
    p_hi = _dot(cos_lo, (h_sum * alt).astype(BF16)) + bias
    q_lo = _dot(sin_lo, h_dif.astype(BF16))
    q_hi = -_dot(sin_lo, (h_dif * alt).astype(BF16))
    d_re = p_lo - p_hi
    d_im = q_lo + q_hi
    psi_c = jnp.concatenate([psi_ref[0]] * (w // psi_ref.shape[-1]), axis=1)
    psi_s = jnp.concatenate([psi_ref[1]] * (w // psi_ref.shape[-1]), axis=1)
    o_ref[0] = p_lo + p_hi
    o_ref[1] = q_lo - q_hi
    o_ref[2] = d_re * psi_c - d_im * psi_s
    o_ref[3] = d_re * psi_s + d_im * psi_c
    o_ref[4] = d_re * psi_c + d_im * psi_s
    o_ref[5] = d_im * psi_c - d_re * psi_s


def _hyena_filter_tables(seq, fw1, fb1, ff1, fw2, fb2, ff2, fw3, fbias, *, w=256):
    d = fbias.shape[-1]
    fwid = fw2.shape[0]
    half = seq // 2
    lanes = LANES
    bands = jnp.linspace(1e-4, HY_BANDS - 1, HY_BANDS, dtype=F32).reshape(1, HY_BANDS)
    deltas = jnp.abs(jnp.linspace(math.log(HY_DECAY_TARGET) / HY_SLOW, math.log(HY_DECAY_TARGET) / HY_FAST,
                                  d, dtype=F32)).reshape(1, d)
    theta = (2 * jnp.arange(half, dtype=F32) + 1.0) * (math.pi / (2 * seq))
    psi = jnp.broadcast_to(jnp.stack([jnp.cos(theta), jnp.sin(theta)])[:, :, None], (2, half, lanes))
    tab_plain = _dft_table(seq, half, shifted=False)
    nj = d // w
    small = lambda shape: pl.BlockSpec(shape, lambda j: (0,) * len(shape))
    kern = functools.partial(_filter_kernel, seq=seq)
    return pl.pallas_call(
        kern,
        grid=(nj,),
        in_specs=[
            small((1, HY_BANDS)), small((1, fwid)), small((HY_BANDS, fwid)), small((HY_BANDS, fwid)),
            small((1, fwid)), small((1, fwid)), small((fwid, fwid)), small((1, fwid)), small((1, fwid)),
            pl.BlockSpec((fwid, w), lambda j: (0, j)),
            pl.BlockSpec((fwid, w), lambda j: (0, nj + j)),
            pl.BlockSpec((1, w), lambda j: (0, j)),
            pl.BlockSpec((1, w), lambda j: (0, j)),
            small((2, half, lanes)),
            _const_spec((2 * half, seq)),
        ],
        out_specs=pl.BlockSpec((6, half, w), lambda j: (0, 0, j)),
        out_shape=jax.ShapeDtypeStruct((6, half, d), F32),
        scratch_shapes=[pltpu.VMEM((seq, fwid), F32)],
        compiler_params=_cparams("arbitrary"),
    )(bands, fw1[0:1], fw1[1:1 + HY_BANDS], fw1[1 + HY_BANDS:], fb1.reshape(1, fwid), ff1.reshape(1, fwid),
      fw2, fb2.reshape(1, fwid), ff2.reshape(1, fwid), fw3, fw3, deltas, fbias.reshape(1, d), psi, tab_plain)


def _fftconv_kernel(u_ref, x1_ref, sef_ref, tab_ref, o_ref, *, seq):
    half = seq // 2
    cos_t = tab_ref[pl.ds(0, half), :]
    sin_t = tab_ref[pl.ds(half, half), :]
    u0 = u_ref[0, 0:half, :]
    u1 = u_ref[0, half:seq, :]
    a0, b0 = _dot(cos_t, u0), _dot(sin_t, u0)
    a1, b1 = _dot(cos_t, u1), _dot(sin_t, u1)
    s_re, s_im, e_re, e_im, f_re, f_im = (sef_ref[i] for i in range(6))
    v0_re = (s_re * a0 + s_im * b0 + f_re * a1 + f_im * b1).astype(BF16)
    v0_im = (s_im * a0 - s_re * b0 + f_im * a1 - f_re * b1).astype(BF16)
    v1_re = (e_re * a0 + e_im * b0 + s_re * a1 + s_im * b1).astype(BF16)
    v1_im = (e_im * a0 - e_re * b0 + s_im * a1 - s_re * b1).astype(BF16)
    y0 = _dot(cos_t, v0_re) - _dot(sin_t, v0_im)
    y1 = _dot(cos_t, v1_re) - _dot(sin_t, v1_im)
    o_ref[0, 0:half, :] = (y0 * (1.0 / seq) * x1_ref[0, 0:half, :].astype(F32)).astype(BF16)
    o_ref[0, half:seq, :] = (y1 * (1.0 / seq) * x1_ref[0, half:seq, :].astype(F32)).astype(BF16)


def _fftconv(u, x1, sef, *, block_elems=2048 * 256):
    bsz, seq, d = u.shape
    half = seq // 2
    w = min(d, max(MXU_WIDTH, block_elems // seq // MXU_WIDTH * MXU_WIDTH))
    nj = d // w
    act = pl.BlockSpec((1, seq, w), lambda j, b: (b, 0, j))
    kern = functools.partial(_fftconv_kernel, seq=seq)
    return pl.pallas_call(
        kern,
        grid=(nj, bsz),
        in_specs=[act, act, pl.BlockSpec((6, half, w), lambda j, b: (0, 0, j)), _const_spec((2 * half, half))],
        out_specs=act,
        out_shape=jax.ShapeDtypeStruct((bsz, seq, d), BF16),
        compiler_params=_cparams("arbitrary", "arbitrary"),
    )(u, x1, sef, _dft_table(half, half, shifted=True))


def _proj_res_parity_kernel(a_ref, x_ref, w_ref, b_ref, gain_ref, g_ref, o_ref, x_slabs, o_slabs):
    th = a_ref.shape[2]
    n_slabs = x_slabs.shape[0]
    for k in range(n_slabs):
        x_slabs[k] = x_ref[0, :, k * LANES:(k + 1) * LANES]
    for parity in range(2):
        rows = pl.ds(parity, th, stride=2)
        y = _dot(a_ref[0, parity], w_ref[...]) + b_ref[...]
        x_par = jnp.concatenate([x_slabs[k, rows, :] for k in range(n_slabs)], axis=1)
        out = x_par + g_ref[0, 0] * (_rms(y) * gain_ref[...])
        for k in range(n_slabs):
            o_slabs[k, rows, :] = out[:, k * LANES:(k + 1) * LANES]
    for k in range(n_slabs):
        o_ref[0, :, k * LANES:(k + 1) * LANES] = o_slabs[k]


def _proj_res_parity(a, x, w, b, gain, mods, gate_chunk, mod_row_fn, *, th):
    bsz, seq, d = x.shape
    half = seq // 2
    th = min(th, half)
    return pl.pallas_call(
        _proj_res_parity_kernel,
        grid=(bsz, half // th),
        in_specs=[
            pl.BlockSpec((1, 2, th, d), lambda b, i: (b, 0, i, 0)),
            pl.BlockSpec((1, 2 * th, d), lambda b, i: (b, i, 0)),
            _const_spec((d, d)),
            pl.BlockSpec((1, d), lambda b, i: (0, 0)),
            pl.BlockSpec((1, d), lambda b, i: (0, 0)),
            _mod_block(d, gate_chunk, lambda b, i: mod_row_fn(b)),
        ],
        out_specs=pl.BlockSpec((1, 2 * th, d), lambda b, i: (b, i, 0)),
        out_shape=jax.ShapeDtypeStruct((bsz, seq, d), F32),
        scratch_shapes=[pltpu.VMEM((d // LANES, 2 * th, LANES), F32)] * 2,
        compiler_params=_cparams("arbitrary", "arbitrary"),
    )(a.reshape(bsz, 2, half, d), x, w, b.reshape(1, d), gain.reshape(1, d), mods)


def _mlp_kernel(x_ref, pre_ref, sc_ref, sh_ref, g_ref, w1_ref, w2_ref, post_ref, o_ref, *, ff_chunk, sub_rows):
    d_ff = w1_ref.shape[-1]
    mult = pre_ref[...] * (1.0 + sc_ref[0, 0])
    sh = sh_ref[0, 0]

    def rows_of(i):
        return slice(i * sub_rows, (i + 1) * sub_rows)

    def hidden(i):
        return (_rms(x_ref[rows_of(i), :]) * mult + sh).astype(BF16)

    def mix(h):
        acc = None
        for c in range(d_ff // ff_chunk):
            cols = pl.ds(c * ff_chunk, ff_chunk)
            a = jnp.maximum(_dot(h, w1_ref[:, cols]), 0.0)
            part = _dot((a * a).astype(BF16), w2_ref[cols, :])
            acc = part if acc is None else acc + part
        return acc

    def finish(i, acc):
        o_ref[rows_of(i), :] = x_ref[rows_of(i), :] + g_ref[0, 0] * (_rms(acc) * post_ref[...])

    n_sub = x_ref.shape[0] // sub_rows
    pending = mix(hidden(0))
    for i in range(1, n_sub):
        nxt = mix(hidden(i))
        finish(i - 1, pending)
        pending = nxt
    finish(n_sub - 1, pending)


def _mlp(x, pre, post, mods, mod_row_fn, w1, w2, *, tm, ff_chunk=1024, sub_rows=512):
    rows, d = x.shape
    d_ff = w1.shape[-1]
    kern = functools.partial(_mlp_kernel, ff_chunk=ff_chunk, sub_rows=min(sub_rows, tm))
    vec = pl.BlockSpec((1, d), lambda i: (0, 0))
    return pl.pallas_call(
        kern,
        grid=(rows // tm,),
        in_specs=[
            pl.BlockSpec((tm, d), lambda i: (i, 0)),
            vec,
            _mod_block(d, 4, mod_row_fn),
            _mod_block(d, 3, mod_row_fn),
            _mod_block(d, 5, mod_row_fn),
            _const_spec((d, d_ff)),
            _const_spec((d_ff, d)),
            vec,
        ],
        out_specs=pl.BlockSpec((tm, d), lambda i: (i, 0)),
        out_shape=jax.ShapeDtypeStruct((rows, d), F32),
        compiler_params=_cparams("arbitrary"),
    )(x, pre.reshape(1, d), mods, mods, mods, w1, w2, post.reshape(1, d))


def _qkv_kernel(x_ref, pre_ref, sc_ref, sh_ref, w_ref, gains_ref, cos_ref, sin_ref,
                *out_refs, n_q_pairs, rope, sub_rows):
    q_ref = out_refs[0] if n_q_pairs else None
    k_ref, v_ref = out_refs[-2:]
    hd = HEAD_DIM
    tm = x_ref.shape[0]
    mult = pre_ref[...] * (1.0 + sc_ref[0, 0])
    sh = sh_ref[0, 0]
    q_scale = ATTN_SCALE * LOG2_E
    gain_ab = [(gains_ref[0:1, :] * q_scale, gains_ref[1:2, :] * q_scale), (gains_ref[2:3, :], gains_ref[3:4, :])]
    half_r = lax.broadcasted_iota(jnp.int32, (hd, hd), 0) // (hd // 2)
    half_c = lax.broadcasted_iota(jnp.int32, (hd, hd), 1) // (hd // 2)
    same_head = (half_r == half_c).astype(BF16)

    def rows_of(i):
        return slice(i * sub_rows, (i + 1) * sub_rows)

    def project(i):
        h = (_rms(x_ref[rows_of(i), :]) * mult + sh).astype(BF16)
        return _dot(h, w_ref[...])

    def finish(i, qkv):
        rows = rows_of(i)
        if rope:
            cos_t, sin_t = cos_ref[rows, :], sin_ref[rows, :]
            tabs = [(ga * cos_t, gb * sin_t, ga * sin_t, gb * cos_t) for ga, gb in gain_ab]

        def pair(idx, kind):
            a = qkv[:, 2 * idx * hd:(2 * idx + 1) * hd]
            b = qkv[:, (2 * idx + 1) * hd:(2 * idx + 2) * hd]
            ssq = _dot((a * a + b * b).astype(BF16), same_head)
            r = lax.rsqrt(ssq * (1.0 / hd) + EPS)
            if rope:
                ca, sb, sa, cb = tabs[kind]
                return (r * (a * ca - b * sb)).astype(BF16), (r * (a * sa + b * cb)).astype(BF16)
            ga, gb = gain_ab[kind]
            return (r * (a * ga)).astype(BF16), (r * (b * gb)).astype(BF16)

        for p in range(n_q_pairs):
            qa, qb = pair(p, 0)
            q_ref[rows, 2 * p * hd:(2 * p + 1) * hd] = qa
            q_ref[rows, (2 * p + 1) * hd:(2 * p + 2) * hd] = qb
        ka, kb = pair(n_q_pairs, 1)
        k_ref[rows, 0:hd] = ka
        k_ref[rows, hd:2 * hd] = kb
        v0 = 2 * (n_q_pairs + 1) * hd
        v_ref[rows, :] = qkv[:, v0:v0 + N_KV_HEADS * hd].astype(BF16)

    n_sub = tm // sub_rows
    pending = project(0)
    for i in range(1, n_sub):
        nxt = project(i)
        finish(i - 1, pending)
        pending = nxt
    finish(n_sub - 1, pending)


def _qkv(x, pre, mods, mod_row_fn, w, gains, cos_t, sin_t, *, n_q_pairs, rope, tm, seq):
    rows, d = x.shape
    hd = HEAD_DIM
    n = w.shape[-1]
    kern = functools.partial(_qkv_kernel, n_q_pairs=n_q_pairs, rope=rope, sub_rows=min(tm, 256))
    vec = pl.BlockSpec((1, d), lambda i: (0, 0))
    tiles_per_seq = seq // tm
    pos = pl.BlockSpec((tm, hd), lambda i: (i % tiles_per_seq, 0))
    widths = ([2 * n_q_pairs * hd] if n_q_pairs else []) + [N_KV_HEADS * hd] * 2
    return pl.pallas_call(
        kern,
        grid=(rows // tm,),
        in_specs=[
            pl.BlockSpec((tm, d), lambda i: (i, 0)),
            vec,
            _mod_block(d, 1, mod_row_fn),
            _mod_block(d, 0, mod_row_fn),
            _const_spec((d, n)),
            pl.BlockSpec((4, hd), lambda i: (0, 0)),
            pos, pos,
        ],
        out_specs=[pl.BlockSpec((tm, wd), lambda i: (i, 0)) for wd in widths],
        out_shape=[jax.ShapeDtypeStruct((rows, wd), BF16) for wd in widths],
        compiler_params=_cparams("arbitrary"),
    )(x, pre.reshape(1, d), mods, mods, w, gains, cos_t, sin_t)


def _attn_kernel(q_ref, kl_ref, vl_ref, kc_ref, vc_ref, x_ref, wo_ref, gain_ref, g_ref, o_ref, k_all, v_ext,
                 *, n_pairs, kv_blocks):
    hd = HEAD_DIM
    tq = q_ref.shape[1]
    seq = kl_ref.shape[1]
    total = k_all.shape[0]

    @pl.when(pl.program_id(1) == 0)
    def _():
        k_all[0:seq, :] = kl_ref[0]
        k_all[seq:total, :] = kc_ref[0]
        for h in range(N_KV_HEADS):
            v_ext[h, 0:seq, 0:hd] = vl_ref[0, :, h * hd:(h + 1) * hd]
            v_ext[h, seq:total, 0:hd] = vc_ref[0, :, h * hd:(h + 1) * hd]
            v_ext[h, :, hd:2 * hd] = jnp.ones((total, hd), BF16)

    nt = (((1,), (1,)), ((), ()))
    lane_half = (lax.broadcasted_iota(jnp.int32, (1, 2 * hd), 1) % hd) // (hd // 2)

    def start(kv_head):
        keep = (lane_half == kv_head).astype(BF16)
        q = jnp.concatenate([q_ref[0, :, 2 * p * hd:(2 * p + 2) * hd] * keep for p in range(n_pairs)], axis=0)
        st = dict(h=kv_head, q=q, m=None, acc=None)
        st["s_next"] = lax.dot_general(q, k_all[kv_blocks[0][0]:kv_blocks[0][1], :], nt, preferred_element_type=F32)
        return st

    def step(st, i):
        s0, s1 = kv_blocks[i]
        s = st["s_next"]
        if i + 1 < len(kv_blocks):
            n0, n1 = kv_blocks[i + 1]
            st["s_next"] = lax.dot_general(st["q"], k_all[n0:n1, :], nt, preferred_element_type=F32)
        m, acc = st["m"], st["acc"]
        m_blk = jnp.max(s, axis=-1, keepdims=True)
        m_new = m_blk if m is None else jnp.maximum(m, m_blk)
        pv = _dot(jnp.exp2(s - m_new).astype(BF16), v_ext[st["h"], s0:s1, :])
        st["acc"] = pv if m is None else acc * jnp.exp2(m - m_new) + pv
        st["m"] = m_new

    def heads_of(st):
        acc = st["acc"]
        o = (acc[:, 0:hd] / acc[:, hd:2 * hd]).astype(BF16)
        return [o[p * tq:(p + 1) * tq, :] for p in range(n_pairs)]

    last = len(kv_blocks) - 1
    first, second = start(0), None
    for i in range(last):
        step(first, i)
    second = start(1)
    step(first, last)
    for i in range(last + 1):
        step(second, i)
    attn = jnp.concatenate(heads_of(first) + heads_of(second), axis=1)
    y = _dot(attn, wo_ref[...])
    o_ref[0] = x_ref[0] + g_ref[0, 0] * (_rms(y) * gain_ref[...])


def _kv_blocks(total, pattern):
    assert total % MXU_WIDTH == 0
    n_tiles = total // MXU_WIDTH
    blocks, start, i = [], 0, 0
    while start < n_tiles:
        size = min(pattern[i % len(pattern)], n_tiles - start)
        blocks.append((start * MXU_WIDTH, (start + size) * MXU_WIDTH))
        start += size
        i += 1
    return tuple(blocks)


def _attention(q, k_l, v_l, k_c, v_c, x, w_o, gain, mods, gate_chunk, mod_row_fn, *, tq=256, kv_pattern=(5, 3, 1)):
    bsz, seq, dq = q.shape
    d = x.shape[-1]
    ctx_len = k_c.shape[1]
    hd = HEAD_DIM
    kv = N_KV_HEADS * hd
    n_pairs = dq // (2 * hd)
    total = seq + ctx_len
    kern = functools.partial(_attn_kernel, n_pairs=n_pairs, kv_blocks=_kv_blocks(total, kv_pattern))
    whole = lambda rows, width: pl.BlockSpec((1, rows, width), lambda b, i: (b, 0, 0))
    tile = lambda width: pl.BlockSpec((1, tq, width), lambda b, i: (b, i, 0))
    return pl.pallas_call(
        kern,
        grid=(bsz, seq // tq),
        in_specs=[
            tile(dq), whole(seq, 2 * hd), whole(seq, kv), whole(ctx_len, 2 * hd), whole(ctx_len, kv),
            tile(d),
            _const_spec((dq, d)),
            pl.BlockSpec((1, d), lambda b, i: (0, 0)),
            _mod_block(d, gate_chunk, lambda b, i: mod_row_fn(b)),
        ],
        out_specs=tile(d),
        out_shape=jax.ShapeDtypeStruct((bsz, seq, d), F32),
        scratch_shapes=[pltpu.VMEM((total, 2 * hd), BF16), pltpu.VMEM((N_KV_HEADS, total, 2 * hd), BF16)],
        compiler_params=_cparams("arbitrary", "arbitrary"),
    )(q, k_l, v_l, k_c, v_c, x, w_o, gain.reshape(1, d), mods)


def _rope_tables(seq):
    rows = seq // GRID_W
    pairs = HEAD_DIM // 4
    row = jnp.repeat(jnp.arange(rows, dtype=F32), GRID_W)
    col = jnp.tile(jnp.arange(GRID_W, dtype=F32), rows)
    inv = ROPE_THETA ** (-jnp.arange(pairs, dtype=F32) / pairs)
    ang = jnp.concatenate([row[:, None] * inv[None, :], col[:, None] * inv[None, :]], axis=-1)
    cos, sin = jnp.cos(ang), jnp.sin(ang)
    return jnp.concatenate([cos, cos], axis=-1), jnp.concatenate([sin, sin], axis=-1)


def _pair_columns(head_a, head_b):
    hd = HEAD_DIM
    even, odd = jnp.arange(0, hd, 2), jnp.arange(1, hd, 2)
    return jnp.concatenate([head_a * hd + even, head_b * hd + even, head_a * hd + odd, head_b * hd + odd])


def kernel(x, c, ctx, c_ctx, mod_w, mod_b, mix_norm_pre, mix_norm_post, mlp_norm_pre, mlp_norm_post, mlp_w1, mlp_w2, hy_w_in, hy_b_in, hy_conv_w, hy_conv_b, hy_filt_w1, hy_filt_b1, hy_filt_freq1, hy_filt_w2, hy_filt_b2, hy_filt_freq2, hy_filt_w3, hy_filt_bias, hy_w_out, hy_b_out, attn_w_qkv, attn_q_norm, attn_k_norm, attn_w_o):
    bsz, seq, d = x.shape
    ctx_len = ctx.shape[1]
    hd = HEAD_DIM
    n_heads = d // hd
    tm = 1024
    tmc = ctx_len

    ctx_row = bsz
    n_rows = -(-(bsz + 1) // MOD_ROWS_PAD) * MOD_ROWS_PAD
    cond = jnp.concatenate([c, c_ctx[None, :], jnp.zeros((n_rows - bsz - 1, d), F32)], axis=0)
    mods_all = _modulation(cond, mod_w, mod_b).reshape(mod_w.shape[0], n_rows, 6, 1, d)

    x_row = lambda i: i // (seq // tm)
    c_row = lambda i: ctx_row

    xf = x.reshape(bsz * seq, d)
    cf = ctx.reshape(bsz * ctx_len, d)

    mods = mods_all[0]
    w_in = hy_w_in[0].astype(BF16)
    w_out = hy_w_out[0].astype(BF16)
    filt = (hy_filt_w1[0], hy_filt_b1[0], hy_filt_freq1[0], hy_filt_w2[0], hy_filt_b2[0], hy_filt_freq2[0],
            hy_filt_w3[0], hy_filt_bias[0])
    w1 = mlp_w1[0].astype(BF16)
    w2 = mlp_w2[0].astype(BF16)

    def hyena(tokens, length, mod_row_b):
        tokens = tokens.reshape(bsz, length, d)
        u, x1 = _hyena_in(tokens, mix_norm_pre[0], mods, mod_row_b, w_in, hy_b_in[0], hy_conv_w[0], hy_conv_b[0])
        gated = _fftconv(u, x1, _hyena_filter_tables(length, *filt))
        out = _proj_res_parity(gated, tokens, w_out, hy_b_out[0], mix_norm_post[0], mods, 2, mod_row_b, th=512)
        return out.reshape(bsz * length, d)

    xf = hyena(xf, seq, lambda b: b)
    cf = hyena(cf, ctx_len, lambda b: ctx_row)
    xf = _mlp(xf, mlp_norm_pre[0], mlp_norm_post[0], mods, x_row, w1, w2, tm=tm)
    cf = _mlp(cf, mlp_norm_pre[0], mlp_norm_post[0], mods, c_row, w1, w2, tm=min(tm, bsz * ctx_len))

    mods = mods_all[1]
    w_qkv = attn_w_qkv[0]
    assert N_KV_HEADS == 2 and n_heads % 2 == 0
    group = n_heads // N_KV_HEADS
    pair_cols = [_pair_columns(p, group + p) for p in range(group)] + [_pair_columns(n_heads, n_heads + 1)]
    v_cols = jnp.arange((n_heads + N_KV_HEADS) * hd, (n_heads + 2 * N_KV_HEADS) * hd)
    w_lat = w_qkv[:, jnp.concatenate(pair_cols + [v_cols])].astype(BF16)
    w_ctx = w_lat[:, n_heads * hd:]
    even, odd = jnp.arange(0, hd, 2), jnp.arange(1, hd, 2)
    gains = jnp.stack([jnp.tile(g[idx], 2) for g in (attn_q_norm[0], attn_k_norm[0]) for idx in (even, odd)])
    cos_t, sin_t = _rope_tables(seq)

    q, k_l, v_l = _qkv(xf, mix_norm_pre[1], mods, x_row, w_lat, gains, cos_t, sin_t,
                       n_q_pairs=group, rope=True, tm=tm, seq=seq)
    k_c, v_c = _qkv(cf, mix_norm_pre[1], mods, c_row, w_ctx, gains, cos_t, sin_t,
                    n_q_pairs=0, rope=False, tm=tmc, seq=tmc)
    kv = N_KV_HEADS * hd
    xf = _attention(q.reshape(bsz, seq, d), k_l.reshape(bsz, seq, kv), v_l.reshape(bsz, seq, kv),
                    k_c.reshape(bsz, ctx_len, kv), v_c.reshape(bsz, ctx_len, kv), xf.reshape(bsz, seq, d),
                    attn_w_o[0].astype(BF16), mix_norm_post[1], mods, 2, lambda b: b).reshape(bsz * seq, d)
    xf = _mlp(xf, mlp_norm_pre[1], mlp_norm_post[1], mods, x_row, mlp_w1[1].astype(BF16),
              mlp_w2[1].astype(BF16), tm=tm)
    return xf.reshape(bsz, seq, d)
```

```python
import functools
import math

import jax
import jax.numpy as jnp
from jax import lax
from jax.experimental import pallas as pl
from jax.experimental.pallas import tpu as pltpu

F32 = jnp.float32
BF16 = jnp.bfloat16

EPS = 1e-6
GRID_W = 64
HY_BANDS = 16
HY_DECAY_TARGET = 1e-2
HY_FAST = 0.3
HY_SLOW = 1.5
HY_SHIFT = 0.0
HEAD_DIM = 128
N_KV_HEADS = 2
ROPE_THETA = 10000.0
ATTN_SCALE = HEAD_DIM ** -0.5
LOG2_E = math.log2(math.e)

MOD_ROWS_PAD = 8
SUBLANES = 8
LANES = 128
MXU_WIDTH = 256

VMEM_LIMIT = 56 * 1024 * 1024


def _cparams(*sem):
    return pltpu.CompilerParams(dimension_semantics=sem, vmem_limit_bytes=VMEM_LIMIT)


def _const_spec(shape):
    nd = len(shape)
    return pl.BlockSpec(shape, lambda *_: (0,) * nd, pipeline_mode=pl.Buffered(1))


def _rms(x):
    return x * lax.rsqrt(jnp.mean(x * x, axis=-1, keepdims=True) + EPS)


def _norm_mod(x, gain, sc, sh):
    return _rms(x) * (gain * (1.0 + sc)) + sh


def _dot(a, b):
    return jnp.dot(a, b, preferred_element_type=F32)


def _dot_hi(a, b):
    return jnp.dot(a, b, preferred_element_type=F32, precision=lax.Precision.HIGHEST)


def _mod_kernel(cond_ref, w_ref, b_ref, o_ref):
    s = cond_ref[...]
    s = s * jax.nn.sigmoid(s)
    o_ref[0] = _dot(s.astype(BF16), w_ref[0].astype(BF16)) + b_ref[0]


def _modulation(cond, mod_w, mod_b):
    depth, d, n = mod_w.shape
    rows = cond.shape[0]
    tn = d
    return pl.pallas_call(
        _mod_kernel,
        grid=(depth, n // tn),
        in_specs=[
            pl.BlockSpec((rows, d), lambda i, j: (0, 0)),
            pl.BlockSpec((1, d, tn), lambda i, j: (i, 0, j)),
            pl.BlockSpec((1, 1, tn), lambda i, j: (i, 0, j)),
        ],
        out_specs=pl.BlockSpec((1, rows, tn), lambda i, j: (i, 0, j)),
        out_shape=jax.ShapeDtypeStruct((depth, rows, n), F32),
        compiler_params=_cparams("arbitrary", "arbitrary"),
    )(cond, mod_w, mod_b.reshape(depth, 1, n))


def _mod_block(d, chunk, row_fn):
    return pl.BlockSpec((1, 1, 1, d), lambda *idx: (row_fn(*idx), chunk, 0, 0))


def _hyena_in_kernel(x_ref, gain_ref, sc_ref, sh_ref, w0_ref, w1_ref, w2_ref, b_ref, cw_ref, cb_ref,
                     u_ref, x1_ref, h_buf, x_slabs, *, seq, rc):
    j = pl.program_id(1)
    w = u_ref.shape[-1]
    half = seq // 2
    n_chunks = half // rc
    n_slabs = x_slabs.shape[0]

    @pl.when(j == 0)
    def _():
        mult = gain_ref[...] * (1.0 + sc_ref[0, 0])
        sh = sh_ref[0, 0]

        def body(c, carry):
            base = pl.multiple_of(2 * c * rc, 2 * rc)
            for k in range(n_slabs):
                x_slabs[k] = x_ref[0, pl.ds(base, 2 * rc), k * LANES:(k + 1) * LANES]
            for parity in range(2):
                xs = jnp.concatenate([x_slabs[k, pl.ds(parity, rc, stride=2), :] for k in range(n_slabs)], axis=1)
                h_buf[pl.ds(base + parity * rc, rc), :] = (_rms(xs) * mult + sh).astype(BF16)
            return carry

        lax.fori_loop(0, n_chunks, body, 0)

    sub = lax.broadcasted_iota(jnp.int32, (SUBLANES, w), 0)
    w_refs = (w0_ref, w1_ref, w2_ref)
    taps = [[cw_ref[k, s:s + 1, :] for s in range(3)] for k in range(3)]
    bias = [cb_ref[s:s + 1, :] + b_ref[s:s + 1, :] * (taps[0][s] + taps[1][s] + taps[2][s]) for s in range(3)]

    def project(c):
        h = h_buf[2 * c * rc:2 * (c + 1) * rc, :]
        return [_dot(h, w_refs[s][...]) for s in range(3)]

    def conv(c, m_before, m_here, m_after):
        even_streams, odd_streams = [], []
        for s in range(3):
            m_even, m_odd = m_here[s][0:rc], m_here[s][rc:2 * rc]
            row_before = -b_ref[s:s + 1, :] if m_before is None else m_before[s][2 * rc - 1:2 * rc, :]
            row_after = -b_ref[s:s + 1, :] if m_after is None else m_after[s][0:1, :]
            odd_prev = pltpu.roll(m_odd, 1, 0)
            odd_prev = jnp.concatenate([jnp.where(sub == 0, row_before, odd_prev[0:SUBLANES]),
                                        odd_prev[SUBLANES:]], axis=0)
            even_next = pltpu.roll(m_even, rc - 1, 0)
            even_next = jnp.concatenate([even_next[:rc - SUBLANES],
                                         jnp.where(sub == SUBLANES - 1, row_after, even_next[rc - SUBLANES:])], axis=0)
            even_streams.append(odd_prev * taps[0][s] + m_even * taps[1][s] + m_odd * taps[2][s] + bias[s])
            odd_streams.append(m_even * taps[0][s] + m_odd * taps[1][s] + even_next * taps[2][s] + bias[s])
        for base, (x1, x2, v) in ((0, even_streams), (half, odd_streams)):
            u_ref[0, base + c * rc:base + (c + 1) * rc, :] = (v * x2).astype(BF16)
            x1_ref[0, base + c * rc:base + (c + 1) * rc, :] = x1.astype(BF16)

    zs = [project(0)]
    for c in range(1, n_chunks):
        zs.append(project(c))
        conv(c - 1, zs[c - 2] if c >= 2 else None, zs[c - 1], zs[c])
    conv(n_chunks - 1, zs[n_chunks - 2] if n_chunks >= 2 else None, zs[n_chunks - 1], None)


def _hyena_in(x, gain, mods, mod_row_fn, w_in, b_in, conv_w, conv_b, *, block_elems=2048 * 512, rc=128):
    bsz, seq, d = x.shape
    w = min(d, max(MXU_WIDTH, block_elems // seq // MXU_WIDTH * MXU_WIDTH))
    nj = d // w
    rc = min(rc, seq // 2)
    kern = functools.partial(_hyena_in_kernel, seq=seq, rc=rc)
    wspec = lambda s: pl.BlockSpec((d, w), lambda b, j: (0, s * nj + j))
    out_spec = pl.BlockSpec((1, seq, w), lambda b, j: (b, 0, j))
    return pl.pallas_call(
        kern,
        grid=(bsz, nj),
        in_specs=[
            pl.BlockSpec((1, seq, d), lambda b, j: (b, 0, 0)),
            pl.BlockSpec((1, d), lambda b, j: (0, 0)),
            _mod_block(d, 1, lambda b, j: mod_row_fn(b)),
            _mod_block(d, 0, lambda b, j: mod_row_fn(b)),
            wspec(0), wspec(1), wspec(2),
            pl.BlockSpec((3, w), lambda b, j: (0, j)),
            pl.BlockSpec((3, 3, w), lambda b, j: (0, 0, j)),
            pl.BlockSpec((3, w), lambda b, j: (0, j)),
        ],
        out_specs=[out_spec, out_spec],
        out_shape=[jax.ShapeDtypeStruct((bsz, seq, d), BF16)] * 2,
        scratch_shapes=[pltpu.VMEM((seq, d), BF16), pltpu.VMEM((d // LANES, 2 * rc, LANES), F32)],
        compiler_params=_cparams("arbitrary", "arbitrary"),
    )(x, gain.reshape(1, d), mods, mods, w_in, w_in, w_in,
      b_in.reshape(3, d), conv_w.reshape(3, 3, d), conv_b.reshape(3, d))


def _dft_table(length, n_freq, shifted, kb=32):
    if shifted:
        period = 8 * length
        nn = 2 * jnp.arange(length, dtype=jnp.int32)[None, :] + 1
    else:
        period = 4 * length
        nn = jnp.arange(length, dtype=jnp.int32)[None, :]
    kh = jnp.arange(n_freq // kb, dtype=jnp.int32)[:, None]
    kl = jnp.arange(kb, dtype=jnp.int32)[:, None]
    to_angle = lambda m: (m % period).astype(F32) * (2.0 * math.pi / period)
    alpha = to_angle(2 * kb * kh * nn)[:, None, :]
    beta = to_angle((2 * kl + 1) * nn)[None, :, :]
    ca, sa, cb, sb = jnp.cos(alpha), jnp.sin(alpha), jnp.cos(beta), jnp.sin(beta)
    cos_t = (ca * cb - sa * sb).reshape(n_freq, length)
    sin_t = (sa * cb + ca * sb).reshape(n_freq, length)
    return jnp.concatenate([cos_t, sin_t], axis=0).astype(BF16)


def _filter_kernel(bands_ref, w1t_ref, w1c_ref, w1s_ref, b1_ref, f1_ref, w2_ref, b2_ref, f2_ref,
                   w3f_ref, w3b_ref, deltas_ref, bias_ref, psi_ref, tab_ref, o_ref, h_buf, *, seq):
    w = o_ref.shape[-1]
    half = seq // 2

    @pl.when(pl.program_id(0) == 0)
    def _():
        t64 = lax.broadcasted_iota(jnp.int32, (seq, w1t_ref.shape[-1]), 0).astype(F32) / seq
        t16 = lax.broadcasted_iota(jnp.int32, (seq, HY_BANDS), 0).astype(F32) / seq
        ang = (2.0 * math.pi * t16) * bands_ref[...]
        pre = t64 * w1t_ref[...] + _dot_hi(jnp.cos(ang), w1c_ref[...]) + _dot_hi(jnp.sin(ang), w1s_ref[...])
        h = jnp.sin(f1_ref[...] * (pre + b1_ref[...]))
        h_buf[...] = jnp.sin(f2_ref[...] * (_dot_hi(h, w2_ref[...]) + b2_ref[...]))

    h = h_buf[...]
    row = lax.broadcasted_iota(jnp.int32, (seq, w), 0)
    decay = jnp.exp(-(row.astype(F32) / seq) * deltas_ref[...]) + HY_SHIFT
    h_f = _dot_hi(h, w3f_ref[...]) * decay
    h_b = _dot_hi(h, w3b_ref[...]) * decay
    h_b = jnp.where(row == 0, 0.0, h_b)
    h_sum = h_f + h_b
    h_dif = h_b - h_f
    alt = jnp.where(row % 2 == 0, 1.0, -1.0)
    cos_lo = tab_ref[pl.ds(0, half), :]
    sin_lo = tab_ref[pl.ds(half, half), :]
    bias = bias_ref[...]
    p_lo = _dot(cos_lo, h_sum.astype(BF16)) + bias
    p_hi = _dot(cos_lo, (h_sum * alt).astype(BF16)) + bias
    q_lo = _dot(sin_lo, h_dif.astype(BF16))
    q_hi = -_dot(sin_lo, (h_dif * alt).astype(BF16))
    d_re = p_lo - p_hi
    d_im = q_lo + q_hi
    psi_c = jnp.concatenate([psi_ref[0]] * (w // psi_ref.shape[-1]), axis=1)
    psi_s = jnp.concatenate([psi_ref[1]] * (w // psi_ref.shape[-1]), axis=1)
    o_ref[0] = p_lo + p_hi
    o_ref[1] = q_lo - q_hi
    o_ref[2] = d_re * psi_c - d_im * psi_s
    o_ref[3] = d_re * psi_s + d_im * psi_c
    o_ref[4] = d_re * psi_c + d_im * psi_s
    o_ref[5] = d_im * psi_c - d_re * psi_s


def _hyena_filter_tables(seq, fw1, fb1, ff1, fw2, fb2, ff2, fw3, fbias, *, w=256):
    d = fbias.shape[-1]
    fwid = fw2.shape[0]
    half = seq // 2
    lanes = LANES
    bands = jnp.linspace(1e-4, HY_BANDS - 1, HY_BANDS, dtype=F32).reshape(1, HY_BANDS)
    deltas = jnp.abs(jnp.linspace(math.log(HY_DECAY_TARGET) / HY_SLOW, math.log(HY_DECAY_TARGET) / HY_FAST,
                                  d, dtype=F32)).reshape(1, d)
    theta = (2 * jnp.arange(half, dtype=F32) + 1.0) * (math.pi / (2 * seq))
    psi = jnp.broadcast_to(jnp.stack([jnp.cos(theta), jnp.sin(theta)])[:, :, None], (2, half, lanes))
    tab_plain = _dft_table(seq, half, shifted=False)
    nj = d // w
    small = lambda shape: pl.BlockSpec(shape, lambda j: (0,) * len(shape))
    kern = functools.partial(_filter_kernel, seq=seq)
    return pl.pallas_call(
        kern,
        grid=(nj,),
        in_specs=[
            small((1, HY_BANDS)), small((1, fwid)), small((HY_BANDS, fwid)), small((HY_BANDS, fwid)),
            small((1, fwid)), small((1, fwid)), small((fwid, fwid)), small((1, fwid)), small((1, fwid)),
            pl.BlockSpec((fwid, w), lambda j: (0, j)),
            pl.BlockSpec((fwid, w), lambda j: (0, nj + j)),
            pl.BlockSpec((1, w), lambda j: (0, j)),
            pl.BlockSpec((1, w), lambda j: (0, j)),
            small((2, half, lanes)),
            _const_spec((2 * half, seq)),
        ],
        out_specs=pl.BlockSpec((6, half, w), lambda j: (0, 0, j)),
        out_shape=jax.ShapeDtypeStruct((6, half, d), F32),
        scratch_shapes=[pltpu.VMEM((seq, fwid), F32)],
        compiler_params=_cparams("arbitrary"),
    )(bands, fw1[0:1], fw1[1:1 + HY_BANDS], fw1[1 + HY_BANDS:], fb1.reshape(1, fwid), ff1.reshape(1, fwid),
      fw2, fb2.reshape(1, fwid), ff2.reshape(1, fwid), fw3, fw3, deltas, fbias.reshape(1, d), psi, tab_plain)


def _fftconv_kernel(u_ref, x1_ref, sef_ref, tab_ref, o_ref, *, seq):
    half = seq // 2
    cos_t = tab_ref[pl.ds(0, half), :]
    sin_t = tab_ref[pl.ds(half, half), :]
    u0 = u_ref[0, 0:half, :]
    u1 = u_ref[0, half:seq, :]
    a0, b0 = _dot(cos_t, u0), _dot(sin_t, u0)
    a1, b1 = _dot(cos_t, u1), _dot(sin_t, u1)
    s_re, s_im, e_re, e_im, f_re, f_im = (sef_ref[i] for i in range(6))
    v0_re = (s_re * a0 + s_im * b0 + f_re * a1 + f_im * b1).astype(BF16)
    v0_im = (s_im * a0 - s_re * b0 + f_im * a1 - f_re * b1).astype(BF16)
    v1_re = (e_re * a0 + e_im * b0 + s_re * a1 + s_im * b1).astype(BF16)
    v1_im = (e_im * a0 - e_re * b0 + s_im * a1 - s_re * b1).astype(BF16)
    y0 = _dot(cos_t, v0_re) - _dot(sin_t, v0_im)
    y1 = _dot(cos_t, v1_re) - _dot(sin_t, v1_im)
    o_ref[0, 0:half, :] = (y0 * (1.0 / seq) * x1_ref[0, 0:half, :].astype(F32)).astype(BF16)
    o_ref[0, half:seq, :] = (y1 * (1.0 / seq) * x1_ref[0, half:seq, :].astype(F32)).astype(BF16)


def _fftconv(u, x1, sef, *, block_elems=2048 * 256):
    bsz, seq, d = u.shape
    half = seq // 2
    w = min(d, max(MXU_WIDTH, block_elems // seq // MXU_WIDTH * MXU_WIDTH))
    nj = d // w
    act = pl.BlockSpec((1, seq, w), lambda j, b: (b, 0, j))
    kern = functools.partial(_fftconv_kernel, seq=seq)
    return pl.pallas_call(
        kern,
        grid=(nj, bsz),
        in_specs=[act, act, pl.BlockSpec((6, half, w), lambda j, b: (0, 0, j)), _const_spec((2 * half, half))],
        out_specs=act,
        out_shape=jax.ShapeDtypeStruct((bsz, seq, d), BF16),
        compiler_params=_cparams("arbitrary", "arbitrary"),
    )(u, x1, sef, _dft_table(half, half, shifted=True))


def _proj_res_parity_kernel(a_ref, x_ref, w_ref, b_ref, gain_ref, g_ref, o_ref, x_slabs, o_slabs):
    th = a_ref.shape[2]
    n_slabs = x_slabs.shape[0]
    for k in range(n_slabs):
        x_slabs[k] = x_ref[0, :, k * LANES:(k + 1) * LANES]
    for parity in range(2):
        rows = pl.ds(parity, th, stride=2)
        y = _dot(a_ref[0, parity], w_ref[...]) + b_ref[...]
        x_par = jnp.concatenate([x_slabs[k, rows, :] for k in range(n_slabs)], axis=1)
        out = x_par + g_ref[0, 0] * (_rms(y) * gain_ref[...])
        for k in range(n_slabs):
            o_slabs[k, rows, :] = out[:, k * LANES:(k + 1) * LANES]
    for k in range(n_slabs):
        o_ref[0, :, k * LANES:(k + 1) * LANES] = o_slabs[k]


def _proj_res_parity(a, x, w, b, gain, mods, gate_chunk, mod_row_fn, *, th):
    bsz, seq, d = x.shape
    half = seq // 2
    th = min(th, half)
    return pl.pallas_call(
        _proj_res_parity_kernel,
        grid=(bsz, half // th),
        in_specs=[
            pl.BlockSpec((1, 2, th, d), lambda b, i: (b, 0, i, 0)),
            pl.BlockSpec((1, 2 * th, d), lambda b, i: (b, i, 0)),
            _const_spec((d, d)),
            pl.BlockSpec((1, d), lambda b, i: (0, 0)),
            pl.BlockSpec((1, d), lambda b, i: (0, 0)),
            _mod_block(d, gate_chunk, lambda b, i: mod_row_fn(b)),
        ],
        out_specs=pl.BlockSpec((1, 2 * th, d), lambda b, i: (b, i, 0)),
        out_shape=jax.ShapeDtypeStruct((bsz, seq, d), F32),
        scratch_shapes=[pltpu.VMEM((d // LANES, 2 * th, LANES), F32)] * 2,
        compiler_params=_cparams("arbitrary", "arbitrary"),
    )(a.reshape(bsz, 2, half, d), x, w, b.reshape(1, d), gain.reshape(1, d), mods)


def _mlp_kernel(x_ref, pre_ref, sc_ref, sh_ref, g_ref, w1_ref, w2_ref, post_ref, o_ref, *, ff_chunk, sub_rows):
    d_ff = w1_ref.shape[-1]
    mult = pre_ref[...] * (1.0 + sc_ref[0, 0])
    sh = sh_ref[0, 0]

    def rows_of(i):
        return slice(i * sub_rows, (i + 1) * sub_rows)

    def hidden(i):
        return (_rms(x_ref[rows_of(i), :]) * mult + sh).astype(BF16)

    def mix(h):
        acc = None
        for c in range(d_ff // ff_chunk):
            cols = pl.ds(c * ff_chunk, ff_chunk)
            a = jnp.maximum(_dot(h, w1_ref[:, cols]), 0.0)
            part = _dot((a * a).astype(BF16), w2_ref[cols, :])
            acc = part if acc is None else acc + part
        return acc

    def finish(i, acc):
        o_ref[rows_of(i), :] = x_ref[rows_of(i), :] + g_ref[0, 0] * (_rms(acc) * post_ref[...])

    n_sub = x_ref.shape[0] // sub_rows
    pending = mix(hidden(0))
    for i in range(1, n_sub):
        nxt = mix(hidden(i))
        finish(i - 1, pending)
        pending = nxt
    finish(n_sub - 1, pending)


def _mlp(x, pre, post, mods, mod_row_fn, w1, w2, *, tm, ff_chunk=1024, sub_rows=512):
    rows, d = x.shape
    d_ff = w1.shape[-1]
    kern = functools.partial(_mlp_kernel, ff_chunk=ff_chunk, sub_rows=min(sub_rows, tm))
    vec = pl.BlockSpec((1, d), lambda i: (0, 0))
    return pl.pallas_call(
        kern,
        grid=(rows // tm,),
        in_specs=[
            pl.BlockSpec((tm, d), lambda i: (i, 0)),
            vec,
            _mod_block(d, 4, mod_row_fn),
            _mod_block(d, 3, mod_row_fn),
            _mod_block(d, 5, mod_row_fn),
            _const_spec((d, d_ff)),
            _const_spec((d_ff, d)),
            vec,
        ],
        out_specs=pl.BlockSpec((tm, d), lambda i: (i, 0)),
        out_shape=jax.ShapeDtypeStruct((rows, d), F32),
        compiler_params=_cparams("arbitrary"),
    )(x, pre.reshape(1, d), mods, mods, mods, w1, w2, post.reshape(1, d))


def _hyena_out_mlp_kernel(a_ref, x_ref, wo_ref, bo_ref, mixpost_ref, g1_ref, pre_ref, sc_ref, sh_ref, g2_ref,
                          w1_ref, w2_ref, post_ref, o_ref, slabs, *, ff_chunk):
    th = a_ref.shape[2]
    d_ff = w1_ref.shape[-1]
    n_slabs = slabs.shape[0]
    for k in range(n_slabs):
        slabs[k] = x_ref[0, :, k * LANES:(k + 1) * LANES]
    parity_rows = [pl.ds(parity, th, stride=2) for parity in range(2)]
    x = jnp.concatenate([jnp.concatenate([slabs[k, rows, :] for k in range(n_slabs)], axis=1)
                         for rows in parity_rows], axis=0)
    a = jnp.concatenate([a_ref[0, 0], a_ref[0, 1]], axis=0)
    y = _dot(a, wo_ref[...]) + bo_ref[...]
    x1 = x + g1_ref[0, 0] * (_rms(y) * mixpost_ref[...])
    h = _norm_mod(x1, pre_ref[...], sc_ref[0, 0], sh_ref[0, 0]).astype(BF16)
    acc = None
    for c in range(d_ff // ff_chunk):
        cols = pl.ds(c * ff_chunk, ff_chunk)
        t = jnp.maximum(_dot(h, w1_ref[:, cols]), 0.0)
        part = _dot((t * t).astype(BF16), w2_ref[cols, :])
        acc = part if acc is None else acc + part
    out = x1 + g2_ref[0, 0] * (_rms(acc) * post_ref[...])
    for parity, rows in enumerate(parity_rows):
        for k in range(n_slabs):
            slabs[k, rows, :] = out[parity * th:(parity + 1) * th, k * LANES:(k + 1) * LANES]
    for k in range(n_slabs):
        o_ref[0, :, k * LANES:(k + 1) * LANES] = slabs[k]


def _hyena_out_mlp(a, x, w_out, b_out, mix_post, pre, post, mods, mod_row_fn, w1, w2, *, th=256, ff_chunk=1024):
    bsz, seq, d = x.shape
    half = seq // 2
    d_ff = w1.shape[-1]
    th = min(th, half)
    kern = functools.partial(_hyena_out_mlp_kernel, ff_chunk=ff_chunk)
    vec = pl.BlockSpec((1, d), lambda b, i: (0, 0))
    mod = lambda chunk: _mod_block(d, chunk, lambda b, i: mod_row_fn(b))
    rows = pl.BlockSpec((1, 2 * th, d), lambda b, i: (b, i, 0))
    return pl.pallas_call(
        kern,
        grid=(bsz, half // th),
        in_specs=[
            pl.BlockSpec((1, 2, th, d), lambda b, i: (b, 0, i, 0)),
            rows,
            _const_spec((d, d)), vec, vec, mod(2),
            vec, mod(4), mod(3), mod(5),
            _const_spec((d, d_ff)), _const_spec((d_ff, d)), vec,
        ],
        out_specs=rows,
        out_shape=jax.ShapeDtypeStruct((bsz, seq, d), F32),
        scratch_shapes=[pltpu.VMEM((d // LANES, 2 * th, LANES), F32)],
        compiler_params=_cparams("arbitrary", "arbitrary"),
    )(a.reshape(bsz, 2, half, d), x, w_out, b_out.reshape(1, d), mix_post.reshape(1, d), mods,
      pre.reshape(1, d), mods, mods, mods, w1, w2, post.reshape(1, d))


def _qkv_kernel(x_ref, pre_ref, sc_ref, sh_ref, w_ref, gains_ref, cos_ref, sin_ref,
                *out_refs, n_q_pairs, rope, sub_rows):
    q_ref = out_refs[0] if n_q_pairs else None
    k_ref, v_ref = out_refs[-2:]
    hd = HEAD_DIM
    tm = x_ref.shape[0]
    mult = pre_ref[...] * (1.0 + sc_ref[0, 0])
    sh = sh_ref[0, 0]
    q_scale = ATTN_SCALE * LOG2_E
    gain_ab = [(gains_ref[0:1, :] * q_scale, gains_ref[1:2, :] * q_scale), (gains_ref[2:3, :], gains_ref[3:4, :])]
    half_r = lax.broadcasted_iota(jnp.int32, (hd, hd), 0) // (hd // 2)
    half_c = lax.broadcasted_iota(jnp.int32, (hd, hd), 1) // (hd // 2)
    same_head = (half_r == half_c).astype(BF16)

    def rows_of(i):
        return slice(i * sub_rows, (i + 1) * sub_rows)

    def project(i):
        h = (_rms(x_ref[rows_of(i), :]) * mult + sh).astype(BF16)
        return _dot(h, w_ref[...])

    def finish(i, qkv):
        rows = rows_of(i)
        if rope:
            cos_t, sin_t = cos_ref[rows, :], sin_ref[rows, :]
            tabs = [(ga * cos_t, gb * sin_t, ga * sin_t, gb * cos_t) for ga, gb in gain_ab]

        def pair(idx, kind):
            a = qkv[:, 2 * idx * hd:(2 * idx + 1) * hd]
            b = qkv[:, (2 * idx + 1) * hd:(2 * idx + 2) * hd]
            ssq = _dot((a * a + b * b).astype(BF16), same_head)
            r = lax.rsqrt(ssq * (1.0 / hd) + EPS)
            if rope:
                ca, sb, sa, cb = tabs[kind]
                return (r * (a * ca - b * sb)).astype(BF16), (r * (a * sa + b * cb)).astype(BF16)
            ga, gb = gain_ab[kind]
            return (r * (a * ga)).astype(BF16), (r * (b * gb)).astype(BF16)

        for p in range(n_q_pairs):
            qa, qb = pair(p, 0)
            q_ref[rows, 2 * p * hd:(2 * p + 1) * hd] = qa
            q_ref[rows, (2 * p + 1) * hd:(2 * p + 2) * hd] = qb
        ka, kb = pair(n_q_pairs, 1)
        k_ref[rows, 0:hd] = ka
        k_ref[rows, hd:2 * hd] = kb
        v0 = 2 * (n_q_pairs + 1) * hd
        v_ref[rows, :] = qkv[:, v0:v0 + N_KV_HEADS * hd].astype(BF16)

    n_sub = tm // sub_rows
    pending = project(0)
    for i in range(1, n_sub):
        nxt = project(i)
        finish(i - 1, pending)
        pending = nxt
    finish(n_sub - 1, pending)


def _qkv(x, pre, mods, mod_row_fn, w, gains, cos_t, sin_t, *, n_q_pairs, rope, tm, seq):
    rows, d = x.shape
    hd = HEAD_DIM
    n = w.shape[-1]
    kern = functools.partial(_qkv_kernel, n_q_pairs=n_q_pairs, rope=rope, sub_rows=min(tm, 256))
    vec = pl.BlockSpec((1, d), lambda i: (0, 0))
    tiles_per_seq = seq // tm
    pos = pl.BlockSpec((tm, hd), lambda i: (i % tiles_per_seq, 0))
    widths = ([2 * n_q_pairs * hd] if n_q_pairs else []) + [N_KV_HEADS * hd] * 2
    return pl.pallas_call(
        kern,
        grid=(rows // tm,),
        in_specs=[
            pl.BlockSpec((tm, d), lambda i: (i, 0)),
            vec,
            _mod_block(d, 1, mod_row_fn),
            _mod_block(d, 0, mod_row_fn),
            _const_spec((d, n)),
            pl.BlockSpec((4, hd), lambda i: (0, 0)),
            pos, pos,
        ],
        out_specs=[pl.BlockSpec((tm, wd), lambda i: (i, 0)) for wd in widths],
        out_shape=[jax.ShapeDtypeStruct((rows, wd), BF16) for wd in widths],
        compiler_params=_cparams("arbitrary"),
    )(x, pre.reshape(1, d), mods, mods, w, gains, cos_t, sin_t)


def _attn_kernel(q_ref, kl_ref, vl_ref, kc_ref, vc_ref, x_ref, wo_ref, gain_ref, g_ref, o_ref, k_all, v_ext,
                 *, n_pairs, kv_blocks):
    hd = HEAD_DIM
    tq = q_ref.shape[1]
    seq = kl_ref.shape[1]
    total = k_all.shape[0]

    @pl.when(pl.program_id(1) == 0)
    def _():
        k_all[0:seq, :] = kl_ref[0]
        k_all[seq:total, :] = kc_ref[0]
        for h in range(N_KV_HEADS):
            v_ext[h, 0:seq, 0:hd] = vl_ref[0, :, h * hd:(h + 1) * hd]
            v_ext[h, seq:total, 0:hd] = vc_ref[0, :, h * hd:(h + 1) * hd]
            v_ext[h, :, hd:2 * hd] = jnp.ones((total, hd), BF16)

    nt = (((1,), (1,)), ((), ()))
    lane_half = (lax.broadcasted_iota(jnp.int32, (1, 2 * hd), 1) % hd) // (hd // 2)

    def start(kv_head):
        keep = (lane_half == kv_head).astype(BF16)
        q = jnp.concatenate([q_ref[0, :, 2 * p * hd:(2 * p + 2) * hd] * keep for p in range(n_pairs)], axis=0)
        st = dict(h=kv_head, q=q, m=None, acc=None)
        st["s_next"] = lax.dot_general(q, k_all[kv_blocks[0][0]:kv_blocks[0][1], :], nt, preferred_element_type=F32)
        return st

    def step(st, i):
        s0, s1 = kv_blocks[i]
        s = st["s_next"]
        if i + 1 < len(kv_blocks):
            n0, n1 = kv_blocks[i + 1]
            st["s_next"] = lax.dot_general(st["q"], k_all[n0:n1, :], nt, preferred_element_type=F32)
        m, acc = st["m"], st["acc"]
        m_blk = jnp.max(s, axis=-1, keepdims=True)
        m_new = m_blk if m is None else jnp.maximum(m, m_blk)
        pv = _dot(jnp.exp2(s - m_new).astype(BF16), v_ext[st["h"], s0:s1, :])
        st["acc"] = pv if m is None else acc * jnp.exp2(m - m_new) + pv
        st["m"] = m_new

    def heads_of(st):
        acc = st["acc"]
        o = (acc[:, 0:hd] / acc[:, hd:2 * hd]).astype(BF16)
        return [o[p * tq:(p + 1) * tq, :] for p in range(n_pairs)]

    last = len(kv_blocks) - 1
    first, second = start(0), None
    for i in range(last):
        step(first, i)
    second = start(1)
    step(first, last)
    for i in range(last + 1):
        step(second, i)
    attn = jnp.concatenate(heads_of(first) + heads_of(second), axis=1)
    y = _dot(attn, wo_ref[...])
    o_ref[0] = x_ref[0] + g_ref[0, 0] * (_rms(y) * gain_ref[...])


def _kv_blocks(total, pattern):
    assert total % MXU_WIDTH == 0
    n_tiles = total // MXU_WIDTH
    blocks, start, i = [], 0, 0
    while start < n_tiles:
        size = min(pattern[i % len(pattern)], n_tiles - start)
        blocks.append((start * MXU_WIDTH, (start + size) * MXU_WIDTH))
        start += size
        i += 1
    return tuple(blocks)


def _attention(q, k_l, v_l, k_c, v_c, x, w_o, gain, mods, gate_chunk, mod_row_fn, *, tq=512, kv_pattern=(5, 3, 1)):
    bsz, seq, dq = q.shape
    d = x.shape[-1]
    ctx_len = k_c.shape[1]
    hd = HEAD_DIM
    kv = N_KV_HEADS * hd
    n_pairs = dq // (2 * hd)
    total = seq + ctx_len
    kern = functools.partial(_attn_kernel, n_pairs=n_pairs, kv_blocks=_kv_blocks(total, kv_pattern))
    whole = lambda rows, width: pl.BlockSpec((1, rows, width), lambda b, i: (b, 0, 0))
    tile = lambda width: pl.BlockSpec((1, tq, width), lambda b, i: (b, i, 0))
    return pl.pallas_call(
        kern,
        grid=(bsz, seq // tq),
        in_specs=[
            tile(dq), whole(seq, 2 * hd), whole(seq, kv), whole(ctx_len, 2 * hd), whole(ctx_len, kv),
            tile(d),
            _const_spec((dq, d)),
            pl.BlockSpec((1, d), lambda b, i: (0, 0)),
            _mod_block(d, gate_chunk, lambda b, i: mod_row_fn(b)),
        ],
        out_specs=tile(d),
        out_shape=jax.ShapeDtypeStruct((bsz, seq, d), F32),
        scratch_shapes=[pltpu.VMEM((total, 2 * hd), BF16), pltpu.VMEM((N_KV_HEADS, total, 2 * hd), BF16)],
        compiler_params=_cparams("arbitrary", "arbitrary"),
    )(q, k_l, v_l, k_c, v_c, x, w_o, gain.reshape(1, d), mods)


def _rope_tables(seq):
    rows = seq // GRID_W
    pairs = HEAD_DIM // 4
    row = jnp.repeat(jnp.arange(rows, dtype=F32), GRID_W)
    col = jnp.tile(jnp.arange(GRID_W, dtype=F32), rows)
    inv = ROPE_THETA ** (-jnp.arange(pairs, dtype=F32) / pairs)
    ang = jnp.concatenate([row[:, None] * inv[None, :], col[:, None] * inv[None, :]], axis=-1)
    cos, sin = jnp.cos(ang), jnp.sin(ang)
    return jnp.concatenate([cos, cos], axis=-1), jnp.concatenate([sin, sin], axis=-1)


def _pair_columns(head_a, head_b):
    hd = HEAD_DIM
    even, odd = jnp.arange(0, hd, 2), jnp.arange(1, hd, 2)
    return jnp.concatenate([head_a * hd + even, head_b * hd + even, head_a * hd + odd, head_b * hd + odd])


def kernel(x, c, ctx, c_ctx, mod_w, mod_b, mix_norm_pre, mix_norm_post, mlp_norm_pre, mlp_norm_post, mlp_w1, mlp_w2, hy_w_in, hy_b_in, hy_conv_w, hy_conv_b, hy_filt_w1, hy_filt_b1, hy_filt_freq1, hy_filt_w2, hy_filt_b2, hy_filt_freq2, hy_filt_w3, hy_filt_bias, hy_w_out, hy_b_out, attn_w_qkv, attn_q_norm, attn_k_norm, attn_w_o):
    bsz, seq, d = x.shape
    ctx_len = ctx.shape[1]
    hd = HEAD_DIM
    n_heads = d // hd
    tm = 1024
    tmc = ctx_len

    ctx_row = bsz
    n_rows = -(-(bsz + 1) // MOD_ROWS_PAD) * MOD_ROWS_PAD
    cond = jnp.concatenate([c, c_ctx[None, :], jnp.zeros((n_rows - bsz - 1, d), F32)], axis=0)
    mods_all = _modulation(cond, mod_w, mod_b).reshape(mod_w.shape[0], n_rows, 6, 1, d)

    x_row = lambda i: i // (seq // tm)
    c_row = lambda i: ctx_row

    xf = x.reshape(bsz * seq, d)
    cf = ctx.reshape(bsz * ctx_len, d)

    mods = mods_all[0]
    w_in = hy_w_in[0].astype(BF16)
    w_out = hy_w_out[0].astype(BF16)
    filt = (hy_filt_w1[0], hy_filt_b1[0], hy_filt_freq1[0], hy_filt_w2[0], hy_filt_b2[0], hy_filt_freq2[0],
            hy_filt_w3[0], hy_filt_bias[0])
    w1 = mlp_w1[0].astype(BF16)
    w2 = mlp_w2[0].astype(BF16)

    def hyena_conv(tokens, mod_row_b):
        u, x1 = _hyena_in(tokens, mix_norm_pre[0], mods, mod_row_b, w_in, hy_b_in[0], hy_conv_w[0], hy_conv_b[0])
        return _fftconv(u, x1, _hyena_filter_tables(tokens.shape[1], *filt))

    x3 = xf.reshape(bsz, seq, d)
    xf = _hyena_out_mlp(hyena_conv(x3, lambda b: b), x3, w_out, hy_b_out[0], mix_norm_post[0],
                        mlp_norm_pre[0], mlp_norm_post[0], mods, lambda b: b, w1, w2).reshape(bsz * seq, d)
    c3 = cf.reshape(bsz, ctx_len, d)
    cf = _proj_res_parity(hyena_conv(c3, lambda b: ctx_row), c3, w_out, hy_b_out[0], mix_norm_post[0], mods, 2,
                          lambda b: ctx_row, th=512).reshape(bsz * ctx_len, d)
    cf = _mlp(cf, mlp_norm_pre[0], mlp_norm_post[0], mods, c_row, w1, w2, tm=min(tm, bsz * ctx_len))

    mods = mods_all[1]
    w_qkv = attn_w_qkv[0]
    assert N_KV_HEADS == 2 and n_heads % 2 == 0
    group = n_heads // N_KV_HEADS
    pair_cols = [_pair_columns(p, group + p) for p in range(group)] + [_pair_columns(n_heads, n_heads + 1)]
    v_cols = jnp.arange((n_heads + N_KV_HEADS) * hd, (n_heads + 2 * N_KV_HEADS) * hd)
    w_lat = w_qkv[:, jnp.concatenate(pair_cols + [v_cols])].astype(BF16)
    w_ctx = w_lat[:, n_heads * hd:]
    even, odd = jnp.arange(0, hd, 2), jnp.arange(1, hd, 2)
    gains = jnp.stack([jnp.tile(g[idx], 2) for g in (attn_q_norm[0], attn_k_norm[0]) for idx in (even, odd)])
    cos_t, sin_t = _rope_tables(seq)

    q, k_l, v_l = _qkv(xf, mix_norm_pre[1], mods, x_row, w_lat, gains, cos_t, sin_t,
                       n_q_pairs=group, rope=True, tm=tm, seq=seq)
    k_c, v_c = _qkv(cf, mix_norm_pre[1], mods, c_row, w_ctx, gains, cos_t, sin_t,
                    n_q_pairs=0, rope=False, tm=tmc, seq=tmc)
    kv = N_KV_HEADS * hd
    xf = _attention(q.reshape(bsz, seq, d), k_l.reshape(bsz, seq, kv), v_l.reshape(bsz, seq, kv),
                    k_c.reshape(bsz, ctx_len, kv), v_c.reshape(bsz, ctx_len, kv), xf.reshape(bsz, seq, d),
                    attn_w_o[0].astype(BF16), mix_norm_post[1], mods, 2, lambda b: b).reshape(bsz * seq, d)
    xf = _mlp(xf, mlp_norm_pre[1], mlp_norm_post[1], mods, x_row, mlp_w1[1].astype(BF16),
              mlp_w2[1].astype(BF16), tm=tm)
    return xf.reshape(bsz, seq, d)
```

```python
import functools
import math

import jax
import jax.numpy as jnp
from jax import lax
from jax.experimental import pallas as pl
from jax.experimental.pallas import tpu as pltpu

F32 = jnp.float32
BF16 = jnp.bfloat16

EPS = 1e-6
GRID_W = 64
HY_BANDS = 16
HY_DECAY_TARGET = 1e-2
HY_FAST = 0.3
HY_SLOW = 1.5
HY_SHIFT = 0.0
HEAD_DIM = 128
N_KV_HEADS = 2
ROPE_THETA = 10000.0
ATTN_SCALE = HEAD_DIM ** -0.5
LOG2_E = math.log2(math.e)

MOD_ROWS_PAD = 8
SUBLANES = 8
LANES = 128
MXU_WIDTH = 256

VMEM_LIMIT = 56 * 1024 * 1024


def _cparams(*sem):
    return pltpu.CompilerParams(dimension_semantics=sem, vmem_limit_bytes=VMEM_LIMIT)


def _const_spec(shape):
    nd = len(shape)
    return pl.BlockSpec(shape, lambda *_: (0,) * nd, pipeline_mode=pl.Buffered(1))


def _rms(x):
    return x * lax.rsqrt(jnp.mean(x * x, axis=-1, keepdims=True) + EPS)


def _norm_mod(x, gain, sc, sh):
    return _rms(x) * (gain * (1.0 + sc)) + sh


def _dot(a, b):
    return jnp.dot(a, b, preferred_element_type=F32)


def _dot_hi(a, b):
    return jnp.dot(a, b, preferred_element_type=F32, precision=lax.Precision.HIGHEST)


def _mod_kernel(cond_ref, w_ref, b_ref, o_ref):
    s = cond_ref[...]
    s = s * jax.nn.sigmoid(s)
    o_ref[0] = _dot(s.astype(BF16), w_ref[0].astype(BF16)) + b_ref[0]


def _modulation(cond, mod_w, mod_b):
    depth, d, n = mod_w.shape
    rows = cond.shape[0]
    tn = d
    return pl.pallas_call(
        _mod_kernel,
        grid=(depth, n // tn),
        in_specs=[
            pl.BlockSpec((rows, d), lambda i, j: (0, 0)),
            pl.BlockSpec((1, d, tn), lambda i, j: (i, 0, j)),
            pl.BlockSpec((1, 1, tn), lambda i, j: (i, 0, j)),
        ],
        out_specs=pl.BlockSpec((1, rows, tn), lambda i, j: (i, 0, j)),
        out_shape=jax.ShapeDtypeStruct((depth, rows, n), F32),
        compiler_params=_cparams("arbitrary", "arbitrary"),
    )(cond, mod_w, mod_b.reshape(depth, 1, n))


def _mod_block(d, chunk, row_fn):
    return pl.BlockSpec((1, 1, 1, d), lambda *idx: (row_fn(*idx), chunk, 0, 0))


def _hyena_in_kernel(x_ref, gain_ref, sc_ref, sh_ref, w0_ref, w1_ref, w2_ref, b_ref, cw_ref, cb_ref,
                     u_ref, x1_ref, h_buf, x_slabs, *, seq, rc):
    j = pl.program_id(1)
    w = u_ref.shape[-1]
    half = seq // 2
    n_chunks = half // rc
    n_slabs = x_slabs.shape[0]

    @pl.when(j == 0)
    def _():
        mult = gain_ref[...] * (1.0 + sc_ref[0, 0])
        sh = sh_ref[0, 0]

        def body(c, carry):
            base = pl.multiple_of(2 * c * rc, 2 * rc)
            for k in range(n_slabs):
                x_slabs[k] = x_ref[0, pl.ds(base, 2 * rc), k * LANES:(k + 1) * LANES]
            for parity in range(2):
                xs = jnp.concatenate([x_slabs[k, pl.ds(parity, rc, stride=2), :] for k in range(n_slabs)], axis=1)
                h_buf[pl.ds(base + parity * rc, rc), :] = (_rms(xs) * mult + sh).astype(BF16)
            return carry

        lax.fori_loop(0, n_chunks, body, 0)

    sub = lax.broadcasted_iota(jnp.int32, (SUBLANES, w), 0)
    w_refs = (w0_ref, w1_ref, w2_ref)
    taps = [[cw_ref[k, s:s + 1, :] for s in range(3)] for k in range(3)]
    bias = [cb_ref[s:s + 1, :] + b_ref[s:s + 1, :] * (taps[0][s] + taps[1][s] + taps[2][s]) for s in range(3)]

    def project(c):
        h = h_buf[2 * c * rc:2 * (c + 1) * rc, :]
        return [_dot(h, w_refs[s][...]) for s in range(3)]

    def conv(c, m_before, m_here, m_after):
        even_streams, odd_streams = [], []
        for s in range(3):
            m_even, m_odd = m_here[s][0:rc], m_here[s][rc:2 * rc]
            row_before = -b_ref[s:s + 1, :] if m_before is None else m_before[s][2 * rc - 1:2 * rc, :]
            row_after = -b_ref[s:s + 1, :] if m_after is None else m_after[s][0:1, :]
            odd_prev = pltpu.roll(m_odd, 1, 0)
            odd_prev = jnp.concatenate([jnp.where(sub == 0, row_before, odd_prev[0:SUBLANES]),
                                        odd_prev[SUBLANES:]], axis=0)
            even_next = pltpu.roll(m_even, rc - 1, 0)
            even_next = jnp.concatenate([even_next[:rc - SUBLANES],
                                         jnp.where(sub == SUBLANES - 1, row_after, even_next[rc - SUBLANES:])], axis=0)
            even_streams.append(odd_prev * taps[0][s] + m_even * taps[1][s] + m_odd * taps[2][s] + bias[s])
            odd_streams.append(m_even * taps[0][s] + m_odd * taps[1][s] + even_next * taps[2][s] + bias[s])
        for base, (x1, x2, v) in ((0, even_streams), (half, odd_streams)):
            u_ref[0, base + c * rc:base + (c + 1) * rc, :] = (v * x2).astype(BF16)
            x1_ref[0, base + c * rc:base + (c + 1) * rc, :] = x1.astype(BF16)

    zs = [project(0)]
    for c in range(1, n_chunks):
        zs.append(project(c))
        conv(c - 1, zs[c - 2] if c >= 2 else None, zs[c - 1], zs[c])
    conv(n_chunks - 1, zs[n_chunks - 2] if n_chunks >= 2 else None, zs[n_chunks - 1], None)


def _hyena_in(x, gain, mods, mod_row_fn, w_in, b_in, conv_w, conv_b, *, block_elems=2048 * 512, rc=128):
    bsz, seq, d = x.shape
    w = min(d, max(MXU_WIDTH, block_elems // seq // MXU_WIDTH * MXU_WIDTH))
    nj = d // w
    rc = min(rc, seq // 2)
    kern = functools.partial(_hyena_in_kernel, seq=seq, rc=rc)
    wspec = lambda s: pl.BlockSpec((d, w), lambda b, j: (0, s * nj + j))
    out_spec = pl.BlockSpec((1, seq, w), lambda b, j: (b, 0, j))
    return pl.pallas_call(
        kern,
        grid=(bsz, nj),
        in_specs=[
            pl.BlockSpec((1, seq, d), lambda b, j: (b, 0, 0)),
            pl.BlockSpec((1, d), lambda b, j: (0, 0)),
            _mod_block(d, 1, lambda b, j: mod_row_fn(b)),
            _mod_block(d, 0, lambda b, j: mod_row_fn(b)),
            wspec(0), wspec(1), wspec(2),
            pl.BlockSpec((3, w), lambda b, j: (0, j)),
            pl.BlockSpec((3, 3, w), lambda b, j: (0, 0, j)),
            pl.BlockSpec((3, w), lambda b, j: (0, j)),
        ],
        out_specs=[out_spec, out_spec],
        out_shape=[jax.ShapeDtypeStruct((bsz, seq, d), BF16)] * 2,
        scratch_shapes=[pltpu.VMEM((seq, d), BF16), pltpu.VMEM((d // LANES, 2 * rc, LANES), F32)],
        compiler_params=_cparams("arbitrary", "arbitrary"),
    )(x, gain.reshape(1, d), mods, mods, w_in, w_in, w_in,
      b_in.reshape(3, d), conv_w.reshape(3, 3, d), conv_b.reshape(3, d))


def _dft_table(length, n_freq, shifted, kb=32):
    if shifted:
        period = 8 * length
        nn = 2 * jnp.arange(length, dtype=jnp.int32)[None, :] + 1
    else:
        period = 4 * length
        nn = jnp.arange(length, dtype=jnp.int32)[None, :]
    kh = jnp.arange(n_freq // kb, dtype=jnp.int32)[:, None]
    kl = jnp.arange(kb, dtype=jnp.int32)[:, None]
    to_angle = lambda m: (m % period).astype(F32) * (2.0 * math.pi / period)
    alpha = to_angle(2 * kb * kh * nn)[:, None, :]
    beta = to_angle((2 * kl + 1) * nn)[None, :, :]
    ca, sa, cb, sb = jnp.cos(alpha), jnp.sin(alpha), jnp.cos(beta), jnp.sin(beta)
    cos_t = (ca * cb - sa * sb).reshape(n_freq, length)
    sin_t = (sa * cb + ca * sb).reshape(n_freq, length)
    return jnp.concatenate([cos_t, sin_t], axis=0).astype(BF16)


def _filter_kernel(bands_ref, w1t_ref, w1c_ref, w1s_ref, b1_ref, f1_ref, w2_ref, b2_ref, f2_ref,
                   w3f_ref, w3b_ref, deltas_ref, bias_ref, psi_ref, tab_ref, o_ref, h_buf, *, seq):
    w = o_ref.shape[-1]
    half = seq // 2

    @pl.when(pl.program_id(0) == 0)
    def _():
        t64 = lax.broadcasted_iota(jnp.int32, (seq, w1t_ref.shape[-1]), 0).astype(F32) / seq
        t16 = lax.broadcasted_iota(jnp.int32, (seq, HY_BANDS), 0).astype(F32) / seq
        ang = (2.0 * math.pi * t16) * bands_ref[...]
        pre = t64 * w1t_ref[...] + _dot_hi(jnp.cos(ang), w1c_ref[...]) + _dot_hi(jnp.sin(ang), w1s_ref[...])
        h = jnp.sin(f1_ref[...] * (pre + b1_ref[...]))
        h_buf[...] = jnp.sin(f2_ref[...] * (_dot_hi(h, w2_ref[...]) + b2_ref[...]))

    h = h_buf[...]
    row = lax.broadcasted_iota(jnp.int32, (seq, w), 0)
    decay = jnp.exp(-(row.astype(F32) / seq) * deltas_ref[...]) + HY_SHIFT
    h_f = _dot_hi(h, w3f_ref[...]) * decay
    h_b = _dot_hi(h, w3b_ref[...]) * decay
    h_b = jnp.where(row == 0, 0.0, h_b)
    h_sum = h_f + h_b
    h_dif = h_b - h_f
    alt = jnp.where(row % 2 == 0, 1.0, -1.0)
    cos_lo = tab_ref[pl.ds(0, half), :]
    sin_lo = tab_ref[pl.ds(half, half), :]
    bias = bias_ref[...]
    p_lo = _dot(cos_lo, h_sum.astype(BF16)) + bias
    p_hi = _dot(cos_lo, (h_sum * alt).astype(BF16)) + bias
    q_lo = _dot(sin_lo, h_dif.astype(BF16))
    q_hi = -_dot(sin_lo, (h_dif * alt).astype(BF16))
    d_re = p_lo - p_hi
    d_im = q_lo + q_hi
    psi_c = jnp.concatenate([psi_ref[0]] * (w // psi_ref.shape[-1]), axis=1)
    psi_s = jnp.concatenate([psi_ref[1]] * (w // psi_ref.shape[-1]), axis=1)
    o_ref[0] = p_lo + p_hi
    o_ref[1] = q_lo - q_hi
    o_ref[2] = d_re * psi_c - d_im * psi_s
    o_ref[3] = d_re * psi_s + d_im * psi_c
    o_ref[4] = d_re * psi_c + d_im * psi_s
    o_ref[5] = d_im * psi_c - d_re * psi_s


def _hyena_filter_tables(seq, fw1, fb1, ff1, fw2, fb2, ff2, fw3, fbias, *, w=256):
    d = fbias.shape[-1]
    fwid = fw2.shape[0]
    half = seq // 2
    lanes = LANES
    bands = jnp.linspace(1e-4, HY_BANDS - 1, HY_BANDS, dtype=F32).reshape(1, HY_BANDS)
    deltas = jnp.abs(jnp.linspace(math.log(HY_DECAY_TARGET) / HY_SLOW, math.log(HY_DECAY_TARGET) / HY_FAST,
                                  d, dtype=F32)).reshape(1, d)
    theta = (2 * jnp.arange(half, dtype=F32) + 1.0) * (math.pi / (2 * seq))
    psi = jnp.broadcast_to(jnp.stack([jnp.cos(theta), jnp.sin(theta)])[:, :, None], (2, half, lanes))
    tab_plain = _dft_table(seq, half, shifted=False)
    nj = d // w
    small = lambda shape: pl.BlockSpec(shape, lambda j: (0,) * len(shape))
    kern = functools.partial(_filter_kernel, seq=seq)
    return pl.pallas_call(
        kern,
        grid=(nj,),
        in_specs=[
            small((1, HY_BANDS)), small((1, fwid)), small((HY_BANDS, fwid)), small((HY_BANDS, fwid)),
            small((1, fwid)), small((1, fwid)), small((fwid, fwid)), small((1, fwid)), small((1, fwid)),
            pl.BlockSpec((fwid, w), lambda j: (0, j)),
            pl.BlockSpec((fwid, w), lambda j: (0, nj + j)),
            pl.BlockSpec((1, w), lambda j: (0, j)),
            pl.BlockSpec((1, w), lambda j: (0, j)),
            small((2, half, lanes)),
            _const_spec((2 * half, seq)),
        ],
        out_specs=pl.BlockSpec((6, half, w), lambda j: (0, 0, j)),
        out_shape=jax.ShapeDtypeStruct((6, half, d), F32),
        scratch_shapes=[pltpu.VMEM((seq, fwid), F32)],
        compiler_params=_cparams("arbitrary"),
    )(bands, fw1[0:1], fw1[1:1 + HY_BANDS], fw1[1 + HY_BANDS:], fb1.reshape(1, fwid), ff1.reshape(1, fwid),
      fw2, fb2.reshape(1, fwid), ff2.reshape(1, fwid), fw3, fw3, deltas, fbias.reshape(1, d), psi, tab_plain)


def _fftconv_kernel(u_ref, x1_ref, sef_ref, tab_ref, o_ref, *, seq):
    half = seq // 2
    cos_t = tab_ref[pl.ds(0, half), :]
    sin_t = tab_ref[pl.ds(half, half), :]
    u0 = u_ref[0, 0:half, :]
    u1 = u_ref[0, half:seq, :]
    a0, b0 = _dot(cos_t, u0), _dot(sin_t, u0)
    a1, b1 = _dot(cos_t, u1), _dot(sin_t, u1)
    s_re, s_im, e_re, e_im, f_re, f_im = (sef_ref[i] for i in range(6))
    v0_re = (s_re * a0 + s_im * b0 + f_re * a1 + f_im * b1).astype(BF16)
    v0_im = (s_im * a0 - s_re * b0 + f_im * a1 - f_re * b1).astype(BF16)
    v1_re = (e_re * a0 + e_im * b0 + s_re * a1 + s_im * b1).astype(BF16)
    v1_im = (e_im * a0 - e_re * b0 + s_im * a1 - s_re * b1).astype(BF16)
    y0 = _dot(cos_t, v0_re) - _dot(sin_t, v0_im)
    y1 = _dot(cos_t, v1_re) - _dot(sin_t, v1_im)
    o_ref[0, 0:half, :] = (y0 * (1.0 / seq) * x1_ref[0, 0:half, :].astype(F32)).astype(BF16)
    o_ref[0, half:seq, :] = (y1 * (1.0 / seq) * x1_ref[0, half:seq, :].astype(F32)).astype(BF16)


def _fftconv(u, x1, sef, *, block_elems=2048 * 256):
    bsz, seq, d = u.shape
    half = seq // 2
    w = min(d, max(MXU_WIDTH, block_elems // seq // MXU_WIDTH * MXU_WIDTH))
    nj = d // w
    act = pl.BlockSpec((1, seq, w), lambda j, b: (b, 0, j))
    kern = functools.partial(_fftconv_kernel, seq=seq)
    return pl.pallas_call(
        kern,
        grid=(nj, bsz),
        in_specs=[act, act, pl.BlockSpec((6, half, w), lambda j, b: (0, 0, j)), _const_spec((2 * half, half))],
        out_specs=act,
        out_shape=jax.ShapeDtypeStruct((bsz, seq, d), BF16),
        compiler_params=_cparams("arbitrary", "arbitrary"),
    )(u, x1, sef, _dft_table(half, half, shifted=True))


def _proj_res_parity_kernel(a_ref, x_ref, w_ref, b_ref, gain_ref, g_ref, o_ref, x_slabs, o_slabs):
    th = a_ref.shape[2]
    n_slabs = x_slabs.shape[0]
    for k in range(n_slabs):
        x_slabs[k] = x_ref[0, :, k * LANES:(k + 1) * LANES]
    for parity in range(2):
        rows = pl.ds(parity, th, stride=2)
        y = _dot(a_ref[0, parity], w_ref[...]) + b_ref[...]
        x_par = jnp.concatenate([x_slabs[k, rows, :] for k in range(n_slabs)], axis=1)
        out = x_par + g_ref[0, 0] * (_rms(y) * gain_ref[...])
        for k in range(n_slabs):
            o_slabs[k, rows, :] = out[:, k * LANES:(k + 1) * LANES]
    for k in range(n_slabs):
        o_ref[0, :, k * LANES:(k + 1) * LANES] = o_slabs[k]


def _proj_res_parity(a, x, w, b, gain, mods, gate_chunk, mod_row_fn, *, th):
    bsz, seq, d = x.shape
    half = seq // 2
    th = min(th, half)
    return pl.pallas_call(
        _proj_res_parity_kernel,
        grid=(bsz, half // th),
        in_specs=[
            pl.BlockSpec((1, 2, th, d), lambda b, i: (b, 0, i, 0)),
            pl.BlockSpec((1, 2 * th, d), lambda b, i: (b, i, 0)),
            _const_spec((d, d)),
            pl.BlockSpec((1, d), lambda b, i: (0, 0)),
            pl.BlockSpec((1, d), lambda b, i: (0, 0)),
            _mod_block(d, gate_chunk, lambda b, i: mod_row_fn(b)),
        ],
        out_specs=pl.BlockSpec((1, 2 * th, d), lambda b, i: (b, i, 0)),
        out_shape=jax.ShapeDtypeStruct((bsz, seq, d), F32),
        scratch_shapes=[pltpu.VMEM((d // LANES, 2 * th, LANES), F32)] * 2,
        compiler_params=_cparams("arbitrary", "arbitrary"),
    )(a.reshape(bsz, 2, half, d), x, w, b.reshape(1, d), gain.reshape(1, d), mods)


def _mlp_kernel(x_ref, pre_ref, sc_ref, sh_ref, g_ref, w1_ref, w2_ref, post_ref, o_ref, *, ff_chunk, sub_rows):
    d_ff = w1_ref.shape[-1]
    mult = pre_ref[...] * (1.0 + sc_ref[0, 0])
    sh = sh_ref[0, 0]

    def rows_of(i):
        return slice(i * sub_rows, (i + 1) * sub_rows)

    def hidden(i):
        return (_rms(x_ref[rows_of(i), :]) * mult + sh).astype(BF16)

    def mix(h):
        acc = None
        for c in range(d_ff // ff_chunk):
            cols = pl.ds(c * ff_chunk, ff_chunk)
            a = jnp.maximum(_dot(h, w1_ref[:, cols]), 0.0)
            part = _dot((a * a).astype(BF16), w2_ref[cols, :])
            acc = part if acc is None else acc + part
        return acc

    def finish(i, acc):
        o_ref[rows_of(i), :] = x_ref[rows_of(i), :] + g_ref[0, 0] * (_rms(acc) * post_ref[...])

    n_sub = x_ref.shape[0] // sub_rows
    pending = mix(hidden(0))
    for i in range(1, n_sub):
        nxt = mix(hidden(i))
        finish(i - 1, pending)
        pending = nxt
    finish(n_sub - 1, pending)


def _mlp(x, pre, post, mods, mod_row_fn, w1, w2, *, tm, ff_chunk=1024, sub_rows=512):
    rows, d = x.shape
    d_ff = w1.shape[-1]
    kern = functools.partial(_mlp_kernel, ff_chunk=ff_chunk, sub_rows=min(sub_rows, tm))
    vec = pl.BlockSpec((1, d), lambda i: (0, 0))
    return pl.pallas_call(
        kern,
        grid=(rows // tm,),
        in_specs=[
            pl.BlockSpec((tm, d), lambda i: (i, 0)),
            vec,
            _mod_block(d, 4, mod_row_fn),
            _mod_block(d, 3, mod_row_fn),
            _mod_block(d, 5, mod_row_fn),
            _const_spec((d, d_ff)),
            _const_spec((d_ff, d)),
            vec,
        ],
        out_specs=pl.BlockSpec((tm, d), lambda i: (i, 0)),
        out_shape=jax.ShapeDtypeStruct((rows, d), F32),
        compiler_params=_cparams("arbitrary"),
    )(x, pre.reshape(1, d), mods, mods, mods, w1, w2, post.reshape(1, d))


def _hyena_out_mlp_kernel(a_ref, x_ref, wo_ref, bo_ref, mixpost_ref, g1_ref, pre_ref, sc_ref, sh_ref, g2_ref,
                          w1_ref, w2_ref, post_ref, o_ref, slabs, *, ff_chunk):
    th = a_ref.shape[2]
    d_ff = w1_ref.shape[-1]
    n_slabs = slabs.shape[0]
    for k in range(n_slabs):
        slabs[k] = x_ref[0, :, k * LANES:(k + 1) * LANES]
    parity_rows = [pl.ds(parity, th, stride=2) for parity in range(2)]
    x = jnp.concatenate([jnp.concatenate([slabs[k, rows, :] for k in range(n_slabs)], axis=1)
                         for rows in parity_rows], axis=0)
    a = jnp.concatenate([a_ref[0, 0], a_ref[0, 1]], axis=0)
    y = _dot(a, wo_ref[...]) + bo_ref[...]
    x1 = x + g1_ref[0, 0] * (_rms(y) * mixpost_ref[...])
    h = _norm_mod(x1, pre_ref[...], sc_ref[0, 0], sh_ref[0, 0]).astype(BF16)
    acc = None
    for c in range(d_ff // ff_chunk):
        cols = pl.ds(c * ff_chunk, ff_chunk)
        t = jnp.maximum(_dot(h, w1_ref[:, cols]), 0.0)
        part = _dot((t * t).astype(BF16), w2_ref[cols, :])
        acc = part if acc is None else acc + part
    out = x1 + g2_ref[0, 0] * (_rms(acc) * post_ref[...])
    for parity, rows in enumerate(parity_rows):
        for k in range(n_slabs):
            slabs[k, rows, :] = out[parity * th:(parity + 1) * th, k * LANES:(k + 1) * LANES]
    for k in range(n_slabs):
        o_ref[0, :, k * LANES:(k + 1) * LANES] = slabs[k]


def _hyena_out_mlp(a, x, w_out, b_out, mix_post, pre, post, mods, mod_row_fn, w1, w2, *, th=256, ff_chunk=1024):
    bsz, seq, d = x.shape
    half = seq // 2
    d_ff = w1.shape[-1]
    th = min(th, half)
    kern = functools.partial(_hyena_out_mlp_kernel, ff_chunk=ff_chunk)
    vec = pl.BlockSpec((1, d), lambda b, i: (0, 0))
    mod = lambda chunk: _mod_block(d, chunk, lambda b, i: mod_row_fn(b))
    rows = pl.BlockSpec((1, 2 * th, d), lambda b, i: (b, i, 0))
    return pl.pallas_call(
        kern,
        grid=(bsz, half // th),
        in_specs=[
            pl.BlockSpec((1, 2, th, d), lambda b, i: (b, 0, i, 0)),
            rows,
            _const_spec((d, d)), vec, vec, mod(2),
            vec, mod(4), mod(3), mod(5),
            _const_spec((d, d_ff)), _const_spec((d_ff, d)), vec,
        ],
        out_specs=rows,
        out_shape=jax.ShapeDtypeStruct((bsz, seq, d), F32),
        scratch_shapes=[pltpu.VMEM((d // LANES, 2 * th, LANES), F32)],
        compiler_params=_cparams("arbitrary", "arbitrary"),
    )(a.reshape(bsz, 2, half, d), x, w_out, b_out.reshape(1, d), mix_post.reshape(1, d), mods,
      pre.reshape(1, d), mods, mods, mods, w1, w2, post.reshape(1, d))


def _qkv_kernel(x_ref, pre_ref, sc_ref, sh_ref, w_ref, gains_ref, cos_ref, sin_ref,
                *out_refs, n_q_pairs, rope, sub_rows):
    q_ref = out_refs[0] if n_q_pairs else None
    k_ref, v_ref = out_refs[-2:]
    hd = HEAD_DIM
    tm = x_ref.shape[0]
    mult = pre_ref[...] * (1.0 + sc_ref[0, 0])
    sh = sh_ref[0, 0]
    q_scale = ATTN_SCALE * LOG2_E
    gain_ab = [(gains_ref[0:1, :] * q_scale, gains_ref[1:2, :] * q_scale), (gains_ref[2:3, :], gains_ref[3:4, :])]
    half_r = lax.broadcasted_iota(jnp.int32, (2 * hd, 2 * hd), 0) // (hd // 2)
    half_c = lax.broadcasted_iota(jnp.int32, (2 * hd, 2 * hd), 1) // (hd // 2)
    same_head = (half_r == half_c).astype(BF16)

    def rows_of(i):
        return slice(i * sub_rows, (i + 1) * sub_rows)

    def project(i):
        h = (_rms(x_ref[rows_of(i), :]) * mult + sh).astype(BF16)
        return _dot(h, w_ref[...])

    def finish(i, qkv):
        rows = rows_of(i)
        if rope:
            cos_t, sin_t = cos_ref[rows, :], sin_ref[rows, :]
            tabs = [(ga * cos_t, gb * sin_t, ga * sin_t, gb * cos_t) for ga, gb in gain_ab]

        n_pairs = n_q_pairs + 1
        squares = [(qkv[:, 2 * p * hd:(2 * p + 1) * hd] ** 2 + qkv[:, (2 * p + 1) * hd:(2 * p + 2) * hd] ** 2)
                   .astype(BF16) for p in range(n_pairs)]
        sums = []
        for p in range(0, n_pairs - 1, 2):
            both = _dot(jnp.concatenate(squares[p:p + 2], axis=1), same_head)
            sums += [both[:, 0:hd], both[:, hd:2 * hd]]
        if n_pairs % 2:
            sums.append(_dot(squares[-1], same_head[0:hd, 0:hd]))

        def pair(idx, kind):
            a = qkv[:, 2 * idx * hd:(2 * idx + 1) * hd]
            b = qkv[:, (2 * idx + 1) * hd:(2 * idx + 2) * hd]
            r = lax.rsqrt(sums[idx] * (1.0 / hd) + EPS)
            if rope:
                ca, sb, sa, cb = tabs[kind]
                return (r * (a * ca - b * sb)).astype(BF16), (r * (a * sa + b * cb)).astype(BF16)
            ga, gb = gain_ab[kind]
            return (r * (a * ga)).astype(BF16), (r * (b * gb)).astype(BF16)

        for p in range(n_q_pairs):
            qa, qb = pair(p, 0)
            q_ref[rows, 2 * p * hd:(2 * p + 1) * hd] = qa
            q_ref[rows, (2 * p + 1) * hd:(2 * p + 2) * hd] = qb
        ka, kb = pair(n_q_pairs, 1)
        k_ref[rows, 0:hd] = ka
        k_ref[rows, hd:2 * hd] = kb
        v0 = 2 * (n_q_pairs + 1) * hd
        v_ref[rows, :] = qkv[:, v0:v0 + N_KV_HEADS * hd].astype(BF16)

    n_sub = tm // sub_rows
    pending = project(0)
    for i in range(1, n_sub):
        nxt = project(i)
        finish(i - 1, pending)
        pending = nxt
    finish(n_sub - 1, pending)


def _qkv(x, pre, mods, mod_row_fn, w, gains, cos_t, sin_t, *, n_q_pairs, rope, tm, seq):
    rows, d = x.shape
    hd = HEAD_DIM
    n = w.shape[-1]
    kern = functools.partial(_qkv_kernel, n_q_pairs=n_q_pairs, rope=rope, sub_rows=min(tm, 256))
    vec = pl.BlockSpec((1, d), lambda i: (0, 0))
    tiles_per_seq = seq // tm
    pos = pl.BlockSpec((tm, hd), lambda i: (i % tiles_per_seq, 0))
    widths = ([2 * n_q_pairs * hd] if n_q_pairs else []) + [N_KV_HEADS * hd] * 2
    return pl.pallas_call(
        kern,
        grid=(rows // tm,),
        in_specs=[
            pl.BlockSpec((tm, d), lambda i: (i, 0)),
            vec,
            _mod_block(d, 1, mod_row_fn),
            _mod_block(d, 0, mod_row_fn),
            _const_spec((d, n)),
            pl.BlockSpec((4, hd), lambda i: (0, 0)),
            pos, pos,
        ],
        out_specs=[pl.BlockSpec((tm, wd), lambda i: (i, 0)) for wd in widths],
        out_shape=[jax.ShapeDtypeStruct((rows, wd), BF16) for wd in widths],
        compiler_params=_cparams("arbitrary"),
    )(x, pre.reshape(1, d), mods, mods, w, gains, cos_t, sin_t)


def _attn_kernel(q_ref, kl_ref, vl_ref, kc_ref, vc_ref, x_ref, wo_ref, gain_ref, g_ref, o_ref, k_all, v_ext,
                 *, n_pairs, kv_blocks):
    hd = HEAD_DIM
    tq = q_ref.shape[1]
    seq = kl_ref.shape[1]
    total = k_all.shape[0]

    @pl.when(pl.program_id(1) == 0)
    def _():
        k_all[0:seq, :] = kl_ref[0]
        k_all[seq:total, :] = kc_ref[0]
        for h in range(N_KV_HEADS):
            v_ext[h, 0:seq, 0:hd] = vl_ref[0, :, h * hd:(h + 1) * hd]
            v_ext[h, seq:total, 0:hd] = vc_ref[0, :, h * hd:(h + 1) * hd]
            v_ext[h, :, hd:2 * hd] = jnp.ones((total, hd), BF16)

    nt = (((1,), (1,)), ((), ()))
    lane_half = (lax.broadcasted_iota(jnp.int32, (1, 2 * hd), 1) % hd) // (hd // 2)

    def start(kv_head):
        keep = (lane_half == kv_head).astype(BF16)
        q = jnp.concatenate([q_ref[0, :, 2 * p * hd:(2 * p + 2) * hd] * keep for p in range(n_pairs)], axis=0)
        st = dict(h=kv_head, q=q, m=None, acc=None)
        st["s_next"] = lax.dot_general(q, k_all[kv_blocks[0][0]:kv_blocks[0][1], :], nt, preferred_element_type=F32)
        return st

    def step(st, i):
        s0, s1 = kv_blocks[i]
        s = st["s_next"]
        if i + 1 < len(kv_blocks):
            n0, n1 = kv_blocks[i + 1]
            st["s_next"] = lax.dot_general(st["q"], k_all[n0:n1, :], nt, preferred_element_type=F32)
        m, acc = st["m"], st["acc"]
        m_blk = jnp.max(s, axis=-1, keepdims=True)
        m_new = m_blk if m is None else jnp.maximum(m, m_blk)
        pv = _dot(jnp.exp2(s - m_new).astype(BF16), v_ext[st["h"], s0:s1, :])
        st["acc"] = pv if m is None else acc * jnp.exp2(m - m_new) + pv
        st["m"] = m_new

    def heads_of(st):
        acc = st["acc"]
        o = (acc[:, 0:hd] / acc[:, hd:2 * hd]).astype(BF16)
        return [o[p * tq:(p + 1) * tq, :] for p in range(n_pairs)]

    last = len(kv_blocks) - 1
    first, second = start(0), None
    for i in range(last):
        step(first, i)
    second = start(1)
    step(first, last)
    for i in range(last + 1):
        step(second, i)
    attn = jnp.concatenate(heads_of(first) + heads_of(second), axis=1)
    y = _dot(attn, wo_ref[...])
    o_ref[0] = x_ref[0] + g_ref[0, 0] * (_rms(y) * gain_ref[...])


def _kv_blocks(total, pattern):
    assert total % MXU_WIDTH == 0
    n_tiles = total // MXU_WIDTH
    blocks, start, i = [], 0, 0
    while start < n_tiles:
        size = min(pattern[i % len(pattern)], n_tiles - start)
        blocks.append((start * MXU_WIDTH, (start + size) * MXU_WIDTH))
        start += size
        i += 1
    return tuple(blocks)


def _attention(q, k_l, v_l, k_c, v_c, x, w_o, gain, mods, gate_chunk, mod_row_fn, *, tq=512, kv_pattern=(5, 3, 1)):
    bsz, seq, dq = q.shape
    d = x.shape[-1]
    ctx_len = k_c.shape[1]
    hd = HEAD_DIM
    kv = N_KV_HEADS * hd
    n_pairs = dq // (2 * hd)
    total = seq + ctx_len
    kern = functools.partial(_attn_kernel, n_pairs=n_pairs, kv_blocks=_kv_blocks(total, kv_pattern))
    whole = lambda rows, width: pl.BlockSpec((1, rows, width), lambda b, i: (b, 0, 0))
    tile = lambda width: pl.BlockSpec((1, tq, width), lambda b, i: (b, i, 0))
    return pl.pallas_call(
        kern,
        grid=(bsz, seq // tq),
        in_specs=[
            tile(dq), whole(seq, 2 * hd), whole(seq, kv), whole(ctx_len, 2 * hd), whole(ctx_len, kv),
            tile(d),
            _const_spec((dq, d)),
            pl.BlockSpec((1, d), lambda b, i: (0, 0)),
            _mod_block(d, gate_chunk, lambda b, i: mod_row_fn(b)),
        ],
        out_specs=tile(d),
        out_shape=jax.ShapeDtypeStruct((bsz, seq, d), F32),
        scratch_shapes=[pltpu.VMEM((total, 2 * hd), BF16), pltpu.VMEM((N_KV_HEADS, total, 2 * hd), BF16)],
        compiler_params=_cparams("arbitrary", "arbitrary"),
    )(q, k_l, v_l, k_c, v_c, x, w_o, gain.reshape(1, d), mods)


def _rope_tables(seq):
    rows = seq // GRID_W
    pairs = HEAD_DIM // 4
    row = jnp.repeat(jnp.arange(rows, dtype=F32), GRID_W)
    col = jnp.tile(jnp.arange(GRID_W, dtype=F32), rows)
    inv = ROPE_THETA ** (-jnp.arange(pairs, dtype=F32) / pairs)
    ang = jnp.concatenate([row[:, None] * inv[None, :], col[:, None] * inv[None, :]], axis=-1)
    cos, sin = jnp.cos(ang), jnp.sin(ang)
    return jnp.concatenate([cos, cos], axis=-1), jnp.concatenate([sin, sin], axis=-1)


def _pair_columns(head_a, head_b):
    hd = HEAD_DIM
    even, odd = jnp.arange(0, hd, 2), jnp.arange(1, hd, 2)
    return jnp.concatenate([head_a * hd + even, head_b * hd + even, head_a * hd + odd, head_b * hd + odd])


def kernel(x, c, ctx, c_ctx, mod_w, mod_b, mix_norm_pre, mix_norm_post, mlp_norm_pre, mlp_norm_post, mlp_w1, mlp_w2, hy_w_in, hy_b_in, hy_conv_w, hy_conv_b, hy_filt_w1, hy_filt_b1, hy_filt_freq1, hy_filt_w2, hy_filt_b2, hy_filt_freq2, hy_filt_w3, hy_filt_bias, hy_w_out, hy_b_out, attn_w_qkv, attn_q_norm, attn_k_norm, attn_w_o):
    bsz, seq, d = x.shape
    ctx_len = ctx.shape[1]
    hd = HEAD_DIM
    n_heads = d // hd
    tm = 1024

    ctx_row = bsz
    n_rows = -(-(bsz + 1) // MOD_ROWS_PAD) * MOD_ROWS_PAD
    cond = jnp.concatenate([c, c_ctx[None, :], jnp.zeros((n_rows - bsz - 1, d), F32)], axis=0)
    mods_all = _modulation(cond, mod_w, mod_b).reshape(mod_w.shape[0], n_rows, 6, 1, d)

    x_row = lambda i: i // (seq // tm)
    c_row = lambda i: ctx_row

    xf = x.reshape(bsz * seq, d)
    cf = ctx.reshape(bsz * ctx_len, d)

    mods = mods_all[0]
    w_in = hy_w_in[0].astype(BF16)
    w_out = hy_w_out[0].astype(BF16)
    filt = (hy_filt_w1[0], hy_filt_b1[0], hy_filt_freq1[0], hy_filt_w2[0], hy_filt_b2[0], hy_filt_freq2[0],
            hy_filt_w3[0], hy_filt_bias[0])
    w1 = mlp_w1[0].astype(BF16)
    w2 = mlp_w2[0].astype(BF16)

    def hyena_conv(tokens, mod_row_b):
        u, x1 = _hyena_in(tokens, mix_norm_pre[0], mods, mod_row_b, w_in, hy_b_in[0], hy_conv_w[0], hy_conv_b[0])
        return _fftconv(u, x1, _hyena_filter_tables(tokens.shape[1], *filt))

    x3 = xf.reshape(bsz, seq, d)
    xf = _hyena_out_mlp(hyena_conv(x3, lambda b: b), x3, w_out, hy_b_out[0], mix_norm_post[0],
                        mlp_norm_pre[0], mlp_norm_post[0], mods, lambda b: b, w1, w2).reshape(bsz * seq, d)
    c3 = cf.reshape(bsz, ctx_len, d)
    cf = _proj_res_parity(hyena_conv(c3, lambda b: ctx_row), c3, w_out, hy_b_out[0], mix_norm_post[0], mods, 2,
                          lambda b: ctx_row, th=512).reshape(bsz * ctx_len, d)
    cf = _mlp(cf, mlp_norm_pre[0], mlp_norm_post[0], mods, c_row, w1, w2, tm=min(tm, bsz * ctx_len))

    mods = mods_all[1]
    w_qkv = attn_w_qkv[0]
    assert N_KV_HEADS == 2 and n_heads % 2 == 0
    group = n_heads // N_KV_HEADS
    pair_cols = [_pair_columns(p, group + p) for p in range(group)] + [_pair_columns(n_heads, n_heads + 1)]
    v_cols = jnp.arange((n_heads + N_KV_HEADS) * hd, (n_heads + 2 * N_KV_HEADS) * hd)
    w_lat = w_qkv[:, jnp.concatenate(pair_cols + [v_cols])].astype(BF16)
    w_ctx = w_lat[:, n_heads * hd:]
    even, odd = jnp.arange(0, hd, 2), jnp.arange(1, hd, 2)
    gains = jnp.stack([jnp.tile(g[idx], 2) for g in (attn_q_norm[0], attn_k_norm[0]) for idx in (even, odd)])
    cos_t, sin_t = _rope_tables(seq)

    q, k_l, v_l = _qkv(xf, mix_norm_pre[1], mods, x_row, w_lat, gains, cos_t, sin_t,
                       n_q_pairs=group, rope=True, tm=tm, seq=seq)
    k_c, v_c = _qkv(cf, mix_norm_pre[1], mods, c_row, w_ctx, gains, cos_t, sin_t,
                    n_q_pairs=0, rope=False, tm=min(tm, bsz * ctx_len), seq=min(tm, bsz * ctx_len))
    kv = N_KV_HEADS * hd
    xf = _attention(q.reshape(bsz, seq, d), k_l.reshape(bsz, seq, kv), v_l.reshape(bsz, seq, kv),
                    k_c.reshape(bsz, ctx_len, kv), v_c.reshape(bsz, ctx_len, kv), xf.reshape(bsz, seq, d),
                    attn_w_o[0].astype(BF16), mix_norm_post[1], mods, 2, lambda b: b).reshape(bsz * seq, d)
    xf = _mlp(xf, mlp_norm_pre[1], mlp_norm_post[1], mods, x_row, mlp_w1[1].astype(BF16),
              mlp_w2[1].astype(BF16), tm=tm)
    return xf.reshape(bsz, seq, d)
```

```python
import functools
import math

import jax
import jax.numpy as jnp
from jax import lax
from jax.experimental import pallas as pl
from jax.experimental.pallas import tpu as pltpu

F32 = jnp.float32
BF16 = jnp.bfloat16

EPS = 1e-6
GRID_W = 64
HY_BANDS = 16
HY_DECAY_TARGET = 1e-2
HY_FAST = 0.3
HY_SLOW = 1.5
HY_SHIFT = 0.0
HEAD_DIM = 128
N_KV_HEADS = 2
ROPE_THETA = 10000.0
ATTN_SCALE = HEAD_DIM ** -0.5
LOG2_E = math.log2(math.e)

MOD_ROWS_PAD = 8
SUBLANES = 8
LANES = 128
MXU_WIDTH = 256

VMEM_LIMIT = 56 * 1024 * 1024


def _cparams(*sem):
    return pltpu.CompilerParams(dimension_semantics=sem, vmem_limit_bytes=VMEM_LIMIT)


def _const_spec(shape, layer=None):
    nd = len(shape)
    if layer is None:
        return pl.BlockSpec(shape, lambda *_: (0,) * nd, pipeline_mode=pl.Buffered(1))
    return pl.BlockSpec((None,) + tuple(shape), lambda *_: (layer,) + (0,) * nd, pipeline_mode=pl.Buffered(1))


def _cast_kernel(w_ref, o_ref):
    o_ref[...] = w_ref[...].astype(BF16)


def _to_bf16(w, *, block_rows=256):
    depth, r, c = w.shape
    block_rows = min(block_rows, r)
    spec = pl.BlockSpec((1, block_rows, c), lambda l, i: (l, i, 0))
    return pl.pallas_call(
        _cast_kernel,
        grid=(depth, r // block_rows),
        in_specs=[spec],
        out_specs=spec,
        out_shape=jax.ShapeDtypeStruct(w.shape, BF16),
        compiler_params=_cparams("arbitrary", "arbitrary"),
    )(w)


def _rms(x):
    return x * lax.rsqrt(jnp.mean(x * x, axis=-1, keepdims=True) + EPS)


def _norm_mod(x, gain, sc, sh):
    return _rms(x) * (gain * (1.0 + sc)) + sh


def _dot(a, b):
    return jnp.dot(a, b, preferred_element_type=F32)


def _dot_hi(a, b):
    return jnp.dot(a, b, preferred_element_type=F32, precision=lax.Precision.HIGHEST)


def _mod_kernel(cond_ref, w_ref, b_ref, o_ref):
    s = cond_ref[...]
    s = s * jax.nn.sigmoid(s)
    o_ref[0] = _dot(s.astype(BF16), w_ref[0].astype(BF16)) + b_ref[0]


def _modulation(cond, mod_w, mod_b):
    depth, d, n = mod_w.shape
    rows = cond.shape[0]
    tn = d
    return pl.pallas_call(
        _mod_kernel,
        grid=(depth, n // tn),
        in_specs=[
            pl.BlockSpec((rows, d), lambda i, j: (0, 0)),
            pl.BlockSpec((1, d, tn), lambda i, j: (i, 0, j)),
            pl.BlockSpec((1, 1, tn), lambda i, j: (i, 0, j)),
        ],
        out_specs=pl.BlockSpec((1, rows, tn), lambda i, j: (i, 0, j)),
        out_shape=jax.ShapeDtypeStruct((depth, rows, n), F32),
        compiler_params=_cparams("arbitrary", "arbitrary"),
    )(cond, mod_w, mod_b.reshape(depth, 1, n))


def _mod_block(d, chunk, row_fn):
    return pl.BlockSpec((1, 1, 1, d), lambda *idx: (row_fn(*idx), chunk, 0, 0))


def _hyena_in_kernel(x_ref, gain_ref, sc_ref, sh_ref, w0_ref, w1_ref, w2_ref, b_ref, cw_ref, cb_ref,
                     u_ref, x1_ref, h_buf, x_slabs, *, seq, rc):
    j = pl.program_id(1)
    w = u_ref.shape[-1]
    half = seq // 2
    n_chunks = half // rc
    n_slabs = x_slabs.shape[0]

    @pl.when(j == 0)
    def _():
        mult = gain_ref[...] * (1.0 + sc_ref[0, 0])
        sh = sh_ref[0, 0]

        def body(c, carry):
            base = pl.multiple_of(2 * c * rc, 2 * rc)
            for k in range(n_slabs):
                x_slabs[k] = x_ref[0, pl.ds(base, 2 * rc), k * LANES:(k + 1) * LANES]
            for parity in range(2):
                xs = jnp.concatenate([x_slabs[k, pl.ds(parity, rc, stride=2), :] for k in range(n_slabs)], axis=1)
                h_buf[pl.ds(base + parity * rc, rc), :] = (_rms(xs) * mult + sh).astype(BF16)
            return carry

        lax.fori_loop(0, n_chunks, body, 0)

    sub = lax.broadcasted_iota(jnp.int32, (SUBLANES, w), 0)
    w_refs = (w0_ref, w1_ref, w2_ref)
    taps = [[cw_ref[k, s:s + 1, :] for s in range(3)] for k in range(3)]
    bias = [cb_ref[s:s + 1, :] + b_ref[s:s + 1, :] * (taps[0][s] + taps[1][s] + taps[2][s]) for s in range(3)]

    def project(c):
        h = h_buf[2 * c * rc:2 * (c + 1) * rc, :]
        return [_dot(h, w_refs[s][...]) for s in range(3)]

    def conv(c, m_before, m_here, m_after):
        even_streams, odd_streams = [], []
        for s in range(3):
            m_even, m_odd = m_here[s][0:rc], m_here[s][rc:2 * rc]
            row_before = -b_ref[s:s + 1, :] if m_before is None else m_before[s][2 * rc - 1:2 * rc, :]
            row_after = -b_ref[s:s + 1, :] if m_after is None else m_after[s][0:1, :]
            odd_prev = pltpu.roll(m_odd, 1, 0)
            odd_prev = jnp.concatenate([jnp.where(sub == 0, row_before, odd_prev[0:SUBLANES]),
                                        odd_prev[SUBLANES:]], axis=0)
            even_next = pltpu.roll(m_even, rc - 1, 0)
            even_next = jnp.concatenate([even_next[:rc - SUBLANES],
                                         jnp.where(sub == SUBLANES - 1, row_after, even_next[rc - SUBLANES:])], axis=0)
            even_streams.append(odd_prev * taps[0][s] + m_even * taps[1][s] + m_odd * taps[2][s] + bias[s])
            odd_streams.append(m_even * taps[0][s] + m_odd * taps[1][s] + even_next * taps[2][s] + bias[s])
        for base, (x1, x2, v) in ((0, even_streams), (half, odd_streams)):
            u_ref[0, base + c * rc:base + (c + 1) * rc, :] = (v * x2).astype(BF16)
            x1_ref[0, base + c * rc:base + (c + 1) * rc, :] = x1.astype(BF16)

    zs = [project(0)]
    for c in range(1, n_chunks):
        zs.append(project(c))
        conv(c - 1, zs[c - 2] if c >= 2 else None, zs[c - 1], zs[c])
    conv(n_chunks - 1, zs[n_chunks - 2] if n_chunks >= 2 else None, zs[n_chunks - 1], None)


def _hyena_in(x, gain, mods, mod_row_fn, w_in, b_in, conv_w, conv_b, *, block_elems=2048 * 512, rc=128):
    bsz, seq, d = x.shape
    w = min(d, max(MXU_WIDTH, block_elems // seq // MXU_WIDTH * MXU_WIDTH))
    nj = d // w
    rc = min(rc, seq // 2)
    kern = functools.partial(_hyena_in_kernel, seq=seq, rc=rc)
    wspec = lambda s: pl.BlockSpec((d, w), lambda b, j: (0, s * nj + j))
    out_spec = pl.BlockSpec((1, seq, w), lambda b, j: (b, 0, j))
    return pl.pallas_call(
        kern,
        grid=(bsz, nj),
        in_specs=[
            pl.BlockSpec((1, seq, d), lambda b, j: (b, 0, 0)),
            pl.BlockSpec((1, d), lambda b, j: (0, 0)),
            _mod_block(d, 1, lambda b, j: mod_row_fn(b)),
            _mod_block(d, 0, lambda b, j: mod_row_fn(b)),
            wspec(0), wspec(1), wspec(2),
            pl.BlockSpec((3, w), lambda b, j: (0, j)),
            pl.BlockSpec((3, 3, w), lambda b, j: (0, 0, j)),
            pl.BlockSpec((3, w), lambda b, j: (0, j)),
        ],
        out_specs=[out_spec, out_spec],
        out_shape=[jax.ShapeDtypeStruct((bsz, seq, d), BF16)] * 2,
        scratch_shapes=[pltpu.VMEM((seq, d), BF16), pltpu.VMEM((d // LANES, 2 * rc, LANES), F32)],
        compiler_params=_cparams("arbitrary", "arbitrary"),
    )(x, gain.reshape(1, d), mods, mods, w_in, w_in, w_in,
      b_in.reshape(3, d), conv_w.reshape(3, 3, d), conv_b.reshape(3, d))


def _dft_table(length, n_freq, shifted, kb=32):
    if shifted:
        period = 8 * length
        nn = 2 * jnp.arange(length, dtype=jnp.int32)[None, :] + 1
    else:
        period = 4 * length
        nn = jnp.arange(length, dtype=jnp.int32)[None, :]
    kh = jnp.arange(n_freq // kb, dtype=jnp.int32)[:, None]
    kl = jnp.arange(kb, dtype=jnp.int32)[:, None]
    to_angle = lambda m: (m % period).astype(F32) * (2.0 * math.pi / period)
    alpha = to_angle(2 * kb * kh * nn)[:, None, :]
    beta = to_angle((2 * kl + 1) * nn)[None, :, :]
    ca, sa, cb, sb = jnp.cos(alpha), jnp.sin(alpha), jnp.cos(beta), jnp.sin(beta)
    cos_t = (ca * cb - sa * sb).reshape(n_freq, length)
    sin_t = (sa * cb + ca * sb).reshape(n_freq, length)
    return jnp.concatenate([cos_t, sin_t], axis=0).astype(BF16)


def _filter_kernel(bands_ref, w1t_ref, w1c_ref, w1s_ref, b1_ref, f1_ref, w2_ref, b2_ref, f2_ref,
                   w3f_ref, w3b_ref, deltas_ref, bias_ref, psi_ref, tab_ref, o_ref, h_buf, *, seq):
    w = o_ref.shape[-1]
    half = seq // 2

    @pl.when(pl.program_id(0) == 0)
    def _():
        t64 = lax.broadcasted_iota(jnp.int32, (seq, w1t_ref.shape[-1]), 0).astype(F32) / seq
        t16 = lax.broadcasted_iota(jnp.int32, (seq, HY_BANDS), 0).astype(F32) / seq
        ang = (2.0 * math.pi * t16) * bands_ref[...]
        pre = t64 * w1t_ref[...] + _dot_hi(jnp.cos(ang), w1c_ref[...]) + _dot_hi(jnp.sin(ang), w1s_ref[...])
        h = jnp.sin(f1_ref[...] * (pre + b1_ref[...]))
        h_buf[...] = jnp.sin(f2_ref[...] * (_dot_hi(h, w2_ref[...]) + b2_ref[...]))

    h = h_buf[...]
    row = lax.broadcasted_iota(jnp.int32, (seq, w), 0)
    decay = jnp.exp(-(row.astype(F32) / seq) * deltas_ref[...]) + HY_SHIFT
    h_f = _dot_hi(h, w3f_ref[...]) * decay
    h_b = _dot_hi(h, w3b_ref[...]) * decay
    h_b = jnp.where(row == 0, 0.0, h_b)
    h_sum = h_f + h_b
    h_dif = h_b - h_f
    alt = jnp.where(row % 2 == 0, 1.0, -1.0)
    cos_lo = tab_ref[pl.ds(0, half), :]
    sin_lo = tab_ref[pl.ds(half, half), :]
    bias = bias_ref[...]
    p_lo = _dot(cos_lo, h_sum.astype(BF16)) + bias
    p_hi = _dot(cos_lo, (h_sum * alt).astype(BF16)) + bias
    q_lo = _dot(sin_lo, h_dif.astype(BF16))
    q_hi = -_dot(sin_lo, (h_dif * alt).astype(BF16))
    d_re = p_lo - p_hi
    d_im = q_lo + q_hi
    psi_c = jnp.concatenate([psi_ref[0]] * (w // psi_ref.shape[-1]), axis=1)
    psi_s = jnp.concatenate([psi_ref[1]] * (w // psi_ref.shape[-1]), axis=1)
    o_ref[0] = p_lo + p_hi
    o_ref[1] = q_lo - q_hi
    o_ref[2] = d_re * psi_c - d_im * psi_s
    o_ref[3] = d_re * psi_s + d_im * psi_c
    o_ref[4] = d_re * psi_c + d_im * psi_s
    o_ref[5] = d_im * psi_c - d_re * psi_s


def _hyena_filter_tables(seq, fw1, fb1, ff1, fw2, fb2, ff2, fw3, fbias, *, w=256):
    d = fbias.shape[-1]
    fwid = fw2.shape[0]
    half = seq // 2
    lanes = LANES
    bands = jnp.linspace(1e-4, HY_BANDS - 1, HY_BANDS, dtype=F32).reshape(1, HY_BANDS)
    deltas = jnp.abs(jnp.linspace(math.log(HY_DECAY_TARGET) / HY_SLOW, math.log(HY_DECAY_TARGET) / HY_FAST,
                                  d, dtype=F32)).reshape(1, d)
    theta = (2 * jnp.arange(half, dtype=F32) + 1.0) * (math.pi / (2 * seq))
    psi = jnp.broadcast_to(jnp.stack([jnp.cos(theta), jnp.sin(theta)])[:, :, None], (2, half, lanes))
    tab_plain = _dft_table(seq, half, shifted=False)
    nj = d // w
    small = lambda shape: pl.BlockSpec(shape, lambda j: (0,) * len(shape))
    kern = functools.partial(_filter_kernel, seq=seq)
    return pl.pallas_call(
        kern,
        grid=(nj,),
        in_specs=[
            small((1, HY_BANDS)), small((1, fwid)), small((HY_BANDS, fwid)), small((HY_BANDS, fwid)),
            small((1, fwid)), small((1, fwid)), small((fwid, fwid)), small((1, fwid)), small((1, fwid)),
            pl.BlockSpec((fwid, w), lambda j: (0, j)),
            pl.BlockSpec((fwid, w), lambda j: (0, nj + j)),
            pl.BlockSpec((1, w), lambda j: (0, j)),
            pl.BlockSpec((1, w), lambda j: (0, j)),
            small((2, half, lanes)),
            _const_spec((2 * half, seq)),
        ],
        out_specs=pl.BlockSpec((6, half, w), lambda j: (0, 0, j)),
        out_shape=jax.ShapeDtypeStruct((6, half, d), F32),
        scratch_shapes=[pltpu.VMEM((seq, fwid), F32)],
        compiler_params=_cparams("arbitrary"),
    )(bands, fw1[0:1], fw1[1:1 + HY_BANDS], fw1[1 + HY_BANDS:], fb1.reshape(1, fwid), ff1.reshape(1, fwid),
      fw2, fb2.reshape(1, fwid), ff2.reshape(1, fwid), fw3, fw3, deltas, fbias.reshape(1, d), psi, tab_plain)


def _fftconv_kernel(u_ref, x1_ref, sef_ref, tab_ref, o_ref, *, seq):
    half = seq // 2
    cos_t = tab_ref[pl.ds(0, half), :]
    sin_t = tab_ref[pl.ds(half, half), :]
    u0 = u_ref[0, 0:half, :]
    u1 = u_ref[0, half:seq, :]
    a0, b0 = _dot(cos_t, u0), _dot(sin_t, u0)
    a1, b1 = _dot(cos_t, u1), _dot(sin_t, u1)
    s_re, s_im, e_re, e_im, f_re, f_im = (sef_ref[i] for i in range(6))
    v0_re = (s_re * a0 + s_im * b0 + f_re * a1 + f_im * b1).astype(BF16)
    v0_im = (s_im * a0 - s_re * b0 + f_im * a1 - f_re * b1).astype(BF16)
    v1_re = (e_re * a0 + e_im * b0 + s_re * a1 + s_im * b1).astype(BF16)
    v1_im = (e_im * a0 - e_re * b0 + s_im * a1 - s_re * b1).astype(BF16)
    y0 = _dot(cos_t, v0_re) - _dot(sin_t, v0_im)
    y1 = _dot(cos_t, v1_re) - _dot(sin_t, v1_im)
    o_ref[0, 0:half, :] = (y0 * (1.0 / seq) * x1_ref[0, 0:half, :].astype(F32)).astype(BF16)
    o_ref[0, half:seq, :] = (y1 * (1.0 / seq) * x1_ref[0, half:seq, :].astype(F32)).astype(BF16)


def _fftconv(u, x1, sef, *, block_elems=2048 * 256):
    bsz, seq, d = u.shape
    half = seq // 2
    w = min(d, max(MXU_WIDTH, block_elems // seq // MXU_WIDTH * MXU_WIDTH))
    nj = d // w
    act = pl.BlockSpec((1, seq, w), lambda j, b: (b, 0, j))
    kern = functools.partial(_fftconv_kernel, seq=seq)
    return pl.pallas_call(
        kern,
        grid=(nj, bsz),
        in_specs=[act, act, pl.BlockSpec((6, half, w), lambda j, b: (0, 0, j)), _const_spec((2 * half, half))],
        out_specs=act,
        out_shape=jax.ShapeDtypeStruct((bsz, seq, d), BF16),
        compiler_params=_cparams("arbitrary", "arbitrary"),
    )(u, x1, sef, _dft_table(half, half, shifted=True))


def _proj_res_parity_kernel(a_ref, x_ref, w_ref, b_ref, gain_ref, g_ref, o_ref, x_slabs, o_slabs):
    th = a_ref.shape[2]
    n_slabs = x_slabs.shape[0]
    for k in range(n_slabs):
        x_slabs[k] = x_ref[0, :, k * LANES:(k + 1) * LANES]
    for parity in range(2):
        rows = pl.ds(parity, th, stride=2)
        y = _dot(a_ref[0, parity], w_ref[...]) + b_ref[...]
        x_par = jnp.concatenate([x_slabs[k, rows, :] for k in range(n_slabs)], axis=1)
        out = x_par + g_ref[0, 0] * (_rms(y) * gain_ref[...])
        for k in range(n_slabs):
            o_slabs[k, rows, :] = out[:, k * LANES:(k + 1) * LANES]
    for k in range(n_slabs):
        o_ref[0, :, k * LANES:(k + 1) * LANES] = o_slabs[k]


def _proj_res_parity(a, x, w, b, gain, mods, gate_chunk, mod_row_fn, *, th):
    bsz, seq, d = x.shape
    half = seq // 2
    th = min(th, half)
    return pl.pallas_call(
        _proj_res_parity_kernel,
        grid=(bsz, half // th),
        in_specs=[
            pl.BlockSpec((1, 2, th, d), lambda b, i: (b, 0, i, 0)),
            pl.BlockSpec((1, 2 * th, d), lambda b, i: (b, i, 0)),
            _const_spec((d, d)),
            pl.BlockSpec((1, d), lambda b, i: (0, 0)),
            pl.BlockSpec((1, d), lambda b, i: (0, 0)),
            _mod_block(d, gate_chunk, lambda b, i: mod_row_fn(b)),
        ],
        out_specs=pl.BlockSpec((1, 2 * th, d), lambda b, i: (b, i, 0)),
        out_shape=jax.ShapeDtypeStruct((bsz, seq, d), F32),
        scratch_shapes=[pltpu.VMEM((d // LANES, 2 * th, LANES), F32)] * 2,
        compiler_params=_cparams("arbitrary", "arbitrary"),
    )(a.reshape(bsz, 2, half, d), x, w, b.reshape(1, d), gain.reshape(1, d), mods)


def _mlp_kernel(x_ref, pre_ref, sc_ref, sh_ref, g_ref, w1_ref, w2_ref, post_ref, o_ref, *, ff_chunk, sub_rows):
    d_ff = w1_ref.shape[-1]
    mult = pre_ref[...] * (1.0 + sc_ref[0, 0])
    sh = sh_ref[0, 0]

    def rows_of(i):
        return slice(i * sub_rows, (i + 1) * sub_rows)

    def hidden(i):
        return (_rms(x_ref[rows_of(i), :]) * mult + sh).astype(BF16)

    def mix(h):
        acc = None
        for c in range(d_ff // ff_chunk):
            cols = pl.ds(c * ff_chunk, ff_chunk)
            a = jnp.maximum(_dot(h, w1_ref[:, cols]), 0.0)
            part = _dot((a * a).astype(BF16), w2_ref[cols, :])
            acc = part if acc is None else acc + part
        return acc

    def finish(i, acc):
        o_ref[rows_of(i), :] = x_ref[rows_of(i), :] + g_ref[0, 0] * (_rms(acc) * post_ref[...])

    n_sub = x_ref.shape[0] // sub_rows
    pending = mix(hidden(0))
    for i in range(1, n_sub):
        nxt = mix(hidden(i))
        finish(i - 1, pending)
        pending = nxt
    finish(n_sub - 1, pending)


def _mlp(x, pre, post, mods, mod_row_fn, w1, w2, layer, *, tm, ff_chunk=1024, sub_rows=512):
    rows, d = x.shape
    d_ff = w1.shape[-1]
    kern = functools.partial(_mlp_kernel, ff_chunk=ff_chunk, sub_rows=min(sub_rows, tm))
    vec = pl.BlockSpec((1, d), lambda i: (0, 0))
    return pl.pallas_call(
        kern,
        grid=(rows // tm,),
        in_specs=[
            pl.BlockSpec((tm, d), lambda i: (i, 0)),
            vec,
            _mod_block(d, 4, mod_row_fn),
            _mod_block(d, 3, mod_row_fn),
            _mod_block(d, 5, mod_row_fn),
            _const_spec((d, d_ff), layer),
            _const_spec((d_ff, d), layer),
            vec,
        ],
        out_specs=pl.BlockSpec((tm, d), lambda i: (i, 0)),
        out_shape=jax.ShapeDtypeStruct((rows, d), F32),
        compiler_params=_cparams("arbitrary"),
    )(x, pre.reshape(1, d), mods, mods, mods, w1, w2, post.reshape(1, d))


def _hyena_out_mlp_kernel(a_ref, x_ref, wo_ref, bo_ref, mixpost_ref, g1_ref, pre_ref, sc_ref, sh_ref, g2_ref,
                          w1_ref, w2_ref, post_ref, o_ref, slabs, *, ff_chunk):
    th = a_ref.shape[2]
    d_ff = w1_ref.shape[-1]
    n_slabs = slabs.shape[0]
    for k in range(n_slabs):
        slabs[k] = x_ref[0, :, k * LANES:(k + 1) * LANES]
    parity_rows = [pl.ds(parity, th, stride=2) for parity in range(2)]
    x = jnp.concatenate([jnp.concatenate([slabs[k, rows, :] for k in range(n_slabs)], axis=1)
                         for rows in parity_rows], axis=0)
    a = jnp.concatenate([a_ref[0, 0], a_ref[0, 1]], axis=0)
    y = _dot(a, wo_ref[...]) + bo_ref[...]
    x1 = x + g1_ref[0, 0] * (_rms(y) * mixpost_ref[...])
    h = _norm_mod(x1, pre_ref[...], sc_ref[0, 0], sh_ref[0, 0]).astype(BF16)
    acc = None
    for c in range(d_ff // ff_chunk):
        cols = pl.ds(c * ff_chunk, ff_chunk)
        t = jnp.maximum(_dot(h, w1_ref[:, cols]), 0.0)
        part = _dot((t * t).astype(BF16), w2_ref[cols, :])
        acc = part if acc is None else acc + part
    out = x1 + g2_ref[0, 0] * (_rms(acc) * post_ref[...])
    for parity, rows in enumerate(parity_rows):
        for k in range(n_slabs):
            slabs[k, rows, :] = out[parity * th:(parity + 1) * th, k * LANES:(k + 1) * LANES]
    for k in range(n_slabs):
        o_ref[0, :, k * LANES:(k + 1) * LANES] = slabs[k]


def _hyena_out_mlp(a, x, w_out, b_out, mix_post, pre, post, mods, mod_row_fn, w1, w2, layer, *, th=256,
                   ff_chunk=1024):
    bsz, seq, d = x.shape
    half = seq // 2
    d_ff = w1.shape[-1]
    th = min(th, half)
    kern = functools.partial(_hyena_out_mlp_kernel, ff_chunk=ff_chunk)
    vec = pl.BlockSpec((1, d), lambda b, i: (0, 0))
    mod = lambda chunk: _mod_block(d, chunk, lambda b, i: mod_row_fn(b))
    rows = pl.BlockSpec((1, 2 * th, d), lambda b, i: (b, i, 0))
    return pl.pallas_call(
        kern,
        grid=(bsz, half // th),
        in_specs=[
            pl.BlockSpec((1, 2, th, d), lambda b, i: (b, 0, i, 0)),
            rows,
            _const_spec((d, d)), vec, vec, mod(2),
            vec, mod(4), mod(3), mod(5),
            _const_spec((d, d_ff), layer), _const_spec((d_ff, d), layer), vec,
        ],
        out_specs=rows,
        out_shape=jax.ShapeDtypeStruct((bsz, seq, d), F32),
        scratch_shapes=[pltpu.VMEM((d // LANES, 2 * th, LANES), F32)],
        compiler_params=_cparams("arbitrary", "arbitrary"),
    )(a.reshape(bsz, 2, half, d), x, w_out, b_out.reshape(1, d), mix_post.reshape(1, d), mods,
      pre.reshape(1, d), mods, mods, mods, w1, w2, post.reshape(1, d))


def _qkv_kernel(x_ref, pre_ref, sc_ref, sh_ref, w_ref, gains_ref, cos_ref, sin_ref,
                *out_refs, n_q_pairs, rope, sub_rows):
    q_ref = out_refs[0] if n_q_pairs else None
    k_ref, v_ref = out_refs[-2:]
    hd = HEAD_DIM
    tm = x_ref.shape[0]
    mult = pre_ref[...] * (1.0 + sc_ref[0, 0])
    sh = sh_ref[0, 0]
    q_scale = ATTN_SCALE * LOG2_E
    gain_ab = [(gains_ref[0:1, :] * q_scale, gains_ref[1:2, :] * q_scale), (gains_ref[2:3, :], gains_ref[3:4, :])]
    half_r = lax.broadcasted_iota(jnp.int32, (2 * hd, 2 * hd), 0) // (hd // 2)
    half_c = lax.broadcasted_iota(jnp.int32, (2 * hd, 2 * hd), 1) // (hd // 2)
    same_head = (half_r == half_c).astype(BF16)

    def rows_of(i):
        return slice(i * sub_rows, (i + 1) * sub_rows)

    def project(i):
        h = (_rms(x_ref[rows_of(i), :]) * mult + sh).astype(BF16)
        return _dot(h, w_ref[...])

    def finish(i, qkv):
        rows = rows_of(i)
        if rope:
            cos_t, sin_t = cos_ref[rows, :], sin_ref[rows, :]
            tabs = [(ga * cos_t, gb * sin_t, ga * sin_t, gb * cos_t) for ga, gb in gain_ab]

        n_pairs = n_q_pairs + 1
        squares = [(qkv[:, 2 * p * hd:(2 * p + 1) * hd] ** 2 + qkv[:, (2 * p + 1) * hd:(2 * p + 2) * hd] ** 2)
                   .astype(BF16) for p in range(n_pairs)]
        sums = []
        for p in range(0, n_pairs - 1, 2):
            both = _dot(jnp.concatenate(squares[p:p + 2], axis=1), same_head)
            sums += [both[:, 0:hd], both[:, hd:2 * hd]]
        if n_pairs % 2:
            sums.append(_dot(squares[-1], same_head[0:hd, 0:hd]))

        def pair(idx, kind):
            a = qkv[:, 2 * idx * hd:(2 * idx + 1) * hd]
            b = qkv[:, (2 * idx + 1) * hd:(2 * idx + 2) * hd]
            r = lax.rsqrt(sums[idx] * (1.0 / hd) + EPS)
            if rope:
                ca, sb, sa, cb = tabs[kind]
                return (r * (a * ca - b * sb)).astype(BF16), (r * (a * sa + b * cb)).astype(BF16)
            ga, gb = gain_ab[kind]
            return (r * (a * ga)).astype(BF16), (r * (b * gb)).astype(BF16)

        for p in range(n_q_pairs):
            qa, qb = pair(p, 0)
            q_ref[rows, 2 * p * hd:(2 * p + 1) * hd] = qa
            q_ref[rows, (2 * p + 1) * hd:(2 * p + 2) * hd] = qb
        ka, kb = pair(n_q_pairs, 1)
        k_ref[rows, 0:hd] = ka
        k_ref[rows, hd:2 * hd] = kb
        v0 = 2 * (n_q_pairs + 1) * hd
        v_ref[rows, :] = qkv[:, v0:v0 + N_KV_HEADS * hd].astype(BF16)

    n_sub = tm // sub_rows
    pending = project(0)
    for i in range(1, n_sub):
        nxt = project(i)
        finish(i - 1, pending)
        pending = nxt
    finish(n_sub - 1, pending)


def _qkv(x, pre, mods, mod_row_fn, w, gains, cos_t, sin_t, *, n_q_pairs, rope, tm, seq):
    rows, d = x.shape
    hd = HEAD_DIM
    n = w.shape[-1]
    kern = functools.partial(_qkv_kernel, n_q_pairs=n_q_pairs, rope=rope, sub_rows=min(tm, 256))
    vec = pl.BlockSpec((1, d), lambda i: (0, 0))
    tiles_per_seq = seq // tm
    pos = pl.BlockSpec((tm, hd), lambda i: (i % tiles_per_seq, 0))
    widths = ([2 * n_q_pairs * hd] if n_q_pairs else []) + [N_KV_HEADS * hd] * 2
    return pl.pallas_call(
        kern,
        grid=(rows // tm,),
        in_specs=[
            pl.BlockSpec((tm, d), lambda i: (i, 0)),
            vec,
            _mod_block(d, 1, mod_row_fn),
            _mod_block(d, 0, mod_row_fn),
            _const_spec((d, n)),
            pl.BlockSpec((4, hd), lambda i: (0, 0)),
            pos, pos,
        ],
        out_specs=[pl.BlockSpec((tm, wd), lambda i: (i, 0)) for wd in widths],
        out_shape=[jax.ShapeDtypeStruct((rows, wd), BF16) for wd in widths],
        compiler_params=_cparams("arbitrary"),
    )(x, pre.reshape(1, d), mods, mods, w, gains, cos_t, sin_t)


def _attn_kernel(q_ref, kl_ref, vl_ref, kc_ref, vc_ref, x_ref, wo_ref, gain_ref, g_ref, o_ref, k_all, v_ext,
                 *, n_pairs, kv_blocks):
    hd = HEAD_DIM
    tq = q_ref.shape[1]
    seq = kl_ref.shape[1]
    total = k_all.shape[0]

    @pl.when(pl.program_id(1) == 0)
    def _():
        k_all[0:seq, :] = kl_ref[0]
        k_all[seq:total, :] = kc_ref[0]
        for h in range(N_KV_HEADS):
            v_ext[h, 0:seq, 0:hd] = vl_ref[0, :, h * hd:(h + 1) * hd]
            v_ext[h, seq:total, 0:hd] = vc_ref[0, :, h * hd:(h + 1) * hd]
            v_ext[h, :, hd:2 * hd] = jnp.ones((total, hd), BF16)

    nt = (((1,), (1,)), ((), ()))
    lane_half = (lax.broadcasted_iota(jnp.int32, (1, 2 * hd), 1) % hd) // (hd // 2)

    def start(kv_head):
        keep = (lane_half == kv_head).astype(BF16)
        q = jnp.concatenate([q_ref[0, :, 2 * p * hd:(2 * p + 2) * hd] * keep for p in range(n_pairs)], axis=0)
        st = dict(h=kv_head, q=q, m=None, acc=None)
        st["s_next"] = lax.dot_general(q, k_all[kv_blocks[0][0]:kv_blocks[0][1], :], nt, preferred_element_type=F32)
        return st

    def step(st, i):
        s0, s1 = kv_blocks[i]
        s = st["s_next"]
        if i + 1 < len(kv_blocks):
            n0, n1 = kv_blocks[i + 1]
            st["s_next"] = lax.dot_general(st["q"], k_all[n0:n1, :], nt, preferred_element_type=F32)
        m, acc = st["m"], st["acc"]
        m_blk = jnp.max(s, axis=-1, keepdims=True)
        m_new = m_blk if m is None else jnp.maximum(m, m_blk)
        pv = _dot(jnp.exp2(s - m_new).astype(BF16), v_ext[st["h"], s0:s1, :])
        st["acc"] = pv if m is None else acc * jnp.exp2(m - m_new) + pv
        st["m"] = m_new

    def heads_of(st):
        acc = st["acc"]
        o = (acc[:, 0:hd] / acc[:, hd:2 * hd]).astype(BF16)
        return [o[p * tq:(p + 1) * tq, :] for p in range(n_pairs)]

    last = len(kv_blocks) - 1
    first, second = start(0), None
    for i in range(last):
        step(first, i)
    second = start(1)
    step(first, last)
    for i in range(last + 1):
        step(second, i)
    attn = jnp.concatenate(heads_of(first) + heads_of(second), axis=1)
    y = _dot(attn, wo_ref[...])
    o_ref[0] = x_ref[0] + g_ref[0, 0] * (_rms(y) * gain_ref[...])


def _kv_blocks(total, pattern):
    assert total % MXU_WIDTH == 0
    n_tiles = total // MXU_WIDTH
    blocks, start, i = [], 0, 0
    while start < n_tiles:
        size = min(pattern[i % len(pattern)], n_tiles - start)
        blocks.append((start * MXU_WIDTH, (start + size) * MXU_WIDTH))
        start += size
        i += 1
    return tuple(blocks)


def _attention(q, k_l, v_l, k_c, v_c, x, w_o, gain, mods, gate_chunk, mod_row_fn, *, tq=512, kv_pattern=(5, 3, 1)):
    bsz, seq, dq = q.shape
    d = x.shape[-1]
    ctx_len = k_c.shape[1]
    hd = HEAD_DIM
    kv = N_KV_HEADS * hd
    n_pairs = dq // (2 * hd)
    total = seq + ctx_len
    kern = functools.partial(_attn_kernel, n_pairs=n_pairs, kv_blocks=_kv_blocks(total, kv_pattern))
    whole = lambda rows, width: pl.BlockSpec((1, rows, width), lambda b, i: (b, 0, 0))
    tile = lambda width: pl.BlockSpec((1, tq, width), lambda b, i: (b, i, 0))
    return pl.pallas_call(
        kern,
        grid=(bsz, seq // tq),
        in_specs=[
            tile(dq), whole(seq, 2 * hd), whole(seq, kv), whole(ctx_len, 2 * hd), whole(ctx_len, kv),
            tile(d),
            _const_spec((dq, d)),
            pl.BlockSpec((1, d), lambda b, i: (0, 0)),
            _mod_block(d, gate_chunk, lambda b, i: mod_row_fn(b)),
        ],
        out_specs=tile(d),
        out_shape=jax.ShapeDtypeStruct((bsz, seq, d), F32),
        scratch_shapes=[pltpu.VMEM((total, 2 * hd), BF16), pltpu.VMEM((N_KV_HEADS, total, 2 * hd), BF16)],
        compiler_params=_cparams("arbitrary", "arbitrary"),
    )(q, k_l, v_l, k_c, v_c, x, w_o, gain.reshape(1, d), mods)


def _rope_tables(seq):
    rows = seq // GRID_W
    pairs = HEAD_DIM // 4
    row = jnp.repeat(jnp.arange(rows, dtype=F32), GRID_W)
    col = jnp.tile(jnp.arange(GRID_W, dtype=F32), rows)
    inv = ROPE_THETA ** (-jnp.arange(pairs, dtype=F32) / pairs)
    ang = jnp.concatenate([row[:, None] * inv[None, :], col[:, None] * inv[None, :]], axis=-1)
    cos, sin = jnp.cos(ang), jnp.sin(ang)
    return jnp.concatenate([cos, cos], axis=-1), jnp.concatenate([sin, sin], axis=-1)


def _pair_columns(head_a, head_b):
    hd = HEAD_DIM
    even, odd = jnp.arange(0, hd, 2), jnp.arange(1, hd, 2)
    return jnp.concatenate([head_a * hd + even, head_b * hd + even, head_a * hd + odd, head_b * hd + odd])


def kernel(x, c, ctx, c_ctx, mod_w, mod_b, mix_norm_pre, mix_norm_post, mlp_norm_pre, mlp_norm_post, mlp_w1, mlp_w2, hy_w_in, hy_b_in, hy_conv_w, hy_conv_b, hy_filt_w1, hy_filt_b1, hy_filt_freq1, hy_filt_w2, hy_filt_b2, hy_filt_freq2, hy_filt_w3, hy_filt_bias, hy_w_out, hy_b_out, attn_w_qkv, attn_q_norm, attn_k_norm, attn_w_o):
    bsz, seq, d = x.shape
    ctx_len = ctx.shape[1]
    hd = HEAD_DIM
    n_heads = d // hd
    tm = 1024

    ctx_row = bsz
    n_rows = -(-(bsz + 1) // MOD_ROWS_PAD) * MOD_ROWS_PAD
    cond = jnp.concatenate([c, c_ctx[None, :], jnp.zeros((n_rows - bsz - 1, d), F32)], axis=0)
    mods_all = _modulation(cond, mod_w, mod_b).reshape(mod_w.shape[0], n_rows, 6, 1, d)

    x_row = lambda i: i // (seq // tm)
    c_row = lambda i: ctx_row

    xf = x.reshape(bsz * seq, d)
    cf = ctx.reshape(bsz * ctx_len, d)

    mods = mods_all[0]
    w_in = hy_w_in[0].astype(BF16)
    w_out = hy_w_out[0].astype(BF16)
    filt = (hy_filt_w1[0], hy_filt_b1[0], hy_filt_freq1[0], hy_filt_w2[0], hy_filt_b2[0], hy_filt_freq2[0],
            hy_filt_w3[0], hy_filt_bias[0])
    w1 = _to_bf16(mlp_w1)
    w2 = _to_bf16(mlp_w2)

    def hyena_conv(tokens, mod_row_b):
        u, x1 = _hyena_in(tokens, mix_norm_pre[0], mods, mod_row_b, w_in, hy_b_in[0], hy_conv_w[0], hy_conv_b[0])
        return _fftconv(u, x1, _hyena_filter_tables(tokens.shape[1], *filt))

    x3 = xf.reshape(bsz, seq, d)
    xf = _hyena_out_mlp(hyena_conv(x3, lambda b: b), x3, w_out, hy_b_out[0], mix_norm_post[0],
                        mlp_norm_pre[0], mlp_norm_post[0], mods, lambda b: b, w1, w2, 0).reshape(bsz * seq, d)
    c3 = cf.reshape(bsz, ctx_len, d)
    cf = _proj_res_parity(hyena_conv(c3, lambda b: ctx_row), c3, w_out, hy_b_out[0], mix_norm_post[0], mods, 2,
                          lambda b: ctx_row, th=512).reshape(bsz * ctx_len, d)
    cf = _mlp(cf, mlp_norm_pre[0], mlp_norm_post[0], mods, c_row, w1, w2, 0, tm=min(tm, bsz * ctx_len))

    mods = mods_all[1]
    w_qkv = attn_w_qkv[0]
    assert N_KV_HEADS == 2 and n_heads % 2 == 0
    group = n_heads // N_KV_HEADS
    pair_cols = [_pair_columns(p, group + p) for p in range(group)] + [_pair_columns(n_heads, n_heads + 1)]
    v_cols = jnp.arange((n_heads + N_KV_HEADS) * hd, (n_heads + 2 * N_KV_HEADS) * hd)
    w_lat = w_qkv[:, jnp.concatenate(pair_cols + [v_cols])].astype(BF16)
    w_ctx = w_lat[:, n_heads * hd:]
    even, odd = jnp.arange(0, hd, 2), jnp.arange(1, hd, 2)
    gains = jnp.stack([jnp.tile(g[idx], 2) for g in (attn_q_norm[0], attn_k_norm[0]) for idx in (even, odd)])
    cos_t, sin_t = _rope_tables(seq)

    q, k_l, v_l = _qkv(xf, mix_norm_pre[1], mods, x_row, w_lat, gains, cos_t, sin_t,
                       n_q_pairs=group, rope=True, tm=tm, seq=seq)
    k_c, v_c = _qkv(cf, mix_norm_pre[1], mods, c_row, w_ctx, gains, cos_t, sin_t,
                    n_q_pairs=0, rope=False, tm=min(tm, bsz * ctx_len), seq=min(tm, bsz * ctx_len))
    kv = N_KV_HEADS * hd
    xf = _attention(q.reshape(bsz, seq, d), k_l.reshape(bsz, seq, kv), v_l.reshape(bsz, seq, kv),
                    k_c.reshape(bsz, ctx_len, kv), v_c.reshape(bsz, ctx_len, kv), xf.reshape(bsz, seq, d),
                    attn_w_o[0].astype(BF16), mix_norm_post[1], mods, 2, lambda b: b).reshape(bsz * seq, d)
    xf = _mlp(xf, mlp_norm_pre[1], mlp_norm_post[1], mods, x_row, w1, w2, 1, tm=tm)
    return xf.reshape(bsz, seq, d)
```

```python
import functools
import math

import jax
import jax.numpy as jnp
from jax import lax
from jax.experimental import pallas as pl
from jax.experimental.pallas import tpu as pltpu

F32 = jnp.float32
BF16 = jnp.bfloat16

EPS = 1e-6
GRID_W = 64
HY_BANDS = 16
HY_DECAY_TARGET = 1e-2
HY_FAST = 0.3
HY_SLOW = 1.5
HY_SHIFT = 0.0
HEAD_DIM = 128
N_KV_HEADS = 2
ROPE_THETA = 10000.0
ATTN_SCALE = HEAD_DIM ** -0.5
LOG2_E = math.log2(math.e)

MOD_ROWS_PAD = 8
SUBLANES = 8
LANES = 128
MXU_WIDTH = 256

VMEM_LIMIT = 56 * 1024 * 1024


def _cparams(*sem):
    return pltpu.CompilerParams(dimension_semantics=sem, vmem_limit_bytes=VMEM_LIMIT)


def _const_spec(shape, layer=None):
    nd = len(shape)
    if layer is None:
        return pl.BlockSpec(shape, lambda *_: (0,) * nd, pipeline_mode=pl.Buffered(1))
    return pl.BlockSpec((None,) + tuple(shape), lambda *_: (layer,) + (0,) * nd, pipeline_mode=pl.Buffered(1))


def _cast_kernel(w_ref, o_ref):
    o_ref[...] = w_ref[...].astype(BF16)


def _to_bf16(w, *, block_elems=1024 * 1024):
    depth, r, c = w.shape
    block_rows = min(r, max(SUBLANES, block_elems // c))
    spec = pl.BlockSpec((1, block_rows, c), lambda l, i: (l, i, 0))
    return pl.pallas_call(
        _cast_kernel,
        grid=(depth, r // block_rows),
        in_specs=[spec],
        out_specs=spec,
        out_shape=jax.ShapeDtypeStruct(w.shape, BF16),
        compiler_params=_cparams("arbitrary", "arbitrary"),
    )(w)


def _rms(x):
    return x * lax.rsqrt(jnp.mean(x * x, axis=-1, keepdims=True) + EPS)


def _norm_mod(x, gain, sc, sh):
    return _rms(x) * (gain * (1.0 + sc)) + sh


def _dot(a, b):
    return jnp.dot(a, b, preferred_element_type=F32)


def _dot_hi(a, b):
    return jnp.dot(a, b, preferred_element_type=F32, precision=lax.Precision.HIGHEST)


def _mod_kernel(cond_ref, w_ref, b_ref, o_ref):
    s = cond_ref[...]
    s = s * jax.nn.sigmoid(s)
    o_ref[0] = _dot(s.astype(BF16), w_ref[0].astype(BF16)) + b_ref[0]


def _modulation(cond, mod_w, mod_b):
    depth, d, n = mod_w.shape
    rows = cond.shape[0]
    tn = d
    return pl.pallas_call(
        _mod_kernel,
        grid=(depth, n // tn),
        in_specs=[
            pl.BlockSpec((rows, d), lambda i, j: (0, 0)),
            pl.BlockSpec((1, d, tn), lambda i, j: (i, 0, j)),
            pl.BlockSpec((1, 1, tn), lambda i, j: (i, 0, j)),
        ],
        out_specs=pl.BlockSpec((1, rows, tn), lambda i, j: (i, 0, j)),
        out_shape=jax.ShapeDtypeStruct((depth, rows, n), F32),
        compiler_params=_cparams("arbitrary", "arbitrary"),
    )(cond, mod_w, mod_b.reshape(depth, 1, n))


def _mod_block(d, chunk, row_fn):
    return pl.BlockSpec((1, 1, 1, d), lambda *idx: (row_fn(*idx), chunk, 0, 0))


def _hyena_in_kernel(x_ref, gain_ref, sc_ref, sh_ref, w0_ref, w1_ref, w2_ref, b_ref, cw_ref, cb_ref,
                     u_ref, x1_ref, h_buf, x_slabs, *, seq, rc):
    j = pl.program_id(1)
    w = u_ref.shape[-1]
    half = seq // 2
    n_chunks = half // rc
    n_slabs = x_slabs.shape[0]

    @pl.when(j == 0)
    def _():
        mult = gain_ref[...] * (1.0 + sc_ref[0, 0])
        sh = sh_ref[0, 0]

        def body(c, carry):
            base = pl.multiple_of(2 * c * rc, 2 * rc)
            for k in range(n_slabs):
                x_slabs[k] = x_ref[0, pl.ds(base, 2 * rc), k * LANES:(k + 1) * LANES]
            for parity in range(2):
                xs = jnp.concatenate([x_slabs[k, pl.ds(parity, rc, stride=2), :] for k in range(n_slabs)], axis=1)
                h_buf[pl.ds(base + parity * rc, rc), :] = (_rms(xs) * mult + sh).astype(BF16)
            return carry

        lax.fori_loop(0, n_chunks, body, 0)

    sub = lax.broadcasted_iota(jnp.int32, (SUBLANES, w), 0)
    w_refs = (w0_ref, w1_ref, w2_ref)
    taps = [[cw_ref[k, s:s + 1, :] for s in range(3)] for k in range(3)]
    bias = [cb_ref[s:s + 1, :] + b_ref[s:s + 1, :] * (taps[0][s] + taps[1][s] + taps[2][s]) for s in range(3)]

    def project(c):
        h = h_buf[2 * c * rc:2 * (c + 1) * rc, :]
        return [_dot(h, w_refs[s][...]) for s in range(3)]

    def conv(c, m_before, m_here, m_after):
        even_streams, odd_streams = [], []
        for s in range(3):
            m_even, m_odd = m_here[s][0:rc], m_here[s][rc:2 * rc]
            row_before = -b_ref[s:s + 1, :] if m_before is None else m_before[s][2 * rc - 1:2 * rc, :]
            row_after = -b_ref[s:s + 1, :] if m_after is None else m_after[s][0:1, :]
            odd_prev = pltpu.roll(m_odd, 1, 0)
            odd_prev = jnp.concatenate([jnp.where(sub == 0, row_before, odd_prev[0:SUBLANES]),
                                        odd_prev[SUBLANES:]], axis=0)
            even_next = pltpu.roll(m_even, rc - 1, 0)
            even_next = jnp.concatenate([even_next[:rc - SUBLANES],
                                         jnp.where(sub == SUBLANES - 1, row_after, even_next[rc - SUBLANES:])], axis=0)
            even_streams.append(odd_prev * taps[0][s] + m_even * taps[1][s] + m_odd * taps[2][s] + bias[s])
            odd_streams.append(m_even * taps[0][s] + m_odd * taps[1][s] + even_next * taps[2][s] + bias[s])
        for base, (x1, x2, v) in ((0, even_streams), (half, odd_streams)):
            u_ref[0, base + c * rc:base + (c + 1) * rc, :] = (v * x2).astype(BF16)
            x1_ref[0, base + c * rc:base + (c + 1) * rc, :] = x1.astype(BF16)

    zs = [project(0)]
    for c in range(1, n_chunks):
        zs.append(project(c))
        conv(c - 1, zs[c - 2] if c >= 2 else None, zs[c - 1], zs[c])
    conv(n_chunks - 1, zs[n_chunks - 2] if n_chunks >= 2 else None, zs[n_chunks - 1], None)


def _hyena_in(x, gain, mods, mod_row_fn, w_in, b_in, conv_w, conv_b, *, block_elems=2048 * 512, rc=128):
    bsz, seq, d = x.shape
    w = min(d, max(MXU_WIDTH, block_elems // seq // MXU_WIDTH * MXU_WIDTH))
    nj = d // w
    rc = min(rc, seq // 2)
    kern = functools.partial(_hyena_in_kernel, seq=seq, rc=rc)
    wspec = lambda s: pl.BlockSpec((d, w), lambda b, j: (0, s * nj + j))
    out_spec = pl.BlockSpec((1, seq, w), lambda b, j: (b, 0, j))
    return pl.pallas_call(
        kern,
        grid=(bsz, nj),
        in_specs=[
            pl.BlockSpec((1, seq, d), lambda b, j: (b, 0, 0)),
            pl.BlockSpec((1, d), lambda b, j: (0, 0)),
            _mod_block(d, 1, lambda b, j: mod_row_fn(b)),
            _mod_block(d, 0, lambda b, j: mod_row_fn(b)),
            wspec(0), wspec(1), wspec(2),
            pl.BlockSpec((3, w), lambda b, j: (0, j)),
            pl.BlockSpec((3, 3, w), lambda b, j: (0, 0, j)),
            pl.BlockSpec((3, w), lambda b, j: (0, j)),
        ],
        out_specs=[out_spec, out_spec],
        out_shape=[jax.ShapeDtypeStruct((bsz, seq, d), BF16)] * 2,
        scratch_shapes=[pltpu.VMEM((seq, d), BF16), pltpu.VMEM((d // LANES, 2 * rc, LANES), F32)],
        compiler_params=_cparams("arbitrary", "arbitrary"),
    )(x, gain.reshape(1, d), mods, mods, w_in, w_in, w_in,
      b_in.reshape(3, d), conv_w.reshape(3, 3, d), conv_b.reshape(3, d))


def _dft_table(length, n_freq, shifted, kb=32):
    if shifted:
        period = 8 * length
        nn = 2 * jnp.arange(length, dtype=jnp.int32)[None, :] + 1
    else:
        period = 4 * length
        nn = jnp.arange(length, dtype=jnp.int32)[None, :]
    kh = jnp.arange(n_freq // kb, dtype=jnp.int32)[:, None]
    kl = jnp.arange(kb, dtype=jnp.int32)[:, None]
    to_angle = lambda m: (m % period).astype(F32) * (2.0 * math.pi / period)
    alpha = to_angle(2 * kb * kh * nn)[:, None, :]
    beta = to_angle((2 * kl + 1) * nn)[None, :, :]
    ca, sa, cb, sb = jnp.cos(alpha), jnp.sin(alpha), jnp.cos(beta), jnp.sin(beta)
    cos_t = (ca * cb - sa * sb).reshape(n_freq, length)
    sin_t = (sa * cb + ca * sb).reshape(n_freq, length)
    return jnp.concatenate([cos_t, sin_t], axis=0).astype(BF16)


def _filter_kernel(bands_ref, w1t_ref, w1c_ref, w1s_ref, b1_ref, f1_ref, w2_ref, b2_ref, f2_ref,
                   w3f_ref, w3b_ref, deltas_ref, bias_ref, psi_ref, tab_ref, o_ref, h_buf, *, seq):
    w = o_ref.shape[-1]
    half = seq // 2

    @pl.when(pl.program_id(0) == 0)
    def _():
        t64 = lax.broadcasted_iota(jnp.int32, (seq, w1t_ref.shape[-1]), 0).astype(F32) / seq
        t16 = lax.broadcasted_iota(jnp.int32, (seq, HY_BANDS), 0).astype(F32) / seq
        ang = (2.0 * math.pi * t16) * bands_ref[...]
        pre = t64 * w1t_ref[...] + _dot_hi(jnp.cos(ang), w1c_ref[...]) + _dot_hi(jnp.sin(ang), w1s_ref[...])
        h = jnp.sin(f1_ref[...] * (pre + b1_ref[...]))
        h_buf[...] = jnp.sin(f2_ref[...] * (_dot_hi(h, w2_ref[...]) + b2_ref[...]))

    h = h_buf[...]
    row = lax.broadcasted_iota(jnp.int32, (seq, w), 0)
    decay = jnp.exp(-(row.astype(F32) / seq) * deltas_ref[...]) + HY_SHIFT
    h_f = _dot_hi(h, w3f_ref[...]) * decay
    h_b = _dot_hi(h, w3b_ref[...]) * decay
    h_b = jnp.where(row == 0, 0.0, h_b)
    h_sum = h_f + h_b
    h_dif = h_b - h_f
    alt = jnp.where(row % 2 == 0, 1.0, -1.0)
    cos_lo = tab_ref[pl.ds(0, half), :]
    sin_lo = tab_ref[pl.ds(half, half), :]
    bias = bias_ref[...]
    p_lo = _dot(cos_lo, h_sum.astype(BF16)) + bias
    p_hi = _dot(cos_lo, (h_sum * alt).astype(BF16)) + bias
    q_lo = _dot(sin_lo, h_dif.astype(BF16))
    q_hi = -_dot(sin_lo, (h_dif * alt).astype(BF16))
    d_re = p_lo - p_hi
    d_im = q_lo + q_hi
    psi_c = jnp.concatenate([psi_ref[0]] * (w // psi_ref.shape[-1]), axis=1)
    psi_s = jnp.concatenate([psi_ref[1]] * (w // psi_ref.shape[-1]), axis=1)
    o_ref[0] = p_lo + p_hi
    o_ref[1] = q_lo - q_hi
    o_ref[2] = d_re * psi_c - d_im * psi_s
    o_ref[3] = d_re * psi_s + d_im * psi_c
    o_ref[4] = d_re * psi_c + d_im * psi_s
    o_ref[5] = d_im * psi_c - d_re * psi_s


def _hyena_filter_tables(seq, fw1, fb1, ff1, fw2, fb2, ff2, fw3, fbias, *, w=256):
    d = fbias.shape[-1]
    fwid = fw2.shape[0]
    half = seq // 2
    lanes = LANES
    bands = jnp.linspace(1e-4, HY_BANDS - 1, HY_BANDS, dtype=F32).reshape(1, HY_BANDS)
    deltas = jnp.abs(jnp.linspace(math.log(HY_DECAY_TARGET) / HY_SLOW, math.log(HY_DECAY_TARGET) / HY_FAST,
                                  d, dtype=F32)).reshape(1, d)
    theta = (2 * jnp.arange(half, dtype=F32) + 1.0) * (math.pi / (2 * seq))
    psi = jnp.broadcast_to(jnp.stack([jnp.cos(theta), jnp.sin(theta)])[:, :, None], (2, half, lanes))
    tab_plain = _dft_table(seq, half, shifted=False)
    nj = d // w
    small = lambda shape: pl.BlockSpec(shape, lambda j: (0,) * len(shape))
    kern = functools.partial(_filter_kernel, seq=seq)
    return pl.pallas_call(
        kern,
        grid=(nj,),
        in_specs=[
            small((1, HY_BANDS)), small((1, fwid)), small((HY_BANDS, fwid)), small((HY_BANDS, fwid)),
            small((1, fwid)), small((1, fwid)), small((fwid, fwid)), small((1, fwid)), small((1, fwid)),
            pl.BlockSpec((fwid, w), lambda j: (0, j)),
            pl.BlockSpec((fwid, w), lambda j: (0, nj + j)),
            pl.BlockSpec((1, w), lambda j: (0, j)),
            pl.BlockSpec((1, w), lambda j: (0, j)),
            small((2, half, lanes)),
            _const_spec((2 * half, seq)),
        ],
        out_specs=pl.BlockSpec((6, half, w), lambda j: (0, 0, j)),
        out_shape=jax.ShapeDtypeStruct((6, half, d), F32),
        scratch_shapes=[pltpu.VMEM((seq, fwid), F32)],
        compiler_params=_cparams("arbitrary"),
    )(bands, fw1[0:1], fw1[1:1 + HY_BANDS], fw1[1 + HY_BANDS:], fb1.reshape(1, fwid), ff1.reshape(1, fwid),
      fw2, fb2.reshape(1, fwid), ff2.reshape(1, fwid), fw3, fw3, deltas, fbias.reshape(1, d), psi, tab_plain)


def _fftconv_kernel(u_ref, x1_ref, sef_ref, tab_ref, o_ref, *, seq):
    half = seq // 2
    cos_t = tab_ref[pl.ds(0, half), :]
    sin_t = tab_ref[pl.ds(half, half), :]
    u0 = u_ref[0, 0:half, :]
    u1 = u_ref[0, half:seq, :]
    a0, b0 = _dot(cos_t, u0), _dot(sin_t, u0)
    a1, b1 = _dot(cos_t, u1), _dot(sin_t, u1)
    s_re, s_im, e_re, e_im, f_re, f_im = (sef_ref[i] for i in range(6))
    v0_re = (s_re * a0 + s_im * b0 + f_re * a1 + f_im * b1).astype(BF16)
    v0_im = (s_im * a0 - s_re * b0 + f_im * a1 - f_re * b1).astype(BF16)
    v1_re = (e_re * a0 + e_im * b0 + s_re * a1 + s_im * b1).astype(BF16)
    v1_im = (e_im * a0 - e_re * b0 + s_im * a1 - s_re * b1).astype(BF16)
    y0 = _dot(cos_t, v0_re) - _dot(sin_t, v0_im)
    y1 = _dot(cos_t, v1_re) - _dot(sin_t, v1_im)
    o_ref[0, 0:half, :] = (y0 * (1.0 / seq) * x1_ref[0, 0:half, :].astype(F32)).astype(BF16)
    o_ref[0, half:seq, :] = (y1 * (1.0 / seq) * x1_ref[0, half:seq, :].astype(F32)).astype(BF16)


def _fftconv(u, x1, sef, *, block_elems=2048 * 256):
    bsz, seq, d = u.shape
    half = seq // 2
    w = min(d, max(MXU_WIDTH, block_elems // seq // MXU_WIDTH * MXU_WIDTH))
    nj = d // w
    act = pl.BlockSpec((1, seq, w), lambda j, b: (b, 0, j))
    kern = functools.partial(_fftconv_kernel, seq=seq)
    return pl.pallas_call(
        kern,
        grid=(nj, bsz),
        in_specs=[act, act, pl.BlockSpec((6, half, w), lambda j, b: (0, 0, j)), _const_spec((2 * half, half))],
        out_specs=act,
        out_shape=jax.ShapeDtypeStruct((bsz, seq, d), BF16),
        compiler_params=_cparams("arbitrary", "arbitrary"),
    )(u, x1, sef, _dft_table(half, half, shifted=True))


def _proj_res_parity_kernel(a_ref, x_ref, w_ref, b_ref, gain_ref, g_ref, o_ref, x_slabs, o_slabs):
    th = a_ref.shape[2]
    n_slabs = x_slabs.shape[0]
    for k in range(n_slabs):
        x_slabs[k] = x_ref[0, :, k * LANES:(k + 1) * LANES]
    for parity in range(2):
        rows = pl.ds(parity, th, stride=2)
        y = _dot(a_ref[0, parity], w_ref[...]) + b_ref[...]
        x_par = jnp.concatenate([x_slabs[k, rows, :] for k in range(n_slabs)], axis=1)
        out = x_par + g_ref[0, 0] * (_rms(y) * gain_ref[...])
        for k in range(n_slabs):
            o_slabs[k, rows, :] = out[:, k * LANES:(k + 1) * LANES]
    for k in range(n_slabs):
        o_ref[0, :, k * LANES:(k + 1) * LANES] = o_slabs[k]


def _proj_res_parity(a, x, w, b, gain, mods, gate_chunk, mod_row_fn, *, th):
    bsz, seq, d = x.shape
    half = seq // 2
    th = min(th, half)
    return pl.pallas_call(
        _proj_res_parity_kernel,
        grid=(bsz, half // th),
        in_specs=[
            pl.BlockSpec((1, 2, th, d), lambda b, i: (b, 0, i, 0)),
            pl.BlockSpec((1, 2 * th, d), lambda b, i: (b, i, 0)),
            _const_spec((d, d)),
            pl.BlockSpec((1, d), lambda b, i: (0, 0)),
            pl.BlockSpec((1, d), lambda b, i: (0, 0)),
            _mod_block(d, gate_chunk, lambda b, i: mod_row_fn(b)),
        ],
        out_specs=pl.BlockSpec((1, 2 * th, d), lambda b, i: (b, i, 0)),
        out_shape=jax.ShapeDtypeStruct((bsz, seq, d), F32),
        scratch_shapes=[pltpu.VMEM((d // LANES, 2 * th, LANES), F32)] * 2,
        compiler_params=_cparams("arbitrary", "arbitrary"),
    )(a.reshape(bsz, 2, half, d), x, w, b.reshape(1, d), gain.reshape(1, d), mods)


def _mlp_kernel(x_ref, pre_ref, sc_ref, sh_ref, g_ref, w1_ref, w2_ref, post_ref, o_ref, *, ff_chunk, sub_rows):
    d_ff = w1_ref.shape[-1]
    mult = pre_ref[...] * (1.0 + sc_ref[0, 0])
    sh = sh_ref[0, 0]

    def rows_of(i):
        return slice(i * sub_rows, (i + 1) * sub_rows)

    def hidden(i):
        return (_rms(x_ref[rows_of(i), :]) * mult + sh).astype(BF16)

    def mix(h):
        acc = None
        for c in range(d_ff // ff_chunk):
            cols = pl.ds(c * ff_chunk, ff_chunk)
            a = jnp.maximum(_dot(h, w1_ref[:, cols]), 0.0)
            part = _dot((a * a).astype(BF16), w2_ref[cols, :])
            acc = part if acc is None else acc + part
        return acc

    def finish(i, acc):
        o_ref[rows_of(i), :] = x_ref[rows_of(i), :] + g_ref[0, 0] * (_rms(acc) * post_ref[...])

    n_sub = x_ref.shape[0] // sub_rows
    pending = mix(hidden(0))
    for i in range(1, n_sub):
        nxt = mix(hidden(i))
        finish(i - 1, pending)
        pending = nxt
    finish(n_sub - 1, pending)


def _mlp(x, pre, post, mods, mod_row_fn, w1, w2, layer, *, tm, ff_chunk=1024, sub_rows=512):
    rows, d = x.shape
    d_ff = w1.shape[-1]
    kern = functools.partial(_mlp_kernel, ff_chunk=ff_chunk, sub_rows=min(sub_rows, tm))
    vec = pl.BlockSpec((1, d), lambda i: (0, 0))
    return pl.pallas_call(
        kern,
        grid=(rows // tm,),
        in_specs=[
            pl.BlockSpec((tm, d), lambda i: (i, 0)),
            vec,
            _mod_block(d, 4, mod_row_fn),
            _mod_block(d, 3, mod_row_fn),
            _mod_block(d, 5, mod_row_fn),
            _const_spec((d, d_ff), layer),
            _const_spec((d_ff, d), layer),
            vec,
        ],
        out_specs=pl.BlockSpec((tm, d), lambda i: (i, 0)),
        out_shape=jax.ShapeDtypeStruct((rows, d), F32),
        compiler_params=_cparams("arbitrary"),
    )(x, pre.reshape(1, d), mods, mods, mods, w1, w2, post.reshape(1, d))


def _hyena_out_mlp_kernel(a_ref, x_ref, wo_ref, bo_ref, mixpost_ref, g1_ref, pre_ref, sc_ref, sh_ref, g2_ref,
                          w1_ref, w2_ref, post_ref, o_ref, slabs, *, ff_chunk):
    th = a_ref.shape[2]
    d_ff = w1_ref.shape[-1]
    n_slabs = slabs.shape[0]
    for k in range(n_slabs):
        slabs[k] = x_ref[0, :, k * LANES:(k + 1) * LANES]
    parity_rows = [pl.ds(parity, th, stride=2) for parity in range(2)]
    x = jnp.concatenate([jnp.concatenate([slabs[k, rows, :] for k in range(n_slabs)], axis=1)
                         for rows in parity_rows], axis=0)
    a = jnp.concatenate([a_ref[0, 0], a_ref[0, 1]], axis=0)
    y = _dot(a, wo_ref[...]) + bo_ref[...]
    x1 = x + g1_ref[0, 0] * (_rms(y) * mixpost_ref[...])
    h = _norm_mod(x1, pre_ref[...], sc_ref[0, 0], sh_ref[0, 0]).astype(BF16)
    acc = None
    for c in range(d_ff // ff_chunk):
        cols = pl.ds(c * ff_chunk, ff_chunk)
        t = jnp.maximum(_dot(h, w1_ref[:, cols]), 0.0)
        part = _dot((t * t).astype(BF16), w2_ref[cols, :])
        acc = part if acc is None else acc + part
    out = x1 + g2_ref[0, 0] * (_rms(acc) * post_ref[...])
    for parity, rows in enumerate(parity_rows):
        for k in range(n_slabs):
            slabs[k, rows, :] = out[parity * th:(parity + 1) * th, k * LANES:(k + 1) * LANES]
    for k in range(n_slabs):
        o_ref[0, :, k * LANES:(k + 1) * LANES] = slabs[k]


def _hyena_out_mlp(a, x, w_out, b_out, mix_post, pre, post, mods, mod_row_fn, w1, w2, layer, *, th=256,
                   ff_chunk=1024):
    bsz, seq, d = x.shape
    half = seq // 2
    d_ff = w1.shape[-1]
    th = min(th, half)
    kern = functools.partial(_hyena_out_mlp_kernel, ff_chunk=ff_chunk)
    vec = pl.BlockSpec((1, d), lambda b, i: (0, 0))
    mod = lambda chunk: _mod_block(d, chunk, lambda b, i: mod_row_fn(b))
    rows = pl.BlockSpec((1, 2 * th, d), lambda b, i: (b, i, 0))
    return pl.pallas_call(
        kern,
        grid=(bsz, half // th),
        in_specs=[
            pl.BlockSpec((1, 2, th, d), lambda b, i: (b, 0, i, 0)),
            rows,
            _const_spec((d, d)), vec, vec, mod(2),
            vec, mod(4), mod(3), mod(5),
            _const_spec((d, d_ff), layer), _const_spec((d_ff, d), layer), vec,
        ],
        out_specs=rows,
        out_shape=jax.ShapeDtypeStruct((bsz, seq, d), F32),
        scratch_shapes=[pltpu.VMEM((d // LANES, 2 * th, LANES), F32)],
        compiler_params=_cparams("arbitrary", "arbitrary"),
    )(a.reshape(bsz, 2, half, d), x, w_out, b_out.reshape(1, d), mix_post.reshape(1, d), mods,
      pre.reshape(1, d), mods, mods, mods, w1, w2, post.reshape(1, d))


def _qkv_kernel(x_ref, pre_ref, sc_ref, sh_ref, w_ref, gains_ref, cos_ref, sin_ref,
                *out_refs, n_q_pairs, rope, sub_rows):
    q_ref = out_refs[0] if n_q_pairs else None
    k_ref, v_ref = out_refs[-2:]
    hd = HEAD_DIM
    tm = x_ref.shape[0]
    mult = pre_ref[...] * (1.0 + sc_ref[0, 0])
    sh = sh_ref[0, 0]
    q_scale = ATTN_SCALE * LOG2_E
    gain_ab = [(gains_ref[0:1, :] * q_scale, gains_ref[1:2, :] * q_scale), (gains_ref[2:3, :], gains_ref[3:4, :])]
    half_r = lax.broadcasted_iota(jnp.int32, (2 * hd, 2 * hd), 0) // (hd // 2)
    half_c = lax.broadcasted_iota(jnp.int32, (2 * hd, 2 * hd), 1) // (hd // 2)
    same_head = (half_r == half_c).astype(BF16)

    def rows_of(i):
        return slice(i * sub_rows, (i + 1) * sub_rows)

    def project(i):
        h = (_rms(x_ref[rows_of(i), :]) * mult + sh).astype(BF16)
        return _dot(h, w_ref[...])

    def finish(i, qkv):
        rows = rows_of(i)
        if rope:
            cos_t, sin_t = cos_ref[rows, :], sin_ref[rows, :]
            tabs = [(ga * cos_t, gb * sin_t, ga * sin_t, gb * cos_t) for ga, gb in gain_ab]

        n_pairs = n_q_pairs + 1
        squares = [(qkv[:, 2 * p * hd:(2 * p + 1) * hd] ** 2 + qkv[:, (2 * p + 1) * hd:(2 * p + 2) * hd] ** 2)
                   .astype(BF16) for p in range(n_pairs)]
        sums = []
        for p in range(0, n_pairs - 1, 2):
            both = _dot(jnp.concatenate(squares[p:p + 2], axis=1), same_head)
            sums += [both[:, 0:hd], both[:, hd:2 * hd]]
        if n_pairs % 2:
            sums.append(_dot(squares[-1], same_head[0:hd, 0:hd]))

        def pair(idx, kind):
            a = qkv[:, 2 * idx * hd:(2 * idx + 1) * hd]
            b = qkv[:, (2 * idx + 1) * hd:(2 * idx + 2) * hd]
            r = lax.rsqrt(sums[idx] * (1.0 / hd) + EPS)
            if rope:
                ca, sb, sa, cb = tabs[kind]
                return (r * (a * ca - b * sb)).astype(BF16), (r * (a * sa + b * cb)).astype(BF16)
            ga, gb = gain_ab[kind]
            return (r * (a * ga)).astype(BF16), (r * (b * gb)).astype(BF16)

        for p in range(n_q_pairs):
            qa, qb = pair(p, 0)
            q_ref[rows, 2 * p * hd:(2 * p + 1) * hd] = qa
            q_ref[rows, (2 * p + 1) * hd:(2 * p + 2) * hd] = qb
        ka, kb = pair(n_q_pairs, 1)
        k_ref[rows, 0:hd] = ka
        k_ref[rows, hd:2 * hd] = kb
        v0 = 2 * (n_q_pairs + 1) * hd
        v_ref[rows, :] = qkv[:, v0:v0 + N_KV_HEADS * hd].astype(BF16)

    n_sub = tm // sub_rows
    pending = project(0)
    for i in range(1, n_sub):
        nxt = project(i)
        finish(i - 1, pending)
        pending = nxt
    finish(n_sub - 1, pending)


def _qkv(x, pre, mods, mod_row_fn, w, gains, cos_t, sin_t, *, n_q_pairs, rope, tm, seq):
    rows, d = x.shape
    hd = HEAD_DIM
    n = w.shape[-1]
    kern = functools.partial(_qkv_kernel, n_q_pairs=n_q_pairs, rope=rope, sub_rows=min(tm, 256))
    vec = pl.BlockSpec((1, d), lambda i: (0, 0))
    tiles_per_seq = seq // tm
    pos = pl.BlockSpec((tm, hd), lambda i: (i % tiles_per_seq, 0))
    widths = ([2 * n_q_pairs * hd] if n_q_pairs else []) + [N_KV_HEADS * hd] * 2
    return pl.pallas_call(
        kern,
        grid=(rows // tm,),
        in_specs=[
            pl.BlockSpec((tm, d), lambda i: (i, 0)),
            vec,
            _mod_block(d, 1, mod_row_fn),
            _mod_block(d, 0, mod_row_fn),
            _const_spec((d, n)),
            pl.BlockSpec((4, hd), lambda i: (0, 0)),
            pos, pos,
        ],
        out_specs=[pl.BlockSpec((tm, wd), lambda i: (i, 0)) for wd in widths],
        out_shape=[jax.ShapeDtypeStruct((rows, wd), BF16) for wd in widths],
        compiler_params=_cparams("arbitrary"),
    )(x, pre.reshape(1, d), mods, mods, w, gains, cos_t, sin_t)


def _attn_kernel(q_ref, kl_ref, vl_ref, kc_ref, vc_ref, x_ref, wo_ref, gain_ref, g_ref, o_ref, k_all, v_ext,
                 *, n_pairs, kv_blocks):
    hd = HEAD_DIM
    tq = q_ref.shape[1]
    seq = kl_ref.shape[1]
    total = k_all.shape[0]

    @pl.when(pl.program_id(1) == 0)
    def _():
        k_all[0:seq, :] = kl_ref[0]
        k_all[seq:total, :] = kc_ref[0]
        for h in range(N_KV_HEADS):
            v_ext[h, 0:seq, 0:hd] = vl_ref[0, :, h * hd:(h + 1) * hd]
            v_ext[h, seq:total, 0:hd] = vc_ref[0, :, h * hd:(h + 1) * hd]
            v_ext[h, :, hd:2 * hd] = jnp.ones((total, hd), BF16)

    nt = (((1,), (1,)), ((), ()))
    lane_half = (lax.broadcasted_iota(jnp.int32, (1, 2 * hd), 1) % hd) // (hd // 2)

    def start(kv_head):
        keep = (lane_half == kv_head).astype(BF16)
        q = jnp.concatenate([q_ref[0, :, 2 * p * hd:(2 * p + 2) * hd] * keep for p in range(n_pairs)], axis=0)
        st = dict(h=kv_head, q=q, m=None, acc=None)
        st["s_next"] = lax.dot_general(q, k_all[kv_blocks[0][0]:kv_blocks[0][1], :], nt, preferred_element_type=F32)
        return st

    def step(st, i):
        s0, s1 = kv_blocks[i]
        s = st["s_next"]
        if i + 1 < len(kv_blocks):
            n0, n1 = kv_blocks[i + 1]
            st["s_next"] = lax.dot_general(st["q"], k_all[n0:n1, :], nt, preferred_element_type=F32)
        m, acc = st["m"], st["acc"]
        m_blk = jnp.max(s, axis=-1, keepdims=True)
        m_new = m_blk if m is None else jnp.maximum(m, m_blk)
        pv = _dot(jnp.exp2(s - m_new).astype(BF16), v_ext[st["h"], s0:s1, :])
        st["acc"] = pv if m is None else acc * jnp.exp2(m - m_new) + pv
        st["m"] = m_new

    def heads_of(st):
        acc = st["acc"]
        o = (acc[:, 0:hd] / acc[:, hd:2 * hd]).astype(BF16)
        return [o[p * tq:(p + 1) * tq, :] for p in range(n_pairs)]

    last = len(kv_blocks) - 1
    first, second = start(0), None
    for i in range(last):
        step(first, i)
    second = start(1)
    step(first, last)
    for i in range(last + 1):
        step(second, i)
    attn = jnp.concatenate(heads_of(first) + heads_of(second), axis=1)
    y = _dot(attn, wo_ref[...])
    o_ref[0] = x_ref[0] + g_ref[0, 0] * (_rms(y) * gain_ref[...])


def _kv_blocks(total, pattern):
    assert total % MXU_WIDTH == 0
    n_tiles = total // MXU_WIDTH
    blocks, start, i = [], 0, 0
    while start < n_tiles:
        size = min(pattern[i % len(pattern)], n_tiles - start)
        blocks.append((start * MXU_WIDTH, (start + size) * MXU_WIDTH))
        start += size
        i += 1
    return tuple(blocks)


def _attention(q, k_l, v_l, k_c, v_c, x, w_o, gain, mods, gate_chunk, mod_row_fn, *, tq=512, kv_pattern=(5, 3, 1)):
    bsz, seq, dq = q.shape
    d = x.shape[-1]
    ctx_len = k_c.shape[1]
    hd = HEAD_DIM
    kv = N_KV_HEADS * hd
    n_pairs = dq // (2 * hd)
    total = seq + ctx_len
    kern = functools.partial(_attn_kernel, n_pairs=n_pairs, kv_blocks=_kv_blocks(total, kv_pattern))
    whole = lambda rows, width: pl.BlockSpec((1, rows, width), lambda b, i: (b, 0, 0))
    tile = lambda width: pl.BlockSpec((1, tq, width), lambda b, i: (b, i, 0))
    return pl.pallas_call(
        kern,
        grid=(bsz, seq // tq),
        in_specs=[
            tile(dq), whole(seq, 2 * hd), whole(seq, kv), whole(ctx_len, 2 * hd), whole(ctx_len, kv),
            tile(d),
            _const_spec((dq, d)),
            pl.BlockSpec((1, d), lambda b, i: (0, 0)),
            _mod_block(d, gate_chunk, lambda b, i: mod_row_fn(b)),
        ],
        out_specs=tile(d),
        out_shape=jax.ShapeDtypeStruct((bsz, seq, d), F32),
        scratch_shapes=[pltpu.VMEM((total, 2 * hd), BF16), pltpu.VMEM((N_KV_HEADS, total, 2 * hd), BF16)],
        compiler_params=_cparams("arbitrary", "arbitrary"),
    )(q, k_l, v_l, k_c, v_c, x, w_o, gain.reshape(1, d), mods)


def _rope_tables(seq):
    rows = seq // GRID_W
    pairs = HEAD_DIM // 4
    row = jnp.repeat(jnp.arange(rows, dtype=F32), GRID_W)
    col = jnp.tile(jnp.arange(GRID_W, dtype=F32), rows)
    inv = ROPE_THETA ** (-jnp.arange(pairs, dtype=F32) / pairs)
    ang = jnp.concatenate([row[:, None] * inv[None, :], col[:, None] * inv[None, :]], axis=-1)
    cos, sin = jnp.cos(ang), jnp.sin(ang)
    return jnp.concatenate([cos, cos], axis=-1), jnp.concatenate([sin, sin], axis=-1)


def _pair_columns(head_a, head_b):
    hd = HEAD_DIM
    even, odd = jnp.arange(0, hd, 2), jnp.arange(1, hd, 2)
    return jnp.concatenate([head_a * hd + even, head_b * hd + even, head_a * hd + odd, head_b * hd + odd])


def kernel(x, c, ctx, c_ctx, mod_w, mod_b, mix_norm_pre, mix_norm_post, mlp_norm_pre, mlp_norm_post, mlp_w1, mlp_w2, hy_w_in, hy_b_in, hy_conv_w, hy_conv_b, hy_filt_w1, hy_filt_b1, hy_filt_freq1, hy_filt_w2, hy_filt_b2, hy_filt_freq2, hy_filt_w3, hy_filt_bias, hy_w_out, hy_b_out, attn_w_qkv, attn_q_norm, attn_k_norm, attn_w_o):
    bsz, seq, d = x.shape
    ctx_len = ctx.shape[1]
    hd = HEAD_DIM
    n_heads = d // hd
    tm = 1024

    ctx_row = bsz
    n_rows = -(-(bsz + 1) // MOD_ROWS_PAD) * MOD_ROWS_PAD
    cond = jnp.concatenate([c, c_ctx[None, :], jnp.zeros((n_rows - bsz - 1, d), F32)], axis=0)
    mods_all = _modulation(cond, mod_w, mod_b).reshape(mod_w.shape[0], n_rows, 6, 1, d)

    x_row = lambda i: i // (seq // tm)
    c_row = lambda i: ctx_row

    xf = x.reshape(bsz * seq, d)
    cf = ctx.reshape(bsz * ctx_len, d)

    mods = mods_all[0]
    w_in = hy_w_in[0].astype(BF16)
    w_out = hy_w_out[0].astype(BF16)
    filt = (hy_filt_w1[0], hy_filt_b1[0], hy_filt_freq1[0], hy_filt_w2[0], hy_filt_b2[0], hy_filt_freq2[0],
            hy_filt_w3[0], hy_filt_bias[0])
    w1 = _to_bf16(mlp_w1)
    w2 = _to_bf16(mlp_w2)

    def hyena_conv(tokens, mod_row_b):
        u, x1 = _hyena_in(tokens, mix_norm_pre[0], mods, mod_row_b, w_in, hy_b_in[0], hy_conv_w[0], hy_conv_b[0])
        return _fftconv(u, x1, _hyena_filter_tables(tokens.shape[1], *filt))

    x3 = xf.reshape(bsz, seq, d)
    xf = _hyena_out_mlp(hyena_conv(x3, lambda b: b), x3, w_out, hy_b_out[0], mix_norm_post[0],
                        mlp_norm_pre[0], mlp_norm_post[0], mods, lambda b: b, w1, w2, 0).reshape(bsz * seq, d)
    c3 = cf.reshape(bsz, ctx_len, d)
    cf = _proj_res_parity(hyena_conv(c3, lambda b: ctx_row), c3, w_out, hy_b_out[0], mix_norm_post[0], mods, 2,
                          lambda b: ctx_row, th=512).reshape(bsz * ctx_len, d)
    cf = _mlp(cf, mlp_norm_pre[0], mlp_norm_post[0], mods, c_row, w1, w2, 0, tm=min(tm, bsz * ctx_len))

    mods = mods_all[1]
    w_qkv = attn_w_qkv[0]
    assert N_KV_HEADS == 2 and n_heads % 2 == 0
    group = n_heads // N_KV_HEADS
    pair_cols = [_pair_columns(p, group + p) for p in range(group)] + [_pair_columns(n_heads, n_heads + 1)]
    v_cols = jnp.arange((n_heads + N_KV_HEADS) * hd, (n_heads + 2 * N_KV_HEADS) * hd)
    w_lat = w_qkv[:, jnp.concatenate(pair_cols + [v_cols])].astype(BF16)
    w_ctx = w_lat[:, n_heads * hd:]
    even, odd = jnp.arange(0, hd, 2), jnp.arange(1, hd, 2)
    gains = jnp.stack([jnp.tile(g[idx], 2) for g in (attn_q_norm[0], attn_k_norm[0]) for idx in (even, odd)])
    cos_t, sin_t = _rope_tables(seq)

    q, k_l, v_l = _qkv(xf, mix_norm_pre[1], mods, x_row, w_lat, gains, cos_t, sin_t,
                       n_q_pairs=group, rope=True, tm=tm, seq=seq)
    k_c, v_c = _qkv(cf, mix_norm_pre[1], mods, c_row, w_ctx, gains, cos_t, sin_t,
                    n_q_pairs=0, rope=False, tm=min(tm, bsz * ctx_len), seq=min(tm, bsz * ctx_len))
    kv = N_KV_HEADS * hd
    xf = _attention(q.reshape(bsz, seq, d), k_l.reshape(bsz, seq, kv), v_l.reshape(bsz, seq, kv),
                    k_c.reshape(bsz, ctx_len, kv), v_c.reshape(bsz, ctx_len, kv), xf.reshape(bsz, seq, d),
                    attn_w_o[0].astype(BF16), mix_norm_post[1], mods, 2, lambda b: b).reshape(bsz * seq, d)
    xf = _mlp(xf, mlp_norm_pre[1], mlp_norm_post[1], mods, x_row, w1, w2, 1, tm=tm)
    return xf.reshape(bsz, seq, d)
```

```python
import functools
import math

import jax
import jax.numpy as jnp
from jax import lax
from jax.experimental import pallas as pl
from jax.experimental.pallas import tpu as pltpu

F32 = jnp.float32
BF16 = jnp.bfloat16

EPS = 1e-6
GRID_W = 64
HY_BANDS = 16
HY_DECAY_TARGET = 1e-2
HY_FAST = 0.3
HY_SLOW = 1.5
HY_SHIFT = 0.0
HEAD_DIM = 128
N_KV_HEADS = 2
ROPE_THETA = 10000.0
ATTN_SCALE = HEAD_DIM ** -0.5
LOG2_E = math.log2(math.e)

MOD_ROWS_PAD = 8
SUBLANES = 8
LANES = 128
MXU_WIDTH = 256

VMEM_LIMIT = 56 * 1024 * 1024


def _cparams(*sem):
    return pltpu.CompilerParams(dimension_semantics=sem, vmem_limit_bytes=VMEM_LIMIT)


def _const_spec(shape, layer=None):
    nd = len(shape)
    if layer is None:
        return pl.BlockSpec(shape, lambda *_: (0,) * nd, pipeline_mode=pl.Buffered(1))
    return pl.BlockSpec((None,) + tuple(shape), lambda *_: (layer,) + (0,) * nd, pipeline_mode=pl.Buffered(1))


def _cast_kernel(w_ref, o_ref):
    o_ref[...] = w_ref[...].astype(BF16)


def _to_bf16(w, *, block_elems=1024 * 1024):
    depth, r, c = w.shape
    block_rows = min(r, max(SUBLANES, block_elems // c))
    spec = pl.BlockSpec((1, block_rows, c), lambda l, i: (l, i, 0))
    return pl.pallas_call(
        _cast_kernel,
        grid=(depth, r // block_rows),
        in_specs=[spec],
        out_specs=spec,
        out_shape=jax.ShapeDtypeStruct(w.shape, BF16),
        compiler_params=_cparams("arbitrary", "arbitrary"),
    )(w)


def _rms(x):
    return x * lax.rsqrt(jnp.mean(x * x, axis=-1, keepdims=True) + EPS)


def _norm_mod(x, gain, sc, sh):
    return _rms(x) * (gain * (1.0 + sc)) + sh


def _dot(a, b):
    return jnp.dot(a, b, preferred_element_type=F32)


def _dot_hi(a, b):
    return jnp.dot(a, b, preferred_element_type=F32, precision=lax.Precision.HIGHEST)


def _mod_kernel(cond_ref, w_ref, b_ref, o_ref):
    s = cond_ref[...]
    s = s * jax.nn.sigmoid(s)
    o_ref[0] = _dot(s.astype(BF16), w_ref[0].astype(BF16)) + b_ref[0]


def _modulation(cond, mod_w, mod_b):
    depth, d, n = mod_w.shape
    rows = cond.shape[0]
    tn = d
    return pl.pallas_call(
        _mod_kernel,
        grid=(depth, n // tn),
        in_specs=[
            pl.BlockSpec((rows, d), lambda i, j: (0, 0)),
            pl.BlockSpec((1, d, tn), lambda i, j: (i, 0, j)),
            pl.BlockSpec((1, 1, tn), lambda i, j: (i, 0, j)),
        ],
        out_specs=pl.BlockSpec((1, rows, tn), lambda i, j: (i, 0, j)),
        out_shape=jax.ShapeDtypeStruct((depth, rows, n), F32),
        compiler_params=_cparams("arbitrary", "arbitrary"),
    )(cond, mod_w, mod_b.reshape(depth, 1, n))


def _mod_block(d, chunk, row_fn):
    return pl.BlockSpec((1, 1, 1, d), lambda *idx: (row_fn(*idx), chunk, 0, 0))


def _hyena_in_kernel(x_ref, gain_ref, sc_ref, sh_ref, w0_ref, w1_ref, w2_ref, b_ref, cw_ref, cb_ref,
                     u_ref, x1_ref, x_slabs, *, seq, rc):
    w = u_ref.shape[-1]
    half = seq // 2
    n_chunks = half // rc
    n_slabs = x_slabs.shape[0]
    mult = gain_ref[...] * (1.0 + sc_ref[0, 0])
    sh = sh_ref[0, 0]
    sub = lax.broadcasted_iota(jnp.int32, (SUBLANES, w), 0)
    w_refs = (w0_ref, w1_ref, w2_ref)
    taps = [[cw_ref[k, s:s + 1, :] for s in range(3)] for k in range(3)]
    bias = [cb_ref[s:s + 1, :] + b_ref[s:s + 1, :] * (taps[0][s] + taps[1][s] + taps[2][s]) for s in range(3)]

    def project(c):
        for k in range(n_slabs):
            x_slabs[k] = x_ref[0, 2 * c * rc:2 * (c + 1) * rc, k * LANES:(k + 1) * LANES]
        xs = jnp.concatenate([jnp.concatenate([x_slabs[k, pl.ds(parity, rc, stride=2), :] for k in range(n_slabs)],
                                              axis=1) for parity in range(2)], axis=0)
        h = (_rms(xs) * mult + sh).astype(BF16)
        return [_dot(h, w_refs[s][...]) for s in range(3)]

    def conv(c, m_before, m_here, m_after):
        even_streams, odd_streams = [], []
        for s in range(3):
            m_even, m_odd = m_here[s][0:rc], m_here[s][rc:2 * rc]
            row_before = -b_ref[s:s + 1, :] if m_before is None else m_before[s][2 * rc - 1:2 * rc, :]
            row_after = -b_ref[s:s + 1, :] if m_after is None else m_after[s][0:1, :]
            odd_prev = pltpu.roll(m_odd, 1, 0)
            odd_prev = jnp.concatenate([jnp.where(sub == 0, row_before, odd_prev[0:SUBLANES]),
                                        odd_prev[SUBLANES:]], axis=0)
            even_next = pltpu.roll(m_even, rc - 1, 0)
            even_next = jnp.concatenate([even_next[:rc - SUBLANES],
                                         jnp.where(sub == SUBLANES - 1, row_after, even_next[rc - SUBLANES:])], axis=0)
            even_streams.append(odd_prev * taps[0][s] + m_even * taps[1][s] + m_odd * taps[2][s] + bias[s])
            odd_streams.append(m_even * taps[0][s] + m_odd * taps[1][s] + even_next * taps[2][s] + bias[s])
        for base, (x1, x2, v) in ((0, even_streams), (half, odd_streams)):
            u_ref[0, base + c * rc:base + (c + 1) * rc, :] = (v * x2).astype(BF16)
            x1_ref[0, base + c * rc:base + (c + 1) * rc, :] = x1.astype(BF16)

    zs = [project(0)]
    for c in range(1, n_chunks):
        zs.append(project(c))
        conv(c - 1, zs[c - 2] if c >= 2 else None, zs[c - 1], zs[c])
    conv(n_chunks - 1, zs[n_chunks - 2] if n_chunks >= 2 else None, zs[n_chunks - 1], None)


def _hyena_in(x, gain, mods, mod_row_fn, w_in, b_in, conv_w, conv_b, *, rc=128):
    bsz, seq, d = x.shape
    rc = min(rc, seq // 2)
    kern = functools.partial(_hyena_in_kernel, seq=seq, rc=rc)
    wspec = lambda s: pl.BlockSpec((d, d), lambda b: (0, s), pipeline_mode=pl.Buffered(1))
    small = lambda shape: pl.BlockSpec(shape, lambda b: (0,) * len(shape))
    out_spec = pl.BlockSpec((1, seq, d), lambda b: (b, 0, 0))
    return pl.pallas_call(
        kern,
        grid=(bsz,),
        in_specs=[
            pl.BlockSpec((1, seq, d), lambda b: (b, 0, 0)),
            small((1, d)),
            _mod_block(d, 1, mod_row_fn),
            _mod_block(d, 0, mod_row_fn),
            wspec(0), wspec(1), wspec(2),
            small((3, d)), small((3, 3, d)), small((3, d)),
        ],
        out_specs=[out_spec, out_spec],
        out_shape=[jax.ShapeDtypeStruct((bsz, seq, d), BF16)] * 2,
        scratch_shapes=[pltpu.VMEM((d // LANES, 2 * rc, LANES), F32)],
        compiler_params=_cparams("arbitrary"),
    )(x, gain.reshape(1, d), mods, mods, w_in, w_in, w_in,
      b_in.reshape(3, d), conv_w.reshape(3, 3, d), conv_b.reshape(3, d))


def _dft_table(length, n_freq, shifted, kb=32):
    if shifted:
        period = 8 * length
        nn = 2 * jnp.arange(length, dtype=jnp.int32)[None, :] + 1
    else:
        period = 4 * length
        nn = jnp.arange(length, dtype=jnp.int32)[None, :]
    kh = jnp.arange(n_freq // kb, dtype=jnp.int32)[:, None]
    kl = jnp.arange(kb, dtype=jnp.int32)[:, None]
    to_angle = lambda m: (m % period).astype(F32) * (2.0 * math.pi / period)
    alpha = to_angle(2 * kb * kh * nn)[:, None, :]
    beta = to_angle((2 * kl + 1) * nn)[None, :, :]
    ca, sa, cb, sb = jnp.cos(alpha), jnp.sin(alpha), jnp.cos(beta), jnp.sin(beta)
    cos_t = (ca * cb - sa * sb).reshape(n_freq, length)
    sin_t = (sa * cb + ca * sb).reshape(n_freq, length)
    return jnp.concatenate([cos_t, sin_t], axis=0).astype(BF16)


def _filter_kernel(bands_ref, w1t_ref, w1c_ref, w1s_ref, b1_ref, f1_ref, w2_ref, b2_ref, f2_ref,
                   w3f_ref, w3b_ref, deltas_ref, bias_ref, psi_ref, tab_ref, o_ref, h_buf, *, seq):
    w = o_ref.shape[-1]
    half = seq // 2

    @pl.when(pl.program_id(0) == 0)
    def _():
        t64 = lax.broadcasted_iota(jnp.int32, (seq, w1t_ref.shape[-1]), 0).astype(F32) / seq
        t16 = lax.broadcasted_iota(jnp.int32, (seq, HY_BANDS), 0).astype(F32) / seq
        ang = (2.0 * math.pi * t16) * bands_ref[...]
        pre = t64 * w1t_ref[...] + _dot_hi(jnp.cos(ang), w1c_ref[...]) + _dot_hi(jnp.sin(ang), w1s_ref[...])
        h = jnp.sin(f1_ref[...] * (pre + b1_ref[...]))
        h_buf[...] = jnp.sin(f2_ref[...] * (_dot_hi(h, w2_ref[...]) + b2_ref[...]))

    h = h_buf[...]
    row = lax.broadcasted_iota(jnp.int32, (seq, w), 0)
    decay = jnp.exp(-(row.astype(F32) / seq) * deltas_ref[...]) + HY_SHIFT
    h_f = _dot_hi(h, w3f_ref[...]) * decay
    h_b = _dot_hi(h, w3b_ref[...]) * decay
    h_b = jnp.where(row == 0, 0.0, h_b)
    h_sum = h_f + h_b
    h_dif = h_b - h_f
    alt = jnp.where(row % 2 == 0, 1.0, -1.0)
    cos_lo = tab_ref[pl.ds(0, half), :]
    sin_lo = tab_ref[pl.ds(half, half), :]
    bias = bias_ref[...]
    p_lo = _dot(cos_lo, h_sum.astype(BF16)) + bias
    p_hi = _dot(cos_lo, (h_sum * alt).astype(BF16)) + bias
    q_lo = _dot(sin_lo, h_dif.astype(BF16))
    q_hi = -_dot(sin_lo, (h_dif * alt).astype(BF16))
    d_re = p_lo - p_hi
    d_im = q_lo + q_hi
    psi_c = jnp.concatenate([psi_ref[0]] * (w // psi_ref.shape[-1]), axis=1)
    psi_s = jnp.concatenate([psi_ref[1]] * (w // psi_ref.shape[-1]), axis=1)
    o_ref[0] = p_lo + p_hi
    o_ref[1] = q_lo - q_hi
    o_ref[2] = d_re * psi_c - d_im * psi_s
    o_ref[3] = d_re * psi_s + d_im * psi_c
    o_ref[4] = d_re * psi_c + d_im * psi_s
    o_ref[5] = d_im * psi_c - d_re * psi_s


def _hyena_filter_tables(seq, fw1, fb1, ff1, fw2, fb2, ff2, fw3, fbias, *, w=256):
    d = fbias.shape[-1]
    fwid = fw2.shape[0]
    half = seq // 2
    lanes = LANES
    bands = jnp.linspace(1e-4, HY_BANDS - 1, HY_BANDS, dtype=F32).reshape(1, HY_BANDS)
    deltas = jnp.abs(jnp.linspace(math.log(HY_DECAY_TARGET) / HY_SLOW, math.log(HY_DECAY_TARGET) / HY_FAST,
                                  d, dtype=F32)).reshape(1, d)
    theta = (2 * jnp.arange(half, dtype=F32) + 1.0) * (math.pi / (2 * seq))
    psi = jnp.broadcast_to(jnp.stack([jnp.cos(theta), jnp.sin(theta)])[:, :, None], (2, half, lanes))
    tab_plain = _dft_table(seq, half, shifted=False)
    nj = d // w
    small = lambda shape: pl.BlockSpec(shape, lambda j: (0,) * len(shape))
    kern = functools.partial(_filter_kernel, seq=seq)
    return pl.pallas_call(
        kern,
        grid=(nj,),
        in_specs=[
            small((1, HY_BANDS)), small((1, fwid)), small((HY_BANDS, fwid)), small((HY_BANDS, fwid)),
            small((1, fwid)), small((1, fwid)), small((fwid, fwid)), small((1, fwid)), small((1, fwid)),
            pl.BlockSpec((fwid, w), lambda j: (0, j)),
            pl.BlockSpec((fwid, w), lambda j: (0, nj + j)),
            pl.BlockSpec((1, w), lambda j: (0, j)),
            pl.BlockSpec((1, w), lambda j: (0, j)),
            small((2, half, lanes)),
            _const_spec((2 * half, seq)),
        ],
        out_specs=pl.BlockSpec((6, half, w), lambda j: (0, 0, j)),
        out_shape=jax.ShapeDtypeStruct((6, half, d), F32),
        scratch_shapes=[pltpu.VMEM((seq, fwid), F32)],
        compiler_params=_cparams("arbitrary"),
    )(bands, fw1[0:1], fw1[1:1 + HY_BANDS], fw1[1 + HY_BANDS:], fb1.reshape(1, fwid), ff1.reshape(1, fwid),
      fw2, fb2.reshape(1, fwid), ff2.reshape(1, fwid), fw3, fw3, deltas, fbias.reshape(1, d), psi, tab_plain)


def _fftconv_kernel(u_ref, x1_ref, sef_ref, tab_ref, o_ref, *, seq):
    half = seq // 2
    cos_t = tab_ref[pl.ds(0, half), :]
    sin_t = tab_ref[pl.ds(half, half), :]
    u0 = u_ref[0, 0:half, :]
    u1 = u_ref[0, half:seq, :]
    a0, b0 = _dot(cos_t, u0), _dot(sin_t, u0)
    a1, b1 = _dot(cos_t, u1), _dot(sin_t, u1)
    s_re, s_im, e_re, e_im, f_re, f_im = (sef_ref[i] for i in range(6))
    v0_re = (s_re * a0 + s_im * b0 + f_re * a1 + f_im * b1).astype(BF16)
    v0_im = (s_im * a0 - s_re * b0 + f_im * a1 - f_re * b1).astype(BF16)
    v1_re = (e_re * a0 + e_im * b0 + s_re * a1 + s_im * b1).astype(BF16)
    v1_im = (e_im * a0 - e_re * b0 + s_im * a1 - s_re * b1).astype(BF16)
    y0 = _dot(cos_t, v0_re) - _dot(sin_t, v0_im)
    y1 = _dot(cos_t, v1_re) - _dot(sin_t, v1_im)
    o_ref[0, 0:half, :] = (y0 * (1.0 / seq) * x1_ref[0, 0:half, :].astype(F32)).astype(BF16)
    o_ref[0, half:seq, :] = (y1 * (1.0 / seq) * x1_ref[0, half:seq, :].astype(F32)).astype(BF16)


def _fftconv(u, x1, sef, *, block_elems=2048 * 256):
    bsz, seq, d = u.shape
    half = seq // 2
    w = min(d, max(MXU_WIDTH, block_elems // seq // MXU_WIDTH * MXU_WIDTH))
    nj = d // w
    act = pl.BlockSpec((1, seq, w), lambda j, b: (b, 0, j))
    kern = functools.partial(_fftconv_kernel, seq=seq)
    return pl.pallas_call(
        kern,
        grid=(nj, bsz),
        in_specs=[act, act, pl.BlockSpec((6, half, w), lambda j, b: (0, 0, j)), _const_spec((2 * half, half))],
        out_specs=act,
        out_shape=jax.ShapeDtypeStruct((bsz, seq, d), BF16),
        compiler_params=_cparams("arbitrary", "arbitrary"),
    )(u, x1, sef, _dft_table(half, half, shifted=True))


def _proj_res_parity_kernel(a_ref, x_ref, w_ref, b_ref, gain_ref, g_ref, o_ref, x_slabs, o_slabs):
    th = a_ref.shape[2]
    n_slabs = x_slabs.shape[0]
    for k in range(n_slabs):
        x_slabs[k] = x_ref[0, :, k * LANES:(k + 1) * LANES]
    for parity in range(2):
        rows = pl.ds(parity, th, stride=2)
        y = _dot(a_ref[0, parity], w_ref[...]) + b_ref[...]
        x_par = jnp.concatenate([x_slabs[k, rows, :] for k in range(n_slabs)], axis=1)
        out = x_par + g_ref[0, 0] * (_rms(y) * gain_ref[...])
        for k in range(n_slabs):
            o_slabs[k, rows, :] = out[:, k * LANES:(k + 1) * LANES]
    for k in range(n_slabs):
        o_ref[0, :, k * LANES:(k + 1) * LANES] = o_slabs[k]


def _proj_res_parity(a, x, w, b, gain, mods, gate_chunk, mod_row_fn, *, th):
    bsz, seq, d = x.shape
    half = seq // 2
    th = min(th, half)
    return pl.pallas_call(
        _proj_res_parity_kernel,
        grid=(bsz, half // th),
        in_specs=[
            pl.BlockSpec((1, 2, th, d), lambda b, i: (b, 0, i, 0)),
            pl.BlockSpec((1, 2 * th, d), lambda b, i: (b, i, 0)),
            _const_spec((d, d)),
            pl.BlockSpec((1, d), lambda b, i: (0, 0)),
            pl.BlockSpec((1, d), lambda b, i: (0, 0)),
            _mod_block(d, gate_chunk, lambda b, i: mod_row_fn(b)),
        ],
        out_specs=pl.BlockSpec((1, 2 * th, d), lambda b, i: (b, i, 0)),
        out_shape=jax.ShapeDtypeStruct((bsz, seq, d), F32),
        scratch_shapes=[pltpu.VMEM((d // LANES, 2 * th, LANES), F32)] * 2,
        compiler_params=_cparams("arbitrary", "arbitrary"),
    )(a.reshape(bsz, 2, half, d), x, w, b.reshape(1, d), gain.reshape(1, d), mods)


def _mlp_kernel(x_ref, pre_ref, sc_ref, sh_ref, g_ref, w1_ref, w2_ref, post_ref, o_ref, *, ff_chunk, sub_rows):
    d_ff = w1_ref.shape[-1]
    mult = pre_ref[...] * (1.0 + sc_ref[0, 0])
    sh = sh_ref[0, 0]

    def rows_of(i):
        return slice(i * sub_rows, (i + 1) * sub_rows)

    def hidden(i):
        return (_rms(x_ref[rows_of(i), :]) * mult + sh).astype(BF16)

    def mix(h):
        acc = None
        for c in range(d_ff // ff_chunk):
            cols = pl.ds(c * ff_chunk, ff_chunk)
            a = jnp.maximum(_dot(h, w1_ref[:, cols]), 0.0)
            part = _dot((a * a).astype(BF16), w2_ref[cols, :])
            acc = part if acc is None else acc + part
        return acc

    def finish(i, acc):
        o_ref[rows_of(i), :] = x_ref[rows_of(i), :] + g_ref[0, 0] * (_rms(acc) * post_ref[...])

    n_sub = x_ref.shape[0] // sub_rows
    pending = mix(hidden(0))
    for i in range(1, n_sub):
        nxt = mix(hidden(i))
        finish(i - 1, pending)
        pending = nxt
    finish(n_sub - 1, pending)


def _mlp(x, pre, post, mods, mod_row_fn, w1, w2, layer, *, tm, ff_chunk=1024, sub_rows=512):
    rows, d = x.shape
    d_ff = w1.shape[-1]
    kern = functools.partial(_mlp_kernel, ff_chunk=ff_chunk, sub_rows=min(sub_rows, tm))
    vec = pl.BlockSpec((1, d), lambda i: (0, 0))
    return pl.pallas_call(
        kern,
        grid=(rows // tm,),
        in_specs=[
            pl.BlockSpec((tm, d), lambda i: (i, 0)),
            vec,
            _mod_block(d, 4, mod_row_fn),
            _mod_block(d, 3, mod_row_fn),
            _mod_block(d, 5, mod_row_fn),
            _const_spec((d, d_ff), layer),
            _const_spec((d_ff, d), layer),
            vec,
        ],
        out_specs=pl.BlockSpec((tm, d), lambda i: (i, 0)),
        out_shape=jax.ShapeDtypeStruct((rows, d), F32),
        compiler_params=_cparams("arbitrary"),
    )(x, pre.reshape(1, d), mods, mods, mods, w1, w2, post.reshape(1, d))


def _hyena_out_mlp_kernel(a_ref, x_ref, wo_ref, bo_ref, mixpost_ref, g1_ref, pre_ref, sc_ref, sh_ref, g2_ref,
                          w1_ref, w2_ref, post_ref, o_ref, slabs, *, ff_chunk):
    th = a_ref.shape[2]
    d_ff = w1_ref.shape[-1]
    n_slabs = slabs.shape[0]
    for k in range(n_slabs):
        slabs[k] = x_ref[0, :, k * LANES:(k + 1) * LANES]
    parity_rows = [pl.ds(parity, th, stride=2) for parity in range(2)]
    x = jnp.concatenate([jnp.concatenate([slabs[k, rows, :] for k in range(n_slabs)], axis=1)
                         for rows in parity_rows], axis=0)
    a = jnp.concatenate([a_ref[0, 0], a_ref[0, 1]], axis=0)
    y = _dot(a, wo_ref[...]) + bo_ref[...]
    x1 = x + g1_ref[0, 0] * (_rms(y) * mixpost_ref[...])
    h = _norm_mod(x1, pre_ref[...], sc_ref[0, 0], sh_ref[0, 0]).astype(BF16)
    acc = None
    for c in range(d_ff // ff_chunk):
        cols = pl.ds(c * ff_chunk, ff_chunk)
        t = jnp.maximum(_dot(h, w1_ref[:, cols]), 0.0)
        part = _dot((t * t).astype(BF16), w2_ref[cols, :])
        acc = part if acc is None else acc + part
    out = x1 + g2_ref[0, 0] * (_rms(acc) * post_ref[...])
    for parity, rows in enumerate(parity_rows):
        for k in range(n_slabs):
            slabs[k, rows, :] = out[parity * th:(parity + 1) * th, k * LANES:(k + 1) * LANES]
    for k in range(n_slabs):
        o_ref[0, :, k * LANES:(k + 1) * LANES] = slabs[k]


def _hyena_out_mlp(a, x, w_out, b_out, mix_post, pre, post, mods, mod_row_fn, w1, w2, layer, *, th=256,
                   ff_chunk=1024):
    bsz, seq, d = x.shape
    half = seq // 2
    d_ff = w1.shape[-1]
    th = min(th, half)
    kern = functools.partial(_hyena_out_mlp_kernel, ff_chunk=ff_chunk)
    vec = pl.BlockSpec((1, d), lambda b, i: (0, 0))
    mod = lambda chunk: _mod_block(d, chunk, lambda b, i: mod_row_fn(b))
    rows = pl.BlockSpec((1, 2 * th, d), lambda b, i: (b, i, 0))
    return pl.pallas_call(
        kern,
        grid=(bsz, half // th),
        in_specs=[
            pl.BlockSpec((1, 2, th, d), lambda b, i: (b, 0, i, 0)),
            rows,
            _const_spec((d, d)), vec, vec, mod(2),
            vec, mod(4), mod(3), mod(5),
            _const_spec((d, d_ff), layer), _const_spec((d_ff, d), layer), vec,
        ],
        out_specs=rows,
        out_shape=jax.ShapeDtypeStruct((bsz, seq, d), F32),
        scratch_shapes=[pltpu.VMEM((d // LANES, 2 * th, LANES), F32)],
        compiler_params=_cparams("arbitrary", "arbitrary"),
    )(a.reshape(bsz, 2, half, d), x, w_out, b_out.reshape(1, d), mix_post.reshape(1, d), mods,
      pre.reshape(1, d), mods, mods, mods, w1, w2, post.reshape(1, d))


def _qkv_kernel(x_ref, pre_ref, sc_ref, sh_ref, w_ref, gains_ref, cos_ref, sin_ref,
                *out_refs, n_q_pairs, rope, sub_rows):
    q_ref = out_refs[0] if n_q_pairs else None
    k_ref, v_ref = out_refs[-2:]
    hd = HEAD_DIM
    tm = x_ref.shape[0]
    mult = pre_ref[...] * (1.0 + sc_ref[0, 0])
    sh = sh_ref[0, 0]
    q_scale = ATTN_SCALE * LOG2_E
    gain_ab = [(gains_ref[0:1, :] * q_scale, gains_ref[1:2, :] * q_scale), (gains_ref[2:3, :], gains_ref[3:4, :])]
    half_r = lax.broadcasted_iota(jnp.int32, (2 * hd, 2 * hd), 0) // (hd // 2)
    half_c = lax.broadcasted_iota(jnp.int32, (2 * hd, 2 * hd), 1) // (hd // 2)
    same_head = (half_r == half_c).astype(BF16)

    def rows_of(i):
        return slice(i * sub_rows, (i + 1) * sub_rows)

    def project(i):
        h = (_rms(x_ref[rows_of(i), :]) * mult + sh).astype(BF16)
        return _dot(h, w_ref[...])

    def finish(i, qkv):
        rows = rows_of(i)
        if rope:
            cos_t, sin_t = cos_ref[rows, :], sin_ref[rows, :]
            tabs = [(ga * cos_t, gb * sin_t, ga * sin_t, gb * cos_t) for ga, gb in gain_ab]

        n_pairs = n_q_pairs + 1
        squares = [(qkv[:, 2 * p * hd:(2 * p + 1) * hd] ** 2 + qkv[:, (2 * p + 1) * hd:(2 * p + 2) * hd] ** 2)
                   .astype(BF16) for p in range(n_pairs)]
        sums = []
        for p in range(0, n_pairs - 1, 2):
            both = _dot(jnp.concatenate(squares[p:p + 2], axis=1), same_head)
            sums += [both[:, 0:hd], both[:, hd:2 * hd]]
        if n_pairs % 2:
            sums.append(_dot(squares[-1], same_head[0:hd, 0:hd]))

        def pair(idx, kind):
            a = qkv[:, 2 * idx * hd:(2 * idx + 1) * hd]
            b = qkv[:, (2 * idx + 1) * hd:(2 * idx + 2) * hd]
            r = lax.rsqrt(sums[idx] * (1.0 / hd) + EPS)
            if rope:
                ca, sb, sa, cb = tabs[kind]
                return (r * (a * ca - b * sb)).astype(BF16), (r * (a * sa + b * cb)).astype(BF16)
            ga, gb = gain_ab[kind]
            return (r * (a * ga)).astype(BF16), (r * (b * gb)).astype(BF16)

        for p in range(n_q_pairs):
            qa, qb = pair(p, 0)
            q_ref[rows, 2 * p * hd:(2 * p + 1) * hd] = qa
            q_ref[rows, (2 * p + 1) * hd:(2 * p + 2) * hd] = qb
        ka, kb = pair(n_q_pairs, 1)
        k_ref[rows, 0:hd] = ka
        k_ref[rows, hd:2 * hd] = kb
        v0 = 2 * (n_q_pairs + 1) * hd
        v_ref[rows, :] = qkv[:, v0:v0 + N_KV_HEADS * hd].astype(BF16)

    n_sub = tm // sub_rows
    pending = project(0)
    for i in range(1, n_sub):
        nxt = project(i)
        finish(i - 1, pending)
        pending = nxt
    finish(n_sub - 1, pending)


def _qkv(x, pre, mods, mod_row_fn, w, gains, cos_t, sin_t, *, n_q_pairs, rope, tm, seq):
    rows, d = x.shape
    hd = HEAD_DIM
    n = w.shape[-1]
    kern = functools.partial(_qkv_kernel, n_q_pairs=n_q_pairs, rope=rope, sub_rows=min(tm, 256))
    vec = pl.BlockSpec((1, d), lambda i: (0, 0))
    tiles_per_seq = seq // tm
    pos = pl.BlockSpec((tm, hd), lambda i: (i % tiles_per_seq, 0))
    widths = ([2 * n_q_pairs * hd] if n_q_pairs else []) + [N_KV_HEADS * hd] * 2
    return pl.pallas_call(
        kern,
        grid=(rows // tm,),
        in_specs=[
            pl.BlockSpec((tm, d), lambda i: (i, 0)),
            vec,
            _mod_block(d, 1, mod_row_fn),
            _mod_block(d, 0, mod_row_fn),
            _const_spec((d, n)),
            pl.BlockSpec((4, hd), lambda i: (0, 0)),
            pos, pos,
        ],
        out_specs=[pl.BlockSpec((tm, wd), lambda i: (i, 0)) for wd in widths],
        out_shape=[jax.ShapeDtypeStruct((rows, wd), BF16) for wd in widths],
        compiler_params=_cparams("arbitrary"),
    )(x, pre.reshape(1, d), mods, mods, w, gains, cos_t, sin_t)


def _attn_kernel(q_ref, kl_ref, vl_ref, kc_ref, vc_ref, x_ref, wo_ref, gain_ref, g_ref, o_ref, k_all, v_ext,
                 *, n_pairs, kv_blocks):
    hd = HEAD_DIM
    tq = q_ref.shape[1]
    seq = kl_ref.shape[1]
    total = k_all.shape[0]

    @pl.when(pl.program_id(1) == 0)
    def _():
        k_all[0:seq, :] = kl_ref[0]
        k_all[seq:total, :] = kc_ref[0]
        for h in range(N_KV_HEADS):
            v_ext[h, 0:seq, 0:hd] = vl_ref[0, :, h * hd:(h + 1) * hd]
            v_ext[h, seq:total, 0:hd] = vc_ref[0, :, h * hd:(h + 1) * hd]
            v_ext[h, :, hd:2 * hd] = jnp.ones((total, hd), BF16)

    nt = (((1,), (1,)), ((), ()))
    lane_half = (lax.broadcasted_iota(jnp.int32, (1, 2 * hd), 1) % hd) // (hd // 2)

    def start(kv_head):
        keep = (lane_half == kv_head).astype(BF16)
        q = jnp.concatenate([q_ref[0, :, 2 * p * hd:(2 * p + 2) * hd] * keep for p in range(n_pairs)], axis=0)
        st = dict(h=kv_head, q=q, m=None, acc=None)
        st["s_next"] = lax.dot_general(q, k_all[kv_blocks[0][0]:kv_blocks[0][1], :], nt, preferred_element_type=F32)
        return st

    def step(st, i):
        s0, s1 = kv_blocks[i]
        s = st["s_next"]
        if i + 1 < len(kv_blocks):
            n0, n1 = kv_blocks[i + 1]
            st["s_next"] = lax.dot_general(st["q"], k_all[n0:n1, :], nt, preferred_element_type=F32)
        m, acc = st["m"], st["acc"]
        m_blk = jnp.max(s, axis=-1, keepdims=True)
        m_new = m_blk if m is None else jnp.maximum(m, m_blk)
        pv = _dot(jnp.exp2(s - m_new).astype(BF16), v_ext[st["h"], s0:s1, :])
        st["acc"] = pv if m is None else acc * jnp.exp2(m - m_new) + pv
        st["m"] = m_new

    def heads_of(st):
        acc = st["acc"]
        o = (acc[:, 0:hd] / acc[:, hd:2 * hd]).astype(BF16)
        return [o[p * tq:(p + 1) * tq, :] for p in range(n_pairs)]

    last = len(kv_blocks) - 1
    first, second = start(0), None
    for i in range(last):
        step(first, i)
    second = start(1)
    step(first, last)
    for i in range(last + 1):
        step(second, i)
    attn = jnp.concatenate(heads_of(first) + heads_of(second), axis=1)
    y = _dot(attn, wo_ref[...])
    o_ref[0] = x_ref[0] + g_ref[0, 0] * (_rms(y) * gain_ref[...])


def _kv_blocks(total, pattern):
    assert total % MXU_WIDTH == 0
    n_tiles = total // MXU_WIDTH
    blocks, start, i = [], 0, 0
    while start < n_tiles:
        size = min(pattern[i % len(pattern)], n_tiles - start)
        blocks.append((start * MXU_WIDTH, (start + size) * MXU_WIDTH))
        start += size
        i += 1
    return tuple(blocks)


def _attention(q, k_l, v_l, k_c, v_c, x, w_o, gain, mods, gate_chunk, mod_row_fn, *, tq=512, kv_pattern=(5, 3, 1)):
    bsz, seq, dq = q.shape
    d = x.shape[-1]
    ctx_len = k_c.shape[1]
    hd = HEAD_DIM
    kv = N_KV_HEADS * hd
    n_pairs = dq // (2 * hd)
    total = seq + ctx_len
    kern = functools.partial(_attn_kernel, n_pairs=n_pairs, kv_blocks=_kv_blocks(total, kv_pattern))
    whole = lambda rows, width: pl.BlockSpec((1, rows, width), lambda b, i: (b, 0, 0))
    tile = lambda width: pl.BlockSpec((1, tq, width), lambda b, i: (b, i, 0))
    return pl.pallas_call(
        kern,
        grid=(bsz, seq // tq),
        in_specs=[
            tile(dq), whole(seq, 2 * hd), whole(seq, kv), whole(ctx_len, 2 * hd), whole(ctx_len, kv),
            tile(d),
            _const_spec((dq, d)),
            pl.BlockSpec((1, d), lambda b, i: (0, 0)),
            _mod_block(d, gate_chunk, lambda b, i: mod_row_fn(b)),
        ],
        out_specs=tile(d),
        out_shape=jax.ShapeDtypeStruct((bsz, seq, d), F32),
        scratch_shapes=[pltpu.VMEM((total, 2 * hd), BF16), pltpu.VMEM((N_KV_HEADS, total, 2 * hd), BF16)],
        compiler_params=_cparams("arbitrary", "arbitrary"),
    )(q, k_l, v_l, k_c, v_c, x, w_o, gain.reshape(1, d), mods)


def _rope_tables(seq):
    rows = seq // GRID_W
    pairs = HEAD_DIM // 4
    row = jnp.repeat(jnp.arange(rows, dtype=F32), GRID_W)
    col = jnp.tile(jnp.arange(GRID_W, dtype=F32), rows)
    inv = ROPE_THETA ** (-jnp.arange(pairs, dtype=F32) / pairs)
    ang = jnp.concatenate([row[:, None] * inv[None, :], col[:, None] * inv[None, :]], axis=-1)
    cos, sin = jnp.cos(ang), jnp.sin(ang)
    return jnp.concatenate([cos, cos], axis=-1), jnp.concatenate([sin, sin], axis=-1)


def _pair_columns(head_a, head_b):
    hd = HEAD_DIM
    even, odd = jnp.arange(0, hd, 2), jnp.arange(1, hd, 2)
    return jnp.concatenate([head_a * hd + even, head_b * hd + even, head_a * hd + odd, head_b * hd + odd])


def kernel(x, c, ctx, c_ctx, mod_w, mod_b, mix_norm_pre, mix_norm_post, mlp_norm_pre, mlp_norm_post, mlp_w1, mlp_w2, hy_w_in, hy_b_in, hy_conv_w, hy_conv_b, hy_filt_w1, hy_filt_b1, hy_filt_freq1, hy_filt_w2, hy_filt_b2, hy_filt_freq2, hy_filt_w3, hy_filt_bias, hy_w_out, hy_b_out, attn_w_qkv, attn_q_norm, attn_k_norm, attn_w_o):
    bsz, seq, d = x.shape
    ctx_len = ctx.shape[1]
    hd = HEAD_DIM
    n_heads = d // hd
    tm = 1024

    ctx_row = bsz
    n_rows = -(-(bsz + 1) // MOD_ROWS_PAD) * MOD_ROWS_PAD
    cond = jnp.concatenate([c, c_ctx[None, :], jnp.zeros((n_rows - bsz - 1, d), F32)], axis=0)
    mods_all = _modulation(cond, mod_w, mod_b).reshape(mod_w.shape[0], n_rows, 6, 1, d)

    x_row = lambda i: i // (seq // tm)
    c_row = lambda i: ctx_row

    xf = x.reshape(bsz * seq, d)
    cf = ctx.reshape(bsz * ctx_len, d)

    mods = mods_all[0]
    w_in = hy_w_in[0].astype(BF16)
    w_out = hy_w_out[0].astype(BF16)
    filt = (hy_filt_w1[0], hy_filt_b1[0], hy_filt_freq1[0], hy_filt_w2[0], hy_filt_b2[0], hy_filt_freq2[0],
            hy_filt_w3[0], hy_filt_bias[0])
    w1 = _to_bf16(mlp_w1)
    w2 = _to_bf16(mlp_w2)

    def hyena_conv(tokens, mod_row_b):
        u, x1 = _hyena_in(tokens, mix_norm_pre[0], mods, mod_row_b, w_in, hy_b_in[0], hy_conv_w[0], hy_conv_b[0])
        return _fftconv(u, x1, _hyena_filter_tables(tokens.shape[1], *filt))

    x3 = xf.reshape(bsz, seq, d)
    xf = _hyena_out_mlp(hyena_conv(x3, lambda b: b), x3, w_out, hy_b_out[0], mix_norm_post[0],
                        mlp_norm_pre[0], mlp_norm_post[0], mods, lambda b: b, w1, w2, 0).reshape(bsz * seq, d)
    c3 = cf.reshape(bsz, ctx_len, d)
    cf = _proj_res_parity(hyena_conv(c3, lambda b: ctx_row), c3, w_out, hy_b_out[0], mix_norm_post[0], mods, 2,
                          lambda b: ctx_row, th=512).reshape(bsz * ctx_len, d)
    cf = _mlp(cf, mlp_norm_pre[0], mlp_norm_post[0], mods, c_row, w1, w2, 0, tm=min(tm, bsz * ctx_len))

    mods = mods_all[1]
    w_qkv = attn_w_qkv[0]
    assert N_KV_HEADS == 2 and n_heads % 2 == 0
    group = n_heads // N_KV_HEADS
    pair_cols = [_pair_columns(p, group + p) for p in range(group)] + [_pair_columns(n_heads, n_heads + 1)]
    v_cols = jnp.arange((n_heads + N_KV_HEADS) * hd, (n_heads + 2 * N_KV_HEADS) * hd)
    w_lat = w_qkv[:, jnp.concatenate(pair_cols + [v_cols])].astype(BF16)
    w_ctx = w_lat[:, n_heads * hd:]
    even, odd = jnp.arange(0, hd, 2), jnp.arange(1, hd, 2)
    gains = jnp.stack([jnp.tile(g[idx], 2) for g in (attn_q_norm[0], attn_k_norm[0]) for idx in (even, odd)])
    cos_t, sin_t = _rope_tables(seq)

    q, k_l, v_l = _qkv(xf, mix_norm_pre[1], mods, x_row, w_lat, gains, cos_t, sin_t,
                       n_q_pairs=group, rope=True, tm=tm, seq=seq)
    k_c, v_c = _qkv(cf, mix_norm_pre[1], mods, c_row, w_ctx, gains, cos_t, sin_t,
                    n_q_pairs=0, rope=False, tm=min(tm, bsz * ctx_len), seq=min(tm, bsz * ctx_len))
    kv = N_KV_HEADS * hd
    xf = _attention(q.reshape(bsz, seq, d), k_l.reshape(bsz, seq, kv), v_l.reshape(bsz, seq, kv),
                    k_c.reshape(bsz, ctx_len, kv), v_c.reshape(bsz, ctx_len, kv), xf.reshape(bsz, seq, d),
                    attn_w_o[0].astype(BF16), mix_norm_post[1], mods, 2, lambda b: b).reshape(bsz * seq, d)
    xf = _mlp(xf, mlp_norm_pre[1], mlp_norm_post[1], mods, x_row, w1, w2, 1, tm=tm)
    return xf.reshape(bsz, seq, d)
```

```python
import functools
import math

import jax
import jax.numpy as jnp
from jax import lax
from jax.experimental import pallas as pl
from jax.experimental.pallas import tpu as pltpu

F32 = jnp.float32
BF16 = jnp.bfloat16

EPS = 1e-6
GRID_W = 64
HY_BANDS = 16
HY_DECAY_TARGET = 1e-2
HY_FAST = 0.3
HY_SLOW = 1.5
HY_SHIFT = 0.0
HEAD_DIM = 128
N_KV_HEADS = 2
ROPE_THETA = 10000.0
ATTN_SCALE = HEAD_DIM ** -0.5
LOG2_E = math.log2(math.e)

MOD_ROWS_PAD = 8
SUBLANES = 8
LANES = 128
MXU_WIDTH = 256

VMEM_LIMIT = 56 * 1024 * 1024


def _cparams(*sem):
    return pltpu.CompilerParams(dimension_semantics=sem, vmem_limit_bytes=VMEM_LIMIT)


def _const_spec(shape, layer=None):
    nd = len(shape)
    if layer is None:
        return pl.BlockSpec(shape, lambda *_: (0,) * nd, pipeline_mode=pl.Buffered(1))
    return pl.BlockSpec((None,) + tuple(shape), lambda *_: (layer,) + (0,) * nd, pipeline_mode=pl.Buffered(1))


def _cast_kernel(w_ref, o_ref):
    o_ref[...] = w_ref[...].astype(BF16)


def _to_bf16(w, *, block_elems=1024 * 1024):
    depth, r, c = w.shape
    block_rows = min(r, max(SUBLANES, block_elems // c))
    spec = pl.BlockSpec((1, block_rows, c), lambda l, i: (l, i, 0))
    return pl.pallas_call(
        _cast_kernel,
        grid=(depth, r // block_rows),
        in_specs=[spec],
        out_specs=spec,
        out_shape=jax.ShapeDtypeStruct(w.shape, BF16),
        compiler_params=_cparams("arbitrary", "arbitrary"),
    )(w)


def _rms(x):
    return x * lax.rsqrt(jnp.mean(x * x, axis=-1, keepdims=True) + EPS)


def _norm_mod(x, gain, sc, sh):
    return _rms(x) * (gain * (1.0 + sc)) + sh


def _dot(a, b):
    return jnp.dot(a, b, preferred_element_type=F32)


def _dot_hi(a, b):
    return jnp.dot(a, b, preferred_element_type=F32, precision=lax.Precision.HIGHEST)


def _mod_kernel(cond_ref, w_ref, b_ref, o_ref):
    s = cond_ref[...]
    s = s * jax.nn.sigmoid(s)
    o_ref[0] = _dot(s.astype(BF16), w_ref[0].astype(BF16)) + b_ref[0]


def _modulation(cond, mod_w, mod_b):
    depth, d, n = mod_w.shape
    rows = cond.shape[0]
    tn = d
    return pl.pallas_call(
        _mod_kernel,
        grid=(depth, n // tn),
        in_specs=[
            pl.BlockSpec((rows, d), lambda i, j: (0, 0)),
            pl.BlockSpec((1, d, tn), lambda i, j: (i, 0, j)),
            pl.BlockSpec((1, 1, tn), lambda i, j: (i, 0, j)),
        ],
        out_specs=pl.BlockSpec((1, rows, tn), lambda i, j: (i, 0, j)),
        out_shape=jax.ShapeDtypeStruct((depth, rows, n), F32),
        compiler_params=_cparams("arbitrary", "arbitrary"),
    )(cond, mod_w, mod_b.reshape(depth, 1, n))


def _mod_block(d, chunk, row_fn):
    return pl.BlockSpec((1, 1, 1, d), lambda *idx: (row_fn(*idx), chunk, 0, 0))


def _hyena_in_kernel(x_ref, gain_ref, sc_ref, sh_ref, w0_ref, w1_ref, w2_ref, b_ref, cw_ref, cb_ref,
                     u_ref, x1_ref, h_buf, x_slabs, *, seq, rc):
    j = pl.program_id(1)
    w = u_ref.shape[-1]
    half = seq // 2
    n_chunks = half // rc
    n_slabs = x_slabs.shape[0]

    @pl.when(j == 0)
    def _():
        mult = gain_ref[...] * (1.0 + sc_ref[0, 0])
        sh = sh_ref[0, 0]

        def body(c, carry):
            base = pl.multiple_of(2 * c * rc, 2 * rc)
            for k in range(n_slabs):
                x_slabs[k] = x_ref[0, pl.ds(base, 2 * rc), k * LANES:(k + 1) * LANES]
            for parity in range(2):
                xs = jnp.concatenate([x_slabs[k, pl.ds(parity, rc, stride=2), :] for k in range(n_slabs)], axis=1)
                h_buf[pl.ds(base + parity * rc, rc), :] = (_rms(xs) * mult + sh).astype(BF16)
            return carry

        lax.fori_loop(0, n_chunks, body, 0)

    sub = lax.broadcasted_iota(jnp.int32, (SUBLANES, w), 0)
    w_refs = (w0_ref, w1_ref, w2_ref)
    taps = [[cw_ref[k, s:s + 1, :] for s in range(3)] for k in range(3)]
    bias = [cb_ref[s:s + 1, :] + b_ref[s:s + 1, :] * (taps[0][s] + taps[1][s] + taps[2][s]) for s in range(3)]

    def project(c):
        h = h_buf[2 * c * rc:2 * (c + 1) * rc, :]
        return [_dot(h, w_refs[s][...]) for s in range(3)]

    def conv(c, m_before, m_here, m_after):
        even_streams, odd_streams = [], []
        for s in range(3):
            m_even, m_odd = m_here[s][0:rc], m_here[s][rc:2 * rc]
            row_before = -b_ref[s:s + 1, :] if m_before is None else m_before[s][2 * rc - 1:2 * rc, :]
            row_after = -b_ref[s:s + 1, :] if m_after is None else m_after[s][0:1, :]
            odd_prev = pltpu.roll(m_odd, 1, 0)
            odd_prev = jnp.concatenate([jnp.where(sub == 0, row_before, odd_prev[0:SUBLANES]),
                                        odd_prev[SUBLANES:]], axis=0)
            even_next = pltpu.roll(m_even, rc - 1, 0)
            even_next = jnp.concatenate([even_next[:rc - SUBLANES],
                                         jnp.where(sub == SUBLANES - 1, row_after, even_next[rc - SUBLANES:])], axis=0)
            even_streams.append(odd_prev * taps[0][s] + m_even * taps[1][s] + m_odd * taps[2][s] + bias[s])
            odd_streams.append(m_even * taps[0][s] + m_odd * taps[1][s] + even_next * taps[2][s] + bias[s])
        for base, (x1, x2, v) in ((0, even_streams), (half, odd_streams)):
            u_ref[0, base + c * rc:base + (c + 1) * rc, :] = (v * x2).astype(BF16)
            x1_ref[0, base + c * rc:base + (c + 1) * rc, :] = x1.astype(BF16)

    zs = [project(0)]
    for c in range(1, n_chunks):
        zs.append(project(c))
        conv(c - 1, zs[c - 2] if c >= 2 else None, zs[c - 1], zs[c])
    conv(n_chunks - 1, zs[n_chunks - 2] if n_chunks >= 2 else None, zs[n_chunks - 1], None)


def _hyena_in(x, gain, mods, mod_row_fn, w_in, b_in, conv_w, conv_b, *, block_elems=2048 * 512, rc=128):
    bsz, seq, d = x.shape
    w = min(d, max(MXU_WIDTH, block_elems // seq // MXU_WIDTH * MXU_WIDTH))
    nj = d // w
    rc = min(rc, seq // 2)
    kern = functools.partial(_hyena_in_kernel, seq=seq, rc=rc)
    wspec = lambda s: pl.BlockSpec((d, w), lambda b, j: (0, s * nj + j))
    out_spec = pl.BlockSpec((1, seq, w), lambda b, j: (b, 0, j))
    return pl.pallas_call(
        kern,
        grid=(bsz, nj),
        in_specs=[
            pl.BlockSpec((1, seq, d), lambda b, j: (b, 0, 0)),
            pl.BlockSpec((1, d), lambda b, j: (0, 0)),
            _mod_block(d, 1, lambda b, j: mod_row_fn(b)),
            _mod_block(d, 0, lambda b, j: mod_row_fn(b)),
            wspec(0), wspec(1), wspec(2),
            pl.BlockSpec((3, w), lambda b, j: (0, j)),
            pl.BlockSpec((3, 3, w), lambda b, j: (0, 0, j)),
            pl.BlockSpec((3, w), lambda b, j: (0, j)),
        ],
        out_specs=[out_spec, out_spec],
        out_shape=[jax.ShapeDtypeStruct((bsz, seq, d), BF16)] * 2,
        scratch_shapes=[pltpu.VMEM((seq, d), BF16), pltpu.VMEM((d // LANES, 2 * rc, LANES), F32)],
        compiler_params=_cparams("arbitrary", "arbitrary"),
    )(x, gain.reshape(1, d), mods, mods, w_in, w_in, w_in,
      b_in.reshape(3, d), conv_w.reshape(3, 3, d), conv_b.reshape(3, d))


def _dft_table(length, n_freq, shifted, kb=32):
    if shifted:
        period = 8 * length
        nn = 2 * jnp.arange(length, dtype=jnp.int32)[None, :] + 1
    else:
        period = 4 * length
        nn = jnp.arange(length, dtype=jnp.int32)[None, :]
    kh = jnp.arange(n_freq // kb, dtype=jnp.int32)[:, None]
    kl = jnp.arange(kb, dtype=jnp.int32)[:, None]
    to_angle = lambda m: (m % period).astype(F32) * (2.0 * math.pi / period)
    alpha = to_angle(2 * kb * kh * nn)[:, None, :]
    beta = to_angle((2 * kl + 1) * nn)[None, :, :]
    ca, sa, cb, sb = jnp.cos(alpha), jnp.sin(alpha), jnp.cos(beta), jnp.sin(beta)
    cos_t = (ca * cb - sa * sb).reshape(n_freq, length)
    sin_t = (sa * cb + ca * sb).reshape(n_freq, length)
    return jnp.concatenate([cos_t, sin_t], axis=0).astype(BF16)


def _filter_kernel(bands_ref, w1t_ref, w1c_ref, w1s_ref, b1_ref, f1_ref, w2_ref, b2_ref, f2_ref,
                   w3f_ref, w3b_ref, deltas_ref, bias_ref, psi_ref, tab_ref, o_ref, h_buf, *, seq):
    w = o_ref.shape[-1]
    half = seq // 2

    @pl.when(pl.program_id(0) == 0)
    def _():
        t64 = lax.broadcasted_iota(jnp.int32, (seq, w1t_ref.shape[-1]), 0).astype(F32) / seq
        t16 = lax.broadcasted_iota(jnp.int32, (seq, HY_BANDS), 0).astype(F32) / seq
        ang = (2.0 * math.pi * t16) * bands_ref[...]
        pre = t64 * w1t_ref[...] + _dot_hi(jnp.cos(ang), w1c_ref[...]) + _dot_hi(jnp.sin(ang), w1s_ref[...])
        h = jnp.sin(f1_ref[...] * (pre + b1_ref[...]))
        h_buf[...] = jnp.sin(f2_ref[...] * (_dot_hi(h, w2_ref[...]) + b2_ref[...]))

    h = h_buf[...]
    row = lax.broadcasted_iota(jnp.int32, (seq, w), 0)
    decay = jnp.exp(-(row.astype(F32) / seq) * deltas_ref[...]) + HY_SHIFT
    h_f = _dot_hi(h, w3f_ref[...]) * decay
    h_b = _dot_hi(h, w3b_ref[...]) * decay
    h_b = jnp.where(row == 0, 0.0, h_b)
    h_sum = h_f + h_b
    h_dif = h_b - h_f
    alt = jnp.where(row % 2 == 0, 1.0, -1.0)
    cos_lo = tab_ref[pl.ds(0, half), :]
    sin_lo = tab_ref[pl.ds(half, half), :]
    bias = bias_ref[...]
    p_lo = _dot(cos_lo, h_sum.astype(BF16)) + bias
    p_hi = _dot(cos_lo, (h_sum * alt).astype(BF16)) + bias
    q_lo = _dot(sin_lo, h_dif.astype(BF16))
    q_hi = -_dot(sin_lo, (h_dif * alt).astype(BF16))
    d_re = p_lo - p_hi
    d_im = q_lo + q_hi
    psi_c = jnp.concatenate([psi_ref[0]] * (w // psi_ref.shape[-1]), axis=1)
    psi_s = jnp.concatenate([psi_ref[1]] * (w // psi_ref.shape[-1]), axis=1)
    o_ref[0] = p_lo + p_hi
    o_ref[1] = q_lo - q_hi
    o_ref[2] = d_re * psi_c - d_im * psi_s
    o_ref[3] = d_re * psi_s + d_im * psi_c
    o_ref[4] = d_re * psi_c + d_im * psi_s
    o_ref[5] = d_im * psi_c - d_re * psi_s


def _hyena_filter_tables(seq, fw1, fb1, ff1, fw2, fb2, ff2, fw3, fbias, *, w=256):
    d = fbias.shape[-1]
    fwid = fw2.shape[0]
    half = seq // 2
    lanes = LANES
    bands = jnp.linspace(1e-4, HY_BANDS - 1, HY_BANDS, dtype=F32).reshape(1, HY_BANDS)
    deltas = jnp.abs(jnp.linspace(math.log(HY_DECAY_TARGET) / HY_SLOW, math.log(HY_DECAY_TARGET) / HY_FAST,
                                  d, dtype=F32)).reshape(1, d)
    theta = (2 * jnp.arange(half, dtype=F32) + 1.0) * (math.pi / (2 * seq))
    psi = jnp.broadcast_to(jnp.stack([jnp.cos(theta), jnp.sin(theta)])[:, :, None], (2, half, lanes))
    tab_plain = _dft_table(seq, half, shifted=False)
    nj = d // w
    small = lambda shape: pl.BlockSpec(shape, lambda j: (0,) * len(shape))
    kern = functools.partial(_filter_kernel, seq=seq)
    return pl.pallas_call(
        kern,
        grid=(nj,),
        in_specs=[
            small((1, HY_BANDS)), small((1, fwid)), small((HY_BANDS, fwid)), small((HY_BANDS, fwid)),
            small((1, fwid)), small((1, fwid)), small((fwid, fwid)), small((1, fwid)), small((1, fwid)),
            pl.BlockSpec((fwid, w), lambda j: (0, j)),
            pl.BlockSpec((fwid, w), lambda j: (0, nj + j)),
            pl.BlockSpec((1, w), lambda j: (0, j)),
            pl.BlockSpec((1, w), lambda j: (0, j)),
            small((2, half, lanes)),
            _const_spec((2 * half, seq)),
        ],
        out_specs=pl.BlockSpec((6, half, w), lambda j: (0, 0, j)),
        out_shape=jax.ShapeDtypeStruct((6, half, d), F32),
        scratch_shapes=[pltpu.VMEM((seq, fwid), F32)],
        compiler_params=_cparams("arbitrary"),
    )(bands, fw1[0:1], fw1[1:1 + HY_BANDS], fw1[1 + HY_BANDS:], fb1.reshape(1, fwid), ff1.reshape(1, fwid),
      fw2, fb2.reshape(1, fwid), ff2.reshape(1, fwid), fw3, fw3, deltas, fbias.reshape(1, d), psi, tab_plain)


def _fftconv_kernel(u_ref, x1_ref, sef_ref, tab_ref, o_ref, *, seq):
    half = seq // 2
    cos_t = tab_ref[pl.ds(0, half), :]
    sin_t = tab_ref[pl.ds(half, half), :]
    u0 = u_ref[0, 0:half, :]
    u1 = u_ref[0, half:seq, :]
    a0, b0 = _dot(cos_t, u0), _dot(sin_t, u0)
    a1, b1 = _dot(cos_t, u1), _dot(sin_t, u1)
    s_re, s_im, e_re, e_im, f_re, f_im = (sef_ref[i] for i in range(6))
    v0_re = (s_re * a0 + s_im * b0 + f_re * a1 + f_im * b1).astype(BF16)
    v0_im = (s_im * a0 - s_re * b0 + f_im * a1 - f_re * b1).astype(BF16)
    v1_re = (e_re * a0 + e_im * b0 + s_re * a1 + s_im * b1).astype(BF16)
    v1_im = (e_im * a0 - e_re * b0 + s_im * a1 - s_re * b1).astype(BF16)
    y0 = _dot(cos_t, v0_re) - _dot(sin_t, v0_im)
    y1 = _dot(cos_t, v1_re) - _dot(sin_t, v1_im)
    o_ref[0, 0:half, :] = (y0 * (1.0 / seq) * x1_ref[0, 0:half, :].astype(F32)).astype(BF16)
    o_ref[0, half:seq, :] = (y1 * (1.0 / seq) * x1_ref[0, half:seq, :].astype(F32)).astype(BF16)


def _fftconv(u, x1, sef, *, block_elems=2048 * 256):
    bsz, seq, d = u.shape
    half = seq // 2
    w = min(d, max(MXU_WIDTH, block_elems // seq // MXU_WIDTH * MXU_WIDTH))
    nj = d // w
    act = pl.BlockSpec((1, seq, w), lambda j, b: (b, 0, j))
    kern = functools.partial(_fftconv_kernel, seq=seq)
    return pl.pallas_call(
        kern,
        grid=(nj, bsz),
        in_specs=[act, act, pl.BlockSpec((6, half, w), lambda j, b: (0, 0, j)), _const_spec((2 * half, half))],
        out_specs=act,
        out_shape=jax.ShapeDtypeStruct((bsz, seq, d), BF16),
        compiler_params=_cparams("arbitrary", "arbitrary"),
    )(u, x1, sef, _dft_table(half, half, shifted=True))


def _mlp_kernel(x_ref, pre_ref, sc_ref, sh_ref, g_ref, w1_ref, w2_ref, post_ref, o_ref, *, ff_chunk, sub_rows):
    d_ff = w1_ref.shape[-1]
    mult = pre_ref[...] * (1.0 + sc_ref[0, 0])
    sh = sh_ref[0, 0]

    def rows_of(i):
        return slice(i * sub_rows, (i + 1) * sub_rows)

    def hidden(i):
        return (_rms(x_ref[rows_of(i), :]) * mult + sh).astype(BF16)

    def mix(h):
        acc = None
        for c in range(d_ff // ff_chunk):
            cols = pl.ds(c * ff_chunk, ff_chunk)
            a = jnp.maximum(_dot(h, w1_ref[:, cols]), 0.0)
            part = _dot((a * a).astype(BF16), w2_ref[cols, :])
            acc = part if acc is None else acc + part
        return acc

    def finish(i, acc):
        o_ref[rows_of(i), :] = x_ref[rows_of(i), :] + g_ref[0, 0] * (_rms(acc) * post_ref[...])

    n_sub = x_ref.shape[0] // sub_rows
    pending = mix(hidden(0))
    for i in range(1, n_sub):
        nxt = mix(hidden(i))
        finish(i - 1, pending)
        pending = nxt
    finish(n_sub - 1, pending)


def _mlp(x, pre, post, mods, mod_row_fn, w1, w2, layer, *, tm, ff_chunk=1024, sub_rows=512):
    rows, d = x.shape
    d_ff = w1.shape[-1]
    kern = functools.partial(_mlp_kernel, ff_chunk=ff_chunk, sub_rows=min(sub_rows, tm))
    vec = pl.BlockSpec((1, d), lambda i: (0, 0))
    return pl.pallas_call(
        kern,
        grid=(rows // tm,),
        in_specs=[
            pl.BlockSpec((tm, d), lambda i: (i, 0)),
            vec,
            _mod_block(d, 4, mod_row_fn),
            _mod_block(d, 3, mod_row_fn),
            _mod_block(d, 5, mod_row_fn),
            _const_spec((d, d_ff), layer),
            _const_spec((d_ff, d), layer),
            vec,
        ],
        out_specs=pl.BlockSpec((tm, d), lambda i: (i, 0)),
        out_shape=jax.ShapeDtypeStruct((rows, d), F32),
        compiler_params=_cparams("arbitrary"),
    )(x, pre.reshape(1, d), mods, mods, mods, w1, w2, post.reshape(1, d))


def _hyena_out_mlp_kernel(a_ref, x_ref, wo_ref, bo_ref, mixpost_ref, g1_ref, pre_ref, sc_ref, sh_ref, g2_ref,
                          w1_ref, w2_ref, post_ref, o_ref, slabs, *, ff_chunk):
    th = a_ref.shape[2]
    d_ff = w1_ref.shape[-1]
    n_slabs = slabs.shape[0]
    for k in range(n_slabs):
        slabs[k] = x_ref[0, :, k * LANES:(k + 1) * LANES]
    parity_rows = [pl.ds(parity, th, stride=2) for parity in range(2)]
    x = jnp.concatenate([jnp.concatenate([slabs[k, rows, :] for k in range(n_slabs)], axis=1)
                         for rows in parity_rows], axis=0)
    a = jnp.concatenate([a_ref[0, 0], a_ref[0, 1]], axis=0)
    y = _dot(a, wo_ref[...]) + bo_ref[...]
    x1 = x + g1_ref[0, 0] * (_rms(y) * mixpost_ref[...])
    h = _norm_mod(x1, pre_ref[...], sc_ref[0, 0], sh_ref[0, 0]).astype(BF16)
    acc = None
    for c in range(d_ff // ff_chunk):
        cols = pl.ds(c * ff_chunk, ff_chunk)
        t = jnp.maximum(_dot(h, w1_ref[:, cols]), 0.0)
        part = _dot((t * t).astype(BF16), w2_ref[cols, :])
        acc = part if acc is None else acc + part
    out = x1 + g2_ref[0, 0] * (_rms(acc) * post_ref[...])
    for parity, rows in enumerate(parity_rows):
        for k in range(n_slabs):
            slabs[k, rows, :] = out[parity * th:(parity + 1) * th, k * LANES:(k + 1) * LANES]
    for k in range(n_slabs):
        o_ref[0, :, k * LANES:(k + 1) * LANES] = slabs[k]


def _hyena_out_mlp(a, x, w_out, b_out, mix_post, pre, post, mods, mod_row_fn, w1, w2, layer, *, th=256,
                   ff_chunk=1024):
    bsz, seq, d = x.shape
    half = seq // 2
    d_ff = w1.shape[-1]
    th = min(th, half)
    kern = functools.partial(_hyena_out_mlp_kernel, ff_chunk=ff_chunk)
    vec = pl.BlockSpec((1, d), lambda b, i: (0, 0))
    mod = lambda chunk: _mod_block(d, chunk, lambda b, i: mod_row_fn(b))
    rows = pl.BlockSpec((1, 2 * th, d), lambda b, i: (b, i, 0))
    return pl.pallas_call(
        kern,
        grid=(bsz, half // th),
        in_specs=[
            pl.BlockSpec((1, 2, th, d), lambda b, i: (b, 0, i, 0)),
            rows,
            _const_spec((d, d)), vec, vec, mod(2),
            vec, mod(4), mod(3), mod(5),
            _const_spec((d, d_ff), layer), _const_spec((d_ff, d), layer), vec,
        ],
        out_specs=rows,
        out_shape=jax.ShapeDtypeStruct((bsz, seq, d), F32),
        scratch_shapes=[pltpu.VMEM((d // LANES, 2 * th, LANES), F32)],
        compiler_params=_cparams("arbitrary", "arbitrary"),
    )(a.reshape(bsz, 2, half, d), x, w_out, b_out.reshape(1, d), mix_post.reshape(1, d), mods,
      pre.reshape(1, d), mods, mods, mods, w1, w2, post.reshape(1, d))


def _qkv_kernel(x_ref, pre_ref, sc_ref, sh_ref, w_ref, gains_ref, cos_ref, sin_ref,
                *out_refs, n_q_pairs, rope, sub_rows):
    q_ref = out_refs[0] if n_q_pairs else None
    k_ref, v_ref = out_refs[-2:]
    hd = HEAD_DIM
    tm = x_ref.shape[0]
    mult = pre_ref[...] * (1.0 + sc_ref[0, 0])
    sh = sh_ref[0, 0]
    q_scale = ATTN_SCALE * LOG2_E
    gain_ab = [(gains_ref[0:1, :] * q_scale, gains_ref[1:2, :] * q_scale), (gains_ref[2:3, :], gains_ref[3:4, :])]
    half_r = lax.broadcasted_iota(jnp.int32, (2 * hd, 2 * hd), 0) // (hd // 2)
    half_c = lax.broadcasted_iota(jnp.int32, (2 * hd, 2 * hd), 1) // (hd // 2)
    same_head = (half_r == half_c).astype(BF16)

    def rows_of(i):
        return slice(i * sub_rows, (i + 1) * sub_rows)

    def project(i):
        h = (_rms(x_ref[rows_of(i), :]) * mult + sh).astype(BF16)
        return _dot(h, w_ref[...])

    def finish(i, qkv):
        rows = rows_of(i)
        if rope:
            cos_t, sin_t = cos_ref[rows, :], sin_ref[rows, :]
            tabs = [(ga * cos_t, gb * sin_t, ga * sin_t, gb * cos_t) for ga, gb in gain_ab]

        n_pairs = n_q_pairs + 1
        squares = [(qkv[:, 2 * p * hd:(2 * p + 1) * hd] ** 2 + qkv[:, (2 * p + 1) * hd:(2 * p + 2) * hd] ** 2)
                   .astype(BF16) for p in range(n_pairs)]
        sums = []
        for p in range(0, n_pairs - 1, 2):
            both = _dot(jnp.concatenate(squares[p:p + 2], axis=1), same_head)
            sums += [both[:, 0:hd], both[:, hd:2 * hd]]
        if n_pairs % 2:
            sums.append(_dot(squares[-1], same_head[0:hd, 0:hd]))

        def pair(idx, kind):
            a = qkv[:, 2 * idx * hd:(2 * idx + 1) * hd]
            b = qkv[:, (2 * idx + 1) * hd:(2 * idx + 2) * hd]
            r = lax.rsqrt(sums[idx] * (1.0 / hd) + EPS)
            if rope:
                ca, sb, sa, cb = tabs[kind]
                return (r * (a * ca - b * sb)).astype(BF16), (r * (a * sa + b * cb)).astype(BF16)
            ga, gb = gain_ab[kind]
            return (r * (a * ga)).astype(BF16), (r * (b * gb)).astype(BF16)

        for p in range(n_q_pairs):
            qa, qb = pair(p, 0)
            q_ref[rows, 2 * p * hd:(2 * p + 1) * hd] = qa
            q_ref[rows, (2 * p + 1) * hd:(2 * p + 2) * hd] = qb
        ka, kb = pair(n_q_pairs, 1)
        k_ref[rows, 0:hd] = ka
        k_ref[rows, hd:2 * hd] = kb
        v0 = 2 * (n_q_pairs + 1) * hd
        v_ref[rows, :] = qkv[:, v0:v0 + N_KV_HEADS * hd].astype(BF16)

    n_sub = tm // sub_rows
    pending = project(0)
    for i in range(1, n_sub):
        nxt = project(i)
        finish(i - 1, pending)
        pending = nxt
    finish(n_sub - 1, pending)


def _qkv(x, pre, mods, mod_row_fn, w, gains, cos_t, sin_t, *, n_q_pairs, rope, tm, seq):
    rows, d = x.shape
    hd = HEAD_DIM
    n = w.shape[-1]
    kern = functools.partial(_qkv_kernel, n_q_pairs=n_q_pairs, rope=rope, sub_rows=min(tm, 256))
    vec = pl.BlockSpec((1, d), lambda i: (0, 0))
    tiles_per_seq = seq // tm
    pos = pl.BlockSpec((tm, hd), lambda i: (i % tiles_per_seq, 0))
    widths = ([2 * n_q_pairs * hd] if n_q_pairs else []) + [N_KV_HEADS * hd] * 2
    return pl.pallas_call(
        kern,
        grid=(rows // tm,),
        in_specs=[
            pl.BlockSpec((tm, d), lambda i: (i, 0)),
            vec,
            _mod_block(d, 1, mod_row_fn),
            _mod_block(d, 0, mod_row_fn),
            _const_spec((d, n)),
            pl.BlockSpec((4, hd), lambda i: (0, 0)),
            pos, pos,
        ],
        out_specs=[pl.BlockSpec((tm, wd), lambda i: (i, 0)) for wd in widths],
        out_shape=[jax.ShapeDtypeStruct((rows, wd), BF16) for wd in widths],
        compiler_params=_cparams("arbitrary"),
    )(x, pre.reshape(1, d), mods, mods, w, gains, cos_t, sin_t)


def _attn_kernel(q_ref, kl_ref, vl_ref, kc_ref, vc_ref, x_ref, wo_ref, gain_ref, g_ref, o_ref, k_all, v_ext,
                 *, n_pairs, kv_blocks):
    hd = HEAD_DIM
    tq = q_ref.shape[1]
    seq = kl_ref.shape[1]
    total = k_all.shape[0]

    @pl.when(pl.program_id(1) == 0)
    def _():
        k_all[0:seq, :] = kl_ref[0]
        k_all[seq:total, :] = kc_ref[0]
        for h in range(N_KV_HEADS):
            v_ext[h, 0:seq, 0:hd] = vl_ref[0, :, h * hd:(h + 1) * hd]
            v_ext[h, seq:total, 0:hd] = vc_ref[0, :, h * hd:(h + 1) * hd]
            v_ext[h, :, hd:2 * hd] = jnp.ones((total, hd), BF16)

    nt = (((1,), (1,)), ((), ()))
    lane_half = (lax.broadcasted_iota(jnp.int32, (1, 2 * hd), 1) % hd) // (hd // 2)

    def start(kv_head):
        keep = (lane_half == kv_head).astype(BF16)
        q = jnp.concatenate([q_ref[0, :, 2 * p * hd:(2 * p + 2) * hd] * keep for p in range(n_pairs)], axis=0)
        st = dict(h=kv_head, q=q, m=None, acc=None)
        st["s_next"] = lax.dot_general(q, k_all[kv_blocks[0][0]:kv_blocks[0][1], :], nt, preferred_element_type=F32)
        return st

    def step(st, i):
        s0, s1 = kv_blocks[i]
        s = st["s_next"]
        if i + 1 < len(kv_blocks):
            n0, n1 = kv_blocks[i + 1]
            st["s_next"] = lax.dot_general(st["q"], k_all[n0:n1, :], nt, preferred_element_type=F32)
        m, acc = st["m"], st["acc"]
        m_blk = jnp.max(s, axis=-1, keepdims=True)
        m_new = m_blk if m is None else jnp.maximum(m, m_blk)
        pv = _dot(jnp.exp2(s - m_new).astype(BF16), v_ext[st["h"], s0:s1, :])
        st["acc"] = pv if m is None else acc * jnp.exp2(m - m_new) + pv
        st["m"] = m_new

    def heads_of(st):
        acc = st["acc"]
        o = (acc[:, 0:hd] / acc[:, hd:2 * hd]).astype(BF16)
        return [o[p * tq:(p + 1) * tq, :] for p in range(n_pairs)]

    last = len(kv_blocks) - 1
    first, second = start(0), None
    for i in range(last):
        step(first, i)
    second = start(1)
    step(first, last)
    for i in range(last + 1):
        step(second, i)
    attn = jnp.concatenate(heads_of(first) + heads_of(second), axis=1)
    y = _dot(attn, wo_ref[...])
    o_ref[0] = x_ref[0] + g_ref[0, 0] * (_rms(y) * gain_ref[...])


def _kv_blocks(total, pattern):
    assert total % MXU_WIDTH == 0
    n_tiles = total // MXU_WIDTH
    blocks, start, i = [], 0, 0
    while start < n_tiles:
        size = min(pattern[i % len(pattern)], n_tiles - start)
        blocks.append((start * MXU_WIDTH, (start + size) * MXU_WIDTH))
        start += size
        i += 1
    return tuple(blocks)


def _attention(q, k_l, v_l, k_c, v_c, x, w_o, gain, mods, gate_chunk, mod_row_fn, *, tq=512, kv_pattern=(5, 3, 1)):
    bsz, seq, dq = q.shape
    d = x.shape[-1]
    ctx_len = k_c.shape[1]
    hd = HEAD_DIM
    kv = N_KV_HEADS * hd
    n_pairs = dq // (2 * hd)
    total = seq + ctx_len
    kern = functools.partial(_attn_kernel, n_pairs=n_pairs, kv_blocks=_kv_blocks(total, kv_pattern))
    whole = lambda rows, width: pl.BlockSpec((1, rows, width), lambda b, i: (b, 0, 0))
    tile = lambda width: pl.BlockSpec((1, tq, width), lambda b, i: (b, i, 0))
    return pl.pallas_call(
        kern,
        grid=(bsz, seq // tq),
        in_specs=[
            tile(dq), whole(seq, 2 * hd), whole(seq, kv), whole(ctx_len, 2 * hd), whole(ctx_len, kv),
            tile(d),
            _const_spec((dq, d)),
            pl.BlockSpec((1, d), lambda b, i: (0, 0)),
            _mod_block(d, gate_chunk, lambda b, i: mod_row_fn(b)),
        ],
        out_specs=tile(d),
        out_shape=jax.ShapeDtypeStruct((bsz, seq, d), F32),
        scratch_shapes=[pltpu.VMEM((total, 2 * hd), BF16), pltpu.VMEM((N_KV_HEADS, total, 2 * hd), BF16)],
        compiler_params=_cparams("arbitrary", "arbitrary"),
    )(q, k_l, v_l, k_c, v_c, x, w_o, gain.reshape(1, d), mods)


def _rope_tables(seq):
    rows = seq // GRID_W
    pairs = HEAD_DIM // 4
    row = jnp.repeat(jnp.arange(rows, dtype=F32), GRID_W)
    col = jnp.tile(jnp.arange(GRID_W, dtype=F32), rows)
    inv = ROPE_THETA ** (-jnp.arange(pairs, dtype=F32) / pairs)
    ang = jnp.concatenate([row[:, None] * inv[None, :], col[:, None] * inv[None, :]], axis=-1)
    cos, sin = jnp.cos(ang), jnp.sin(ang)
    return jnp.concatenate([cos, cos], axis=-1), jnp.concatenate([sin, sin], axis=-1)


def _pair_columns(head_a, head_b):
    hd = HEAD_DIM
    even, odd = jnp.arange(0, hd, 2), jnp.arange(1, hd, 2)
    return jnp.concatenate([head_a * hd + even, head_b * hd + even, head_a * hd + odd, head_b * hd + odd])


def kernel(x, c, ctx, c_ctx, mod_w, mod_b, mix_norm_pre, mix_norm_post, mlp_norm_pre, mlp_norm_post, mlp_w1, mlp_w2, hy_w_in, hy_b_in, hy_conv_w, hy_conv_b, hy_filt_w1, hy_filt_b1, hy_filt_freq1, hy_filt_w2, hy_filt_b2, hy_filt_freq2, hy_filt_w3, hy_filt_bias, hy_w_out, hy_b_out, attn_w_qkv, attn_q_norm, attn_k_norm, attn_w_o):
    bsz, seq, d = x.shape
    ctx_len = ctx.shape[1]
    hd = HEAD_DIM
    n_heads = d // hd
    tm = 1024

    ctx_row = bsz
    n_rows = -(-(bsz + 1) // MOD_ROWS_PAD) * MOD_ROWS_PAD
    cond = jnp.concatenate([c, c_ctx[None, :], jnp.zeros((n_rows - bsz - 1, d), F32)], axis=0)
    mods_all = _modulation(cond, mod_w, mod_b).reshape(mod_w.shape[0], n_rows, 6, 1, d)

    x_row = lambda i: i // (seq // tm)
    c_row = lambda i: ctx_row

    xf = x.reshape(bsz * seq, d)
    cf = ctx.reshape(bsz * ctx_len, d)

    mods = mods_all[0]
    w_in = hy_w_in[0].astype(BF16)
    w_out = hy_w_out[0].astype(BF16)
    filt = (hy_filt_w1[0], hy_filt_b1[0], hy_filt_freq1[0], hy_filt_w2[0], hy_filt_b2[0], hy_filt_freq2[0],
            hy_filt_w3[0], hy_filt_bias[0])
    w1 = _to_bf16(mlp_w1)
    w2 = _to_bf16(mlp_w2)

    def hyena_conv(tokens, mod_row_b):
        u, x1 = _hyena_in(tokens, mix_norm_pre[0], mods, mod_row_b, w_in, hy_b_in[0], hy_conv_w[0], hy_conv_b[0])
        return _fftconv(u, x1, _hyena_filter_tables(tokens.shape[1], *filt))

    x3 = xf.reshape(bsz, seq, d)
    xf = _hyena_out_mlp(hyena_conv(x3, lambda b: b), x3, w_out, hy_b_out[0], mix_norm_post[0],
                        mlp_norm_pre[0], mlp_norm_post[0], mods, lambda b: b, w1, w2, 0).reshape(bsz * seq, d)
    c3 = cf.reshape(bsz, ctx_len, d)
    cf = _hyena_out_mlp(hyena_conv(c3, lambda b: ctx_row), c3, w_out, hy_b_out[0], mix_norm_post[0],
                        mlp_norm_pre[0], mlp_norm_post[0], mods, lambda b: ctx_row, w1, w2, 0
                        ).reshape(bsz * ctx_len, d)

    mods = mods_all[1]
    w_qkv = attn_w_qkv[0]
    assert N_KV_HEADS == 2 and n_heads % 2 == 0
    group = n_heads // N_KV_HEADS
    pair_cols = [_pair_columns(p, group + p) for p in range(group)] + [_pair_columns(n_heads, n_heads + 1)]
    v_cols = jnp.arange((n_heads + N_KV_HEADS) * hd, (n_heads + 2 * N_KV_HEADS) * hd)
    w_lat = w_qkv[:, jnp.concatenate(pair_cols + [v_cols])].astype(BF16)
    w_ctx = w_lat[:, n_heads * hd:]
    even, odd = jnp.arange(0, hd, 2), jnp.arange(1, hd, 2)
    gains = jnp.stack([jnp.tile(g[idx], 2) for g in (attn_q_norm[0], attn_k_norm[0]) for idx in (even, odd)])
    cos_t, sin_t = _rope_tables(seq)

    q, k_l, v_l = _qkv(xf, mix_norm_pre[1], mods, x_row, w_lat, gains, cos_t, sin_t,
                       n_q_pairs=group, rope=True, tm=tm, seq=seq)
    k_c, v_c = _qkv(cf, mix_norm_pre[1], mods, c_row, w_ctx, gains, cos_t, sin_t,
                    n_q_pairs=0, rope=False, tm=min(tm, bsz * ctx_len), seq=min(tm, bsz * ctx_len))
    kv = N_KV_HEADS * hd
    xf = _attention(q.reshape(bsz, seq, d), k_l.reshape(bsz, seq, kv), v_l.reshape(bsz, seq, kv),
                    k_c.reshape(bsz, ctx_len, kv), v_c.reshape(bsz, ctx_len, kv), xf.reshape(bsz, seq, d),
                    attn_w_o[0].astype(BF16), mix_norm_post[1], mods, 2, lambda b: b).reshape(bsz * seq, d)
    xf = _mlp(xf, mlp_norm_pre[1], mlp_norm_post[1], mods, x_row, w1, w2, 1, tm=tm)
    return xf.reshape(bsz, seq, d)
```
